```python
import math
import jax
import jax.numpy as jnp
from jax import lax
import numpy as np

D_MODEL = 4096
BATCH = 4
SEQ = 2048
DEPTH = 2
DEC_BATCH = 8
DEC_SEQ = 4
PAST_LEN = 16384
PAGE_SIZE = 128

N_BRANCH = 4
HEAD_DIM = 128
N_HEADS = D_MODEL // HEAD_DIM // N_BRANCH
N_KV = 2
GROUP = N_HEADS // N_KV
MIX_W = N_HEADS * HEAD_DIM
KV_W = 2 * N_KV * HEAD_DIM
DIFF_DH = HEAD_DIM // 2
SCALE = HEAD_DIM ** -0.5
DIFF_SCALE = DIFF_DH ** -0.5
QB = 128
GATHER_QB = 32
FORGET_BIAS_INIT = 3.0
MOBA_BLOCK = 256
MOBA_TOPK = 3
NSA_CMP_LEN = 32
NSA_CMP_STRIDE = 16
NSA_SLC_BLOCK = 64
NSA_TOPN = 16
NSA_N_LOCAL = 2
NSA_WINDOW = 512
NSA_FORCE = 1e9
N_BUCKETS = 32
MAX_DISTANCE = 4096
N_EXPERTS = 32
TOP_K = 4
D_EXPERT = D_MODEL // 4
SWIGLU_ALPHA = 1.702
SWIGLU_LIMIT = 7.0
MOE_MAX_BLOCK = 256
MOE_MIN_BLOCK = 8
DN_ALPHA = (2 * DEPTH) ** 0.25
DN_BETA = (8 * DEPTH) ** -0.25
LN_EPS = 1e-5
IN_SPLITS = (MIX_W, KV_W, N_HEADS, MIX_W, KV_W, MIX_W, KV_W, MIX_W, KV_W, KV_W, KV_W, 3 * N_HEADS)
D_IN = sum(IN_SPLITS)
FORGET_OFF = MIX_W + KV_W
STATE_NAMES = ("fox_kv", "fox_logf", "diff_kv", "moba_kv", "nsa_cmp_kv", "nsa_slc_kv", "nsa_win_kv")

kernel_name = "hybrid_fox_diff_moba_nsa_moe_step"


def layer_norm(x, g, b):
    xf = x.astype(jnp.float32)
    mu = jnp.mean(xf, -1, keepdims=True)
    var = jnp.mean(jnp.square(xf - mu), -1, keepdims=True)
    return ((xf - mu) * lax.rsqrt(var + LN_EPS) * g + b).astype(x.dtype)


def masked_softmax(s, mask):
    s = jnp.where(mask, s.astype(jnp.float32), -jnp.inf)
    m = jnp.max(s, -1, keepdims=True)
    m = jnp.where(jnp.isfinite(m), m, 0.0)
    p = jnp.exp(s - m)
    z = jnp.sum(p, -1, keepdims=True)
    return p / jnp.where(z > 0, z, 1.0)


def t5_bucket(dist):
    n = jnp.maximum(dist, 0)
    exact = N_BUCKETS // 2
    nf = jnp.maximum(n, 1).astype(jnp.float32)
    large = exact + (jnp.log(nf / exact) / math.log(MAX_DISTANCE / exact) * (N_BUCKETS - exact)).astype(jnp.int32)
    return jnp.where(n < exact, n, jnp.minimum(large, N_BUCKETS - 1))


def rel_bias_heads(table, dist):
    b = table[t5_bucket(dist)]
    return jnp.moveaxis(b, -1, 0).reshape(N_KV, GROUP, *dist.shape)


def sweep(fn, arrays, blk):
    T = arrays[0].shape[1]
    nb = -(-T // blk)
    pad = nb * blk - T

    def prep(a):
        a = jnp.pad(a, [(0, 0), (0, pad)] + [(0, 0)] * (a.ndim - 2))
        return jnp.moveaxis(a.reshape(a.shape[0], nb, blk, *a.shape[2:]), 1, 0)

    starts = jnp.arange(nb, dtype=jnp.int32) * blk
    out = lax.map(lambda sa: fn(sa[0], *sa[1]), (starts, tuple(prep(a) for a in arrays)))
    out = jnp.moveaxis(out, 0, 1)
    out = out.reshape(out.shape[0], nb * blk, *out.shape[3:])
    return out[:, :T]


def gather_pages(pool, page_table):
    g = pool[page_table]
    return g.reshape(g.shape[0], g.shape[1] * g.shape[2], *g.shape[3:])


def fox_attention(q, k, v, F_q, F_k):
    B, T = q.shape[:2]
    Lk = k.shape[1]
    q0 = Lk - T
    kpos = jnp.arange(Lk)
    fk = jnp.moveaxis(F_k.reshape(B, Lk, N_KV, GROUP), 1, -1)[:, :, :, None, :]

    def block(start, qb, fb):
        n = qb.shape[1]
        qpos = q0 + start + jnp.arange(n)
        s = jnp.einsum("bqkgd,bskd->bkgqs", qb.reshape(B, n, N_KV, GROUP, HEAD_DIM), k).astype(jnp.float32) * SCALE
        fq = jnp.moveaxis(fb.reshape(B, n, N_KV, GROUP), 1, -1)[..., None]
        p = masked_softmax(s + fq - fk, kpos[None, :] <= qpos[:, None])
        o = jnp.einsum("bkgqs,bskd->bqkgd", p.astype(v.dtype), v)
        return o.reshape(B, n, MIX_W)

    return sweep(block, (q, F_q), min(QB, T))


def diff_attention(q, k, v, lam, lam_init, g_norm, table):
    B, T = q.shape[:2]
    Lk = k.shape[1]
    q0 = Lk - T
    kpos = jnp.arange(Lk)

    def block(start, qb):
        n = qb.shape[1]
        qpos = q0 + start + jnp.arange(n)
        s = jnp.einsum("bqkgjd,bskjd->bkgjqs", qb.reshape(B, n, N_KV, GROUP, 2, DIFF_DH), k).astype(jnp.float32) * DIFF_SCALE
        bias = rel_bias_heads(table, qpos[:, None] - kpos[None, :])[:, :, None]
        p = masked_softmax(s + bias, kpos[None, :] <= qpos[:, None])
        a = p[:, :, :, 0] - lam * p[:, :, :, 1]
        o = jnp.einsum("bkgqs,bskd->bqkgd", a.astype(v.dtype), v).astype(jnp.float32)
        o = o * lax.rsqrt(jnp.mean(jnp.square(o), -1, keepdims=True) + LN_EPS) * g_norm * (1.0 - lam_init)
        return o.reshape(B, n, MIX_W)

    return sweep(block, (q,), min(QB, T))


def moba_attention(q, k, v, table):
    B, T = q.shape[:2]
    Lk = k.shape[1]
    q0 = Lk - T
    BS = MOBA_BLOCK
    nblk = -(-Lk // BS)
    padk = ((0, 0), (0, nblk * BS - Lk), (0, 0), (0, 0))
    kb = jnp.pad(k, padk).reshape(B, nblk, BS, N_KV, HEAD_DIM).transpose(0, 3, 1, 2, 4)
    vb = jnp.pad(v, padk).reshape(B, nblk, BS, N_KV, HEAD_DIM).transpose(0, 3, 1, 2, 4)
    kmean = jnp.mean(kb.astype(jnp.float32), axis=3)
    ksel = min(MOBA_TOPK, nblk)
    tab = table.reshape(N_BUCKETS, N_KV, GROUP)
    bi = jnp.arange(B)[:, None, None, None]
    ki = jnp.arange(N_KV)[None, :, None, None]
    ki5 = ki[..., None]
    gi5 = jnp.arange(GROUP)[None, None, :, None, None]
    blk_ids = jnp.arange(nblk)
    offs = jnp.arange(BS)

    def block(start, qb):
        n = qb.shape[1]
        qpos = q0 + start + jnp.arange(n)
        own = qpos // BS
        qh = jnp.moveaxis(qb.reshape(B, n, N_KV, GROUP, HEAD_DIM), 1, 3)
        gs = jnp.einsum("bkgnd,bkmd->bkgnm", qh.astype(jnp.float32), kmean)
        gs = jnp.where(blk_ids[None, :] < own[:, None], gs, -jnp.inf)
        top_v, top_i = lax.top_k(gs, ksel)
        own_c = jnp.minimum(own, nblk - 1)
        k_own = kb[:, :, own_c]
        v_own = vb[:, :, own_c]
        d_own = qpos[:, None] - (own[:, None] * BS + offs)
        s0 = jnp.einsum("bkgnd,bknsd->bkgns", qh, k_own).astype(jnp.float32) * SCALE + rel_bias_heads(table, d_own)
        s_list = [s0]
        m_list = [jnp.broadcast_to(d_own >= 0, s0.shape)]
        for j in range(ksel):
            idx = top_i[..., j]
            kg = kb[bi, ki, idx]
            d_j = qpos[:, None] - (idx[..., None] * BS + offs)
            s_j = jnp.einsum("bkgnd,bkgnsd->bkgns", qh, kg).astype(jnp.float32) * SCALE + tab[t5_bucket(d_j), ki5, gi5]
            s_list.append(s_j)
            m_list.append(jnp.broadcast_to(jnp.isfinite(top_v[..., j])[..., None], s_j.shape))
        p = masked_softmax(jnp.concatenate(s_list, -1), jnp.concatenate(m_list, -1))
        o = jnp.einsum("bkgns,bknsd->bkgnd", p[..., :BS].astype(v.dtype), v_own)
        for j in range(ksel):
            vg = vb[bi, ki, top_i[..., j]]
            o = o + jnp.einsum("bkgns,bkgnsd->bkgnd", p[..., (j + 1) * BS:(j + 2) * BS].astype(v.dtype), vg)
        return jnp.moveaxis(o, 3, 1).reshape(B, n, MIX_W)

    return sweep(block, (q,), min(GATHER_QB, T))


def nsa_attention(q, kv_cmp, kv_slc, gates, w_cmp_l, table):
    B, T = q.shape[:2]
    Lk = kv_cmp.shape[1]
    q0 = Lk - T
    S, L, BS = NSA_CMP_STRIDE, NSA_CMP_LEN, NSA_SLC_BLOCK
    R = L // S
    n_chunk = Lk // S
    n_cmp = n_chunk - R + 1
    chunks = kv_cmp[:, :n_chunk * S].reshape(B, n_chunk, S, 2, N_KV, HEAD_DIM)
    kvc = sum(jnp.einsum("bnsjkd,jsde->bjkne", chunks[:, r:r + n_cmp], w_cmp_l[:, r * S:(r + 1) * S]) for r in range(R))
    kc, vc = kvc[:, 0], kvc[:, 1]
    cmp_end = jnp.arange(n_cmp) * S + (L - 1)
    n_slc = -(-Lk // BS)
    c_start = np.arange(n_cmp) * S
    s_ids = np.arange(n_slc)
    cover = jnp.asarray(((c_start[:, None] < (s_ids[None, :] + 1) * BS) & (c_start[:, None] + L > s_ids[None, :] * BS)).astype(np.float32))
    sb = jnp.pad(kv_slc, ((0, 0), (0, n_slc * BS - Lk), (0, 0), (0, 0), (0, 0))).reshape(B, n_slc, BS, 2, N_KV, HEAD_DIM)
    k_blk = sb[:, :, :, 0].transpose(0, 3, 1, 2, 4)
    v_blk = sb[:, :, :, 1].transpose(0, 3, 1, 2, 4)
    n_top = min(NSA_TOPN, n_slc)
    tab = table.reshape(N_BUCKETS, N_KV, GROUP)
    bi = jnp.arange(B)[:, None, None, None]
    ki = jnp.arange(N_KV)[None, :, None, None]
    blk_ids = jnp.arange(n_slc)
    offs = jnp.arange(BS)

    def block(start, qb, gb):
        n = qb.shape[1]
        qpos = q0 + start + jnp.arange(n)
        qh = jnp.moveaxis(qb.reshape(B, n, N_KV, GROUP, HEAD_DIM), 1, 3)
        sc = jnp.einsum("bkgnd,bkmd->bkgnm", qh, kc).astype(jnp.float32) * SCALE
        sc = sc + rel_bias_heads(table, qpos[:, None] - cmp_end[None, :])
        pc = masked_softmax(sc, cmp_end[None, :] <= qpos[:, None])
        o_cmp = jnp.einsum("bkgnm,bkmd->bkgnd", pc.astype(vc.dtype), vc)
        imp = jnp.einsum("bkgnm,mj->bknj", pc, cover)
        cur = (qpos // BS)[:, None]
        forced = (blk_ids[None, :] == 0) | ((blk_ids[None, :] <= cur) & (blk_ids[None, :] > cur - NSA_N_LOCAL))
        imp = jnp.where(forced, NSA_FORCE, jnp.where(blk_ids[None, :] <= cur, imp, -1.0))
        top_v, top_i = lax.top_k(imp, n_top)
        kg = k_blk[bi, ki, top_i]
        vg = v_blk[bi, ki, top_i]
        dist = qpos[:, None, None] - (top_i[..., None] * BS + offs)
        bias = jnp.moveaxis(tab[t5_bucket(dist), ki[..., None]], -1, 2)
        ss = jnp.einsum("bkgnd,bkntsd->bkgnts", qh, kg).astype(jnp.float32) * SCALE + bias
        mask = ((top_v >= 0)[..., None] & (dist >= 0))[:, :, None]
        ps = masked_softmax(ss.reshape(B, N_KV, GROUP, n, n_top * BS), mask.reshape(B, N_KV, 1, n, n_top * BS))
        o_slc = jnp.einsum("bkgnt,bkntd->bkgnd", ps.astype(vg.dtype), vg.reshape(B, N_KV, n, n_top * BS, HEAD_DIM))
        g = jnp.moveaxis(gb.reshape(B, n, 2, N_KV, GROUP), 1, -1)[..., None]
        o = g[:, 0] * o_cmp + g[:, 1] * o_slc
        return jnp.moveaxis(o, 3, 1).reshape(B, n, MIX_W)

    return sweep(block, (q, gates), min(GATHER_QB, T))


def window_attention(q, k, v, table):
    B, T = q.shape[:2]
    Lw = k.shape[1]
    W = NSA_WINDOW
    blk = min(QB, T)
    tp = -(-T // blk) * blk
    padw = ((0, 0), (W, tp - T), (0, 0), (0, 0))
    kp = jnp.pad(k, padw)
    vp = jnp.pad(v, padw)
    base = Lw - T

    def block(start, qb):
        n = qb.shape[1]
        ks = lax.dynamic_slice_in_dim(kp, base + start, W + n, axis=1)
        vs = lax.dynamic_slice_in_dim(vp, base + start, W + n, axis=1)
        kidx = base + start - W + jnp.arange(W + n)
        qidx = base + start + jnp.arange(n)
        dist = qidx[:, None] - kidx[None, :]
        mask = (kidx >= 0)[None, :] & (dist >= 0) & (dist <= W)
        s = jnp.einsum("bqkgd,bskd->bkgqs", qb.reshape(B, n, N_KV, GROUP, HEAD_DIM), ks).astype(jnp.float32) * SCALE
        p = masked_softmax(s + rel_bias_heads(table, dist), mask)
        o = jnp.einsum("bkgqs,bskd->bqkgd", p.astype(vs.dtype), vs)
        return o.reshape(B, n, MIX_W)

    return sweep(block, (q,), blk)


def token_mixers(l, h, P, past):
    B, T, _ = h.shape
    proj = h @ P["w_in"][l] + P["b_in"][l]
    (q_fox, kv_fox, f_fox, q_diff, kv_diff, q_moba, kv_moba, q_nsa, kv_cmp, kv_slc, kv_win, g_nsa) = jnp.split(
        proj, np.cumsum(IN_SPLITS)[:-1].tolist(), axis=-1)
    kvs = lambda a: a.reshape(B, T, 2, N_KV, HEAD_DIM)
    new = dict(fox_kv=kvs(kv_fox), fox_logf=jax.nn.log_sigmoid(f_fox.astype(jnp.float32)),
               diff_kv=kvs(kv_diff), moba_kv=kvs(kv_moba), nsa_cmp_kv=kvs(kv_cmp), nsa_slc_kv=kvs(kv_slc))
    win_new = kvs(kv_win)
    if past is None:
        full = new
        win = win_new
    else:
        full = {n: jnp.concatenate([past[n], new[n]], axis=1) for n in new}
        win = jnp.concatenate([past["nsa_win_kv"], win_new], axis=1)

    F = jnp.cumsum(full["fox_logf"].astype(jnp.float32), axis=1)
    Lk = F.shape[1]
    fkv = full["fox_kv"]
    o_fox = fox_attention(q_fox.reshape(B, T, N_HEADS, HEAD_DIM), fkv[:, :, 0], fkv[:, :, 1], F[:, Lk - T:], F)

    dl = P["diff_lambda"][l].astype(jnp.float32)
    lam_init = 0.8 - 0.6 * math.exp(-0.3 * l)
    lam = jnp.exp(jnp.sum(dl[0] * dl[1])) - jnp.exp(jnp.sum(dl[2] * dl[3])) + lam_init
    dkv = full["diff_kv"]
    o_diff = diff_attention(q_diff.reshape(B, T, N_HEADS, 2, DIFF_DH), dkv[:, :, 0].reshape(B, Lk, N_KV, 2, DIFF_DH),
                            dkv[:, :, 1], lam, lam_init, P["diff_norm_g"][l], P["rel_bias"][:, 0])

    mkv = full["moba_kv"]
    o_moba = moba_attention(q_moba.reshape(B, T, N_HEADS, HEAD_DIM), mkv[:, :, 0], mkv[:, :, 1], P["rel_bias"][:, 1])

    gates = jax.nn.sigmoid(g_nsa.astype(jnp.float32)).reshape(B, T, 3, N_HEADS)
    qn = q_nsa.reshape(B, T, N_HEADS, HEAD_DIM)
    o_cs = nsa_attention(qn, full["nsa_cmp_kv"], full["nsa_slc_kv"], gates[:, :, :2], P["w_cmp"][l], P["rel_bias"][:, 2])
    o_win = window_attention(qn, win[:, :, 0], win[:, :, 1], P["rel_bias"][:, 2])
    o_nsa = o_cs + (gates[:, :, 2, :, None] * o_win.reshape(B, T, N_HEADS, HEAD_DIM)).reshape(B, T, MIX_W)

    merged = 0.0
    for b_i, o_b in enumerate((o_fox, o_diff, o_moba, o_nsa)):
        gate = jax.nn.sigmoid(h @ P["w_gate"][l, b_i] + P["b_gate"][l, b_i])
        merged = merged + gate * (o_b.astype(h.dtype) @ P["w_branch"][l, b_i])
    y = merged @ P["w_o"][l]
    rows = dict(new)
    rows["nsa_win_kv"] = win[:, -min(NSA_WINDOW, win.shape[1]):]
    return y.astype(h.dtype), rows


def moe_ffn(l, h, P):
    B, T, D = h.shape
    n_tok = B * T
    xs = h.reshape(n_tok, D)
    logits = (xs @ P["w_router"][l] + P["b_router"][l]).astype(jnp.float32)
    top_v, top_i = lax.top_k(logits, TOP_K)
    gates = jax.nn.softmax(top_v, axis=-1)
    n_asg = n_tok * TOP_K
    blk = max(MOE_MIN_BLOCK, min(MOE_MAX_BLOCK, 1 << max(0, (-(-n_asg // N_EXPERTS) - 1).bit_length())))
    e_flat = top_i.reshape(-1)
    onehot = jax.nn.one_hot(e_flat, N_EXPERTS, dtype=jnp.int32)
    rank = jnp.take_along_axis(jnp.cumsum(onehot, axis=0), e_flat[:, None], axis=1)[:, 0] - 1
    counts = jnp.sum(onehot, axis=0)
    padded = (counts + blk - 1) // blk * blk
    pad_end = jnp.cumsum(padded)
    dest = pad_end[e_flat] - padded[e_flat] + rank
    n_rows = -(-n_asg // blk) * blk + N_EXPERTS * blk
    n_blocks = n_rows // blk
    row_tok = jnp.full((n_rows,), n_tok, jnp.int32).at[dest].set(jnp.arange(n_asg, dtype=jnp.int32) // TOP_K)
    row_gate = jnp.zeros((n_rows,), jnp.float32).at[dest].set(gates.reshape(-1))
    blk_expert = jnp.minimum(jnp.searchsorted(pad_end, jnp.arange(n_blocks, dtype=jnp.int32) * blk, side="right"), N_EXPERTS - 1)
    x_rows = jnp.concatenate([xs, jnp.zeros((1, D), xs.dtype)], 0)[row_tok].reshape(n_blocks, blk, D)
    w_gu, b_gu, w_dn, b_dn = P["w_gu"], P["b_gu"], P["w_dn"], P["b_dn"]

    def expert_block(args):
        xb, e = args
        gu = xb @ w_gu[l, e] + b_gu[l, e]
        gate = jnp.minimum(gu[:, :D_EXPERT], SWIGLU_LIMIT)
        up = jnp.clip(gu[:, D_EXPERT:], -SWIGLU_LIMIT, SWIGLU_LIMIT)
        act = (up + 1.0) * gate * jax.nn.sigmoid(SWIGLU_ALPHA * gate)
        return act @ w_dn[l, e] + b_dn[l, e]

    y_rows = lax.map(expert_block, (x_rows, blk_expert)).reshape(n_rows, D)
    y = jnp.zeros((n_tok + 1, D), jnp.float32).at[row_tok].add(y_rows * row_gate[:, None])
    return y[:n_tok].reshape(B, T, D).astype(h.dtype)


def run_group(x, c, P, past_fn):
    B, T, D = x.shape
    rows = {n: [] for n in STATE_NAMES}
    for l in range(DEPTH):
        mod = (jax.nn.silu(c) @ P["w_ada"][l] + P["b_ada"][l]).astype(x.dtype).reshape(B, 6, 1, D)
        h = x * (1.0 + mod[:, 1]) + mod[:, 0]
        a, new_rows = token_mixers(l, h, P, past_fn(l))
        x = layer_norm(DN_ALPHA * x + (1.0 + mod[:, 2]) * a, P["ln_g"][l, 0], P["ln_b"][l, 0])
        h = x * (1.0 + mod[:, 4]) + mod[:, 3]
        f = moe_ffn(l, h, P)
        x = layer_norm(DN_ALPHA * x + (1.0 + mod[:, 5]) * f, P["ln_g"][l, 1], P["ln_b"][l, 1])
        for n in STATE_NAMES:
            rows[n].append(new_rows[n])
    return x, {n: jnp.stack(rows[n]) for n in STATE_NAMES}


def setup_inputs(seed: int = 0) -> dict:
    key = jax.random.key(seed)
    keys = iter(jax.random.split(key, 40))

    def nrm(shape, scale):
        return jax.random.normal(next(keys), shape, jnp.float32) * scale

    n_pages = PAST_LEN // PAGE_SIZE
    n_used = DEC_BATCH * n_pages
    n_phys = n_used + max(1, n_used // 4)
    page_table = jax.random.permutation(next(keys), n_phys)[:n_used].reshape(DEC_BATCH, n_pages).astype(jnp.int32)
    kv_pool = (DEPTH, n_phys, PAGE_SIZE, 2, N_KV, HEAD_DIM)
    w_buf = min(NSA_WINDOW, PAST_LEN)
    b_in = nrm((DEPTH, D_IN), 0.02).at[:, FORGET_OFF:FORGET_OFF + N_HEADS].add(FORGET_BIAS_INIT)
    return {
        "x_prompt": nrm((BATCH, SEQ, D_MODEL), 1.0),
        "x_sample": nrm((DEC_BATCH, DEC_SEQ, D_MODEL), 1.0),
        "cache_fox_kv": nrm(kv_pool, 1.0),
        "cache_fox_logf": jax.nn.log_sigmoid(FORGET_BIAS_INIT + nrm((DEPTH, n_phys, PAGE_SIZE, N_HEADS), 1.0)),
        "cache_diff_kv": nrm(kv_pool, 1.0),
        "cache_moba_kv": nrm(kv_pool, 1.0),
        "cache_nsa_cmp_kv": nrm(kv_pool, 1.0),
        "cache_nsa_slc_kv": nrm(kv_pool, 1.0),
        "cache_nsa_win_kv": nrm((DEPTH, DEC_BATCH, w_buf, 2, N_KV, HEAD_DIM), 1.0),
        "page_table": page_table,
        "c_prompt": nrm((BATCH, D_MODEL), 1.0),
        "c_sample": nrm((DEC_BATCH, D_MODEL), 1.0),
        "w_ada": nrm((DEPTH, D_MODEL, 6 * D_MODEL), 0.2 * D_MODEL ** -0.5),
        "b_ada": nrm((DEPTH, 6 * D_MODEL), 0.02),
        "ln_g": 1.0 + nrm((DEPTH, 2, D_MODEL), 0.02),
        "ln_b": nrm((DEPTH, 2, D_MODEL), 0.02),
        "w_in": nrm((DEPTH, D_MODEL, D_IN), D_MODEL ** -0.5),
        "b_in": b_in,
        "diff_lambda": nrm((DEPTH, 4, DIFF_DH), 0.1),
        "diff_norm_g": 1.0 + nrm((DEPTH, HEAD_DIM), 0.02),
        "rel_bias": nrm((N_BUCKETS, 3, N_HEADS), 0.2),
        "w_cmp": nrm((DEPTH, 2, NSA_CMP_LEN, HEAD_DIM, HEAD_DIM), (NSA_CMP_LEN * HEAD_DIM) ** -0.5),
        "w_gate": nrm((DEPTH, N_BRANCH, D_MODEL, D_MODEL), D_MODEL ** -0.5),
        "b_gate": nrm((DEPTH, N_BRANCH, D_MODEL), 0.02),
        "w_branch": nrm((DEPTH, N_BRANCH, MIX_W, D_MODEL), MIX_W ** -0.5),
        "w_o": nrm((DEPTH, D_MODEL, D_MODEL), DN_BETA * D_MODEL ** -0.5),
        "w_router": nrm((DEPTH, D_MODEL, N_EXPERTS), D_MODEL ** -0.5),
        "b_router": nrm((DEPTH, N_EXPERTS), 0.01),
        "w_gu": nrm((DEPTH, N_EXPERTS, D_MODEL, 2 * D_EXPERT), D_MODEL ** -0.5),
        "b_gu": nrm((DEPTH, N_EXPERTS, 2 * D_EXPERT), 0.02),
        "w_dn": nrm((DEPTH, N_EXPERTS, D_EXPERT, D_MODEL), DN_BETA * D_EXPERT ** -0.5),
        "b_dn": nrm((DEPTH, N_EXPERTS, D_MODEL), 0.02),
    }


def reference(x_prompt, x_sample, cache_fox_kv, cache_fox_logf, cache_diff_kv, cache_moba_kv, cache_nsa_cmp_kv,
              cache_nsa_slc_kv, cache_nsa_win_kv, page_table, c_prompt, c_sample, w_ada, b_ada, ln_g, ln_b, w_in, b_in,
              diff_lambda, diff_norm_g, rel_bias, w_cmp, w_gate, b_gate, w_branch, w_o, w_router, b_router,
              w_gu, b_gu, w_dn, b_dn):
    P = dict(w_ada=w_ada, b_ada=b_ada, ln_g=ln_g, ln_b=ln_b, w_in=w_in, b_in=b_in, diff_lambda=diff_lambda,
             diff_norm_g=diff_norm_g, rel_bias=rel_bias, w_cmp=w_cmp, w_gate=w_gate, b_gate=b_gate,
             w_branch=w_branch, w_o=w_o, w_router=w_router, b_router=b_router, w_gu=w_gu, b_gu=b_gu,
             w_dn=w_dn, b_dn=b_dn)

    def sample_past(l):
        return dict(
            fox_kv=gather_pages(cache_fox_kv[l], page_table),
            fox_logf=gather_pages(cache_fox_logf[l], page_table),
            diff_kv=gather_pages(cache_diff_kv[l], page_table),
            moba_kv=gather_pages(cache_moba_kv[l], page_table),
            nsa_cmp_kv=gather_pages(cache_nsa_cmp_kv[l], page_table),
            nsa_slc_kv=gather_pages(cache_nsa_slc_kv[l], page_table),
            nsa_win_kv=cache_nsa_win_kv[l])

    y_prompt, sp = run_group(x_prompt, c_prompt, P, lambda l: None)
    y_sample, ss = run_group(x_sample, c_sample, P, sample_past)
    return (y_prompt, y_sample,
            sp["fox_kv"], sp["fox_logf"], sp["diff_kv"], sp["moba_kv"], sp["nsa_cmp_kv"], sp["nsa_slc_kv"], sp["nsa_win_kv"],
            ss["fox_kv"], ss["fox_logf"], ss["diff_kv"], ss["moba_kv"], ss["nsa_cmp_kv"], ss["nsa_slc_kv"], ss["nsa_win_kv"])
```

```python
import functools
import math

import jax
import jax.numpy as jnp
import numpy as np
from jax import lax
from jax.experimental import pallas as pl
from jax.experimental.pallas import tpu as pltpu

D_MODEL = 4096
DEPTH = 2
PAGE_SIZE = 128
HEAD_DIM = 128
N_HEADS = 8
N_KV = 2
GROUP = N_HEADS // N_KV
MIX_W = N_HEADS * HEAD_DIM
KV_W = 2 * N_KV * HEAD_DIM
DIFF_DH = HEAD_DIM // 2
SCALE = HEAD_DIM ** -0.5
DIFF_SCALE = DIFF_DH ** -0.5
MOBA_BLOCK = 256
MOBA_TOPK = 3
NSA_CMP_LEN = 32
NSA_CMP_STRIDE = 16
NSA_SLC_BLOCK = 64
NSA_TOPN = 16
NSA_N_LOCAL = 2
NSA_WINDOW = 512
NSA_FORCE = 1e9
N_BUCKETS = 32
MAX_DISTANCE = 4096
N_EXPERTS = 32
TOP_K = 4
D_EXPERT = D_MODEL // 4
SWIGLU_ALPHA = 1.702
SWIGLU_LIMIT = 7.0
DN_ALPHA = (2 * DEPTH) ** 0.25
LN_EPS = 1e-5
IN_SPLITS = (MIX_W, KV_W, N_HEADS, MIX_W, KV_W, MIX_W, KV_W, MIX_W, KV_W, KV_W, KV_W, 3 * N_HEADS)
D_IN = sum(IN_SPLITS)

LANE = 128
QF, KVF, QD, KVD, QM, KVM, QN, KVC, KVS, KVW, MISC = 0, 8, 12, 20, 24, 32, 36, 44, 48, 52, 56
PROJ_W = 60 * LANE
NEG = -1e30
NEG_TEST = -1e29
TB = 512
VMEM_LIMIT = 56 * 1024 * 1024
BF = jnp.bfloat16
F32 = jnp.float32


def _cparams(n_axes):
    return pltpu.CompilerParams(dimension_semantics=("arbitrary",) * n_axes, vmem_limit_bytes=VMEM_LIMIT)


def _dot(a, b):
    return jnp.dot(a, b, preferred_element_type=F32)


def _dot_nt(a, b):
    return lax.dot_general(a, b, (((1,), (1,)), ((), ())), preferred_element_type=F32)


def _dot_hi(a, b):
    return jnp.dot(a, b, precision=lax.Precision.HIGHEST, preferred_element_type=F32)


def _sigmoid(x):
    return 1.0 / (1.0 + jnp.exp(-x))


def _cast_rows(src_ref, dst_ref, rows, chunk=256):
    def body(r, c):
        sl = pl.ds(pl.multiple_of(r * chunk, chunk), chunk)
        dst_ref[sl, :] = src_ref[sl, :].astype(BF)
        return c
    lax.fori_loop(0, rows // chunk, body, 0)


def _ada_kernel(c_ref, w_ref, b_ref, o_ref):
    c = c_ref[...]
    a = (c * _sigmoid(c)).astype(BF)
    o_ref[...] = _dot(a, w_ref[...].astype(BF)) + b_ref[...]


def _ada_mod(c_all, w_ada, b_ada):
    r = c_all.shape[0]
    tn = 512
    n6 = 6 * D_MODEL
    return pl.pallas_call(
        _ada_kernel,
        grid=(DEPTH, n6 // tn),
        in_specs=[pl.BlockSpec((r, D_MODEL), lambda l, n: (0, 0)),
                  pl.BlockSpec((None, D_MODEL, tn), lambda l, n: (l, 0, n)),
                  pl.BlockSpec((None, 1, tn), lambda l, n: (l, 0, n))],
        out_specs=pl.BlockSpec((None, r, tn), lambda l, n: (l, 0, n)),
        out_shape=jax.ShapeDtypeStruct((DEPTH, r, n6), F32),
        compiler_params=_cparams(2), name="ada_mod",
    )(c_all, w_ada, b_ada.reshape(DEPTH, 1, n6))


def _modulate_kernel(x_ref, m_ref, o_ref):
    o_ref[...] = (x_ref[...] * (1.0 + m_ref[1:2, :]) + m_ref[0:1, :]).astype(BF)


def _modulate(x, mod, l):
    b, t, d = x.shape
    tm = min(t, 512)
    return pl.pallas_call(
        _modulate_kernel,
        grid=(b, t // tm),
        in_specs=[pl.BlockSpec((None, tm, d), lambda i, j: (i, j, 0)),
                  pl.BlockSpec((None, None, 6, d), lambda i, j: (l, i, 0, 0))],
        out_specs=pl.BlockSpec((None, tm, d), lambda i, j: (i, j, 0)),
        out_shape=jax.ShapeDtypeStruct((b, t, d), BF),
        compiler_params=_cparams(2), name="modulate",
    )(x, mod)


def _mm_kernel(x_ref, w_ref, b_ref, o_ref, wbf_ref):
    @pl.when(pl.program_id(1) == 0)
    def _():
        _cast_rows(w_ref, wbf_ref, w_ref.shape[0])
    o_ref[...] = (_dot(x_ref[...], wbf_ref[...]) + b_ref[...]).astype(o_ref.dtype)


def _mm(x, w, bias, l, out_dtype=F32, tn=512):
    m, k = x.shape
    n = w.shape[-1]
    tm = min(m, 512)
    return pl.pallas_call(
        _mm_kernel,
        grid=(n // tn, m // tm),
        in_specs=[pl.BlockSpec((tm, k), lambda j, i: (i, 0)),
                  pl.BlockSpec((None, k, tn), lambda j, i: (l, 0, j)),
                  pl.BlockSpec((None, 1, tn), lambda j, i: (l, 0, j))],
        out_specs=pl.BlockSpec((tm, tn), lambda j, i: (i, j)),
        out_shape=jax.ShapeDtypeStruct((m, n), out_dtype),
        scratch_shapes=[pltpu.VMEM((k, tn), BF)],
        compiler_params=_cparams(2), name="mm",
    )(x, w, bias)


def _gate_merge_kernel(h_ref, o_ref, wg_ref, bg_ref, wb_ref, out_ref, acc_ref, wgbf_ref, wbbf_ref):
    br = pl.program_id(2)

    @pl.when(br == 0)
    def _():
        acc_ref[...] = jnp.zeros_like(acc_ref)

    _cast_rows(wg_ref, wgbf_ref, wg_ref.shape[0])
    _cast_rows(wb_ref, wbbf_ref, wb_ref.shape[0])
    g = _dot(h_ref[...], wgbf_ref[...]) + bg_ref[...]
    u = _dot(o_ref[...], wbbf_ref[...])
    acc_ref[...] += _sigmoid(g) * u

    @pl.when(br == 3)
    def _():
        out_ref[...] = acc_ref[...].astype(out_ref.dtype)


def _gate_merge(h, o_all, w_gate, b_gate, w_branch, l):
    m = h.shape[0]
    tm = min(m, 512)
    tn = 512
    return pl.pallas_call(
        _gate_merge_kernel,
        grid=(D_MODEL // tn, m // tm, 4),
        in_specs=[pl.BlockSpec((tm, D_MODEL), lambda j, i, b: (i, 0)),
                  pl.BlockSpec((None, tm, MIX_W), lambda j, i, b: (b, i, 0)),
                  pl.BlockSpec((None, None, D_MODEL, tn), lambda j, i, b: (l, b, 0, j)),
                  pl.BlockSpec((None, None, 1, tn), lambda j, i, b: (l, b, 0, j)),
                  pl.BlockSpec((None, None, MIX_W, tn), lambda j, i, b: (l, b, 0, j))],
        out_specs=pl.BlockSpec((tm, tn), lambda j, i, b: (i, j)),
        out_shape=jax.ShapeDtypeStruct((m, D_MODEL), BF),
        scratch_shapes=[pltpu.VMEM((tm, tn), F32), pltpu.VMEM((D_MODEL, tn), BF), pltpu.VMEM((MIX_W, tn), BF)],
        compiler_params=_cparams(3), name="gate_merge",
    )(h, o_all, w_gate, b_gate.reshape(DEPTH, 4, 1, D_MODEL), w_branch)


def _ln_kernel(*refs, n_slots, gate_row, shift_row, emit_h):
    it = iter(refs)
    x_ref, y_ref = next(it), next(it)
    wt_ref = next(it) if n_slots > 1 else None
    ma_ref, mb_ref, g_ref, b_ref = next(it), next(it), next(it), next(it)
    xo_ref = next(it)
    ho_ref = next(it) if emit_h else None
    if n_slots > 1:
        y = y_ref[:, 0, :] * wt_ref[:, 0:1]
        for s in range(1, n_slots):
            y = y + y_ref[:, s, :] * wt_ref[:, s:s + 1]
    else:
        y = y_ref[...]
    z = DN_ALPHA * x_ref[...] + (1.0 + ma_ref[gate_row:gate_row + 1, :]) * y
    mu = jnp.mean(z, axis=-1, keepdims=True)
    zc = z - mu
    var = jnp.mean(zc * zc, axis=-1, keepdims=True)
    xn = zc * lax.rsqrt(var + LN_EPS) * g_ref[...] + b_ref[...]
    xo_ref[...] = xn
    if emit_h:
        ho_ref[...] = (xn * (1.0 + mb_ref[shift_row + 1:shift_row + 2, :]) + mb_ref[shift_row:shift_row + 1, :]).astype(BF)


def _ln_mod(x, y, wts, mod, ln_g, ln_b, l, which, l_next, shift_row):
    b, t, d = x.shape
    n_slots = 1 if wts is None else y.shape[2]
    tm = min(t, 256 if n_slots == 1 else 128)
    emit_h = l_next is not None
    gate_row = 2 if which == 0 else 5
    ln_i = which
    in_specs = [pl.BlockSpec((None, tm, d), lambda i, j: (i, j, 0))]
    args = [x, y]
    if n_slots > 1:
        in_specs += [pl.BlockSpec((None, tm, n_slots, d), lambda i, j: (i, j, 0, 0)),
                     pl.BlockSpec((None, tm, n_slots), lambda i, j: (i, j, 0))]
        args.append(wts)
    else:
        in_specs.append(pl.BlockSpec((None, tm, d), lambda i, j: (i, j, 0)))
    lb = l if l_next is None else l_next
    in_specs += [pl.BlockSpec((None, None, 6, d), lambda i, j: (l, i, 0, 0)),
                 pl.BlockSpec((None, None, 6, d), lambda i, j: (lb, i, 0, 0)),
                 pl.BlockSpec((None, None, 1, d), lambda i, j: (l, ln_i, 0, 0)),
                 pl.BlockSpec((None, None, 1, d), lambda i, j: (l, ln_i, 0, 0))]
    args += [mod, mod, ln_g.reshape(DEPTH, 2, 1, d), ln_b.reshape(DEPTH, 2, 1, d)]
    out_specs = [pl.BlockSpec((None, tm, d), lambda i, j: (i, j, 0))]
    out_shape = [jax.ShapeDtypeStruct((b, t, d), F32)]
    if emit_h:
        out_specs.append(pl.BlockSpec((None, tm, d), lambda i, j: (i, j, 0)))
        out_shape.append(jax.ShapeDtypeStruct((b, t, d), BF))
    res = pl.pallas_call(
        functools.partial(_ln_kernel, n_slots=n_slots, gate_row=gate_row, shift_row=shift_row, emit_h=emit_h),
        grid=(b, t // tm), in_specs=in_specs, out_specs=out_specs, out_shape=out_shape,
        compiler_params=_cparams(2), name="ln_mod",
    )(*args)
    return (res[0], res[1]) if emit_h else (res[0], None)


def _topk_lanes(vals, lane, k, floor):
    picks = []
    lane_f = lane.astype(F32)
    for _ in range(k):
        mx = jnp.max(vals, axis=-1, keepdims=True)
        idx = jnp.min(jnp.where(vals == mx, lane_f, 4096.0), axis=-1, keepdims=True)
        picks.append((mx, idx))
        vals = jnp.where(lane_f == idx, floor, vals)
    return picks


def _router_kernel(x_ref, w_ref, b_ref, gate_ref, idx_ref):
    logits = _dot_hi(x_ref[...].astype(F32), w_ref[...]) + b_ref[...]
    lane = lax.broadcasted_iota(jnp.int32, logits.shape, 1)
    vals = jnp.where(lane < N_EXPERTS, logits, NEG)
    picks = _topk_lanes(vals, lane, TOP_K, -3e38)
    v0 = picks[0][0]
    es = [jnp.exp(v - v0) for v, _ in picks]
    z = es[0] + es[1] + es[2] + es[3]
    gates = jnp.zeros(logits.shape, F32)
    idxs = jnp.zeros(logits.shape, F32)
    for k in range(TOP_K):
        gates = jnp.where(lane == k, es[k] / z, gates)
        idxs = jnp.where(lane == k, picks[k][1], idxs)
    gate_ref[...] = gates
    idx_ref[...] = idxs.astype(jnp.int32)


def _router(h2, w_router, b_router, l):
    m = h2.shape[0]
    tm = min(m, 512)
    w = jnp.pad(w_router, ((0, 0), (0, 0), (0, LANE - N_EXPERTS)))
    bb = jnp.pad(b_router, ((0, 0), (0, LANE - N_EXPERTS))).reshape(DEPTH, 1, LANE)
    return pl.pallas_call(
        _router_kernel,
        grid=(m // tm,),
        in_specs=[pl.BlockSpec((tm, D_MODEL), lambda i: (i, 0)),
                  pl.BlockSpec((None, D_MODEL, LANE), lambda i: (l, 0, 0)),
                  pl.BlockSpec((None, 1, LANE), lambda i: (l, 0, 0))],
        out_specs=[pl.BlockSpec((tm, LANE), lambda i: (i, 0)), pl.BlockSpec((tm, LANE), lambda i: (i, 0))],
        out_shape=[jax.ShapeDtypeStruct((m, LANE), F32), jax.ShapeDtypeStruct((m, LANE), jnp.int32)],
        compiler_params=_cparams(1), name="router",
    )(h2, w, bb)


def _moe_up_kernel(te_ref, tv_ref, x_ref, wg_ref, wu_ref, bg_ref, bu_ref, o_ref):
    t = pl.program_id(0)

    @pl.when(tv_ref[t] > 0)
    def _():
        x = x_ref[...]
        g = _dot(x, wg_ref[...].astype(BF)) + bg_ref[...]
        u = _dot(x, wu_ref[...].astype(BF)) + bu_ref[...]
        g = jnp.minimum(g, SWIGLU_LIMIT)
        u = jnp.clip(u, -SWIGLU_LIMIT, SWIGLU_LIMIT)
        o_ref[...] = ((u + 1.0) * g * _sigmoid(SWIGLU_ALPHA * g)).astype(o_ref.dtype)

    @pl.when(tv_ref[t] == 0)
    def _():
        o_ref[...] = jnp.zeros_like(o_ref)


def _moe_dn_kernel(te_ref, tv_ref, a_ref, w_ref, b_ref, o_ref):
    t = pl.program_id(0)

    @pl.when(tv_ref[t] > 0)
    def _():
        o_ref[...] = _dot(a_ref[...], w_ref[...].astype(BF)) + b_ref[...]

    @pl.when(tv_ref[t] == 0)
    def _():
        o_ref[...] = jnp.zeros_like(o_ref)


def _moe_experts(x_rows, tile_e, tile_v, w_gu, b_gu, w_dn, b_dn, l, tm):
    r = x_rows.shape[0]
    nt = r // tm
    tn = 256
    nj = D_EXPERT // tn
    b_gu4 = b_gu.reshape(DEPTH, N_EXPERTS, 1, 2 * D_EXPERT)
    act = pl.pallas_call(
        _moe_up_kernel,
        grid_spec=pltpu.PrefetchScalarGridSpec(
            num_scalar_prefetch=2, grid=(nt, nj),
            in_specs=[pl.BlockSpec((tm, D_MODEL), lambda t, j, te, tv: (t, 0)),
                      pl.BlockSpec((None, None, D_MODEL, tn), lambda t, j, te, tv: (l, te[t], 0, j * tv[t])),
                      pl.BlockSpec((None, None, D_MODEL, tn), lambda t, j, te, tv: (l, te[t], 0, nj + j * tv[t])),
                      pl.BlockSpec((None, None, 1, tn), lambda t, j, te, tv: (l, te[t], 0, j * tv[t])),
                      pl.BlockSpec((None, None, 1, tn), lambda t, j, te, tv: (l, te[t], 0, nj + j * tv[t]))],
            out_specs=pl.BlockSpec((tm, tn), lambda t, j, te, tv: (t, j))),
        out_shape=jax.ShapeDtypeStruct((r, D_EXPERT), BF),
        compiler_params=_cparams(2), name="moe_up",
    )(tile_e, tile_v, x_rows, w_gu, w_gu, b_gu4, b_gu4)
    tn2 = 1024
    return pl.pallas_call(
        _moe_dn_kernel,
        grid_spec=pltpu.PrefetchScalarGridSpec(
            num_scalar_prefetch=2, grid=(nt, D_MODEL // tn2),
            in_specs=[pl.BlockSpec((tm, D_EXPERT), lambda t, j, te, tv: (t, 0)),
                      pl.BlockSpec((None, None, D_EXPERT, tn2), lambda t, j, te, tv: (l, te[t], 0, j * tv[t])),
                      pl.BlockSpec((None, None, 1, tn2), lambda t, j, te, tv: (l, te[t], 0, j * tv[t]))],
            out_specs=pl.BlockSpec((tm, tn2), lambda t, j, te, tv: (t, j))),
        out_shape=jax.ShapeDtypeStruct((r, D_MODEL), F32),
        compiler_params=_cparams(2), name="moe_dn",
    )(tile_e, tile_v, act, w_dn, b_dn.reshape(DEPTH, N_EXPERTS, 1, D_MODEL))


def _moe(h2, w_router, b_router, w_gu, b_gu, w_dn, b_dn, l):
    b, t, d = h2.shape
    n_tok = b * t
    xs = h2.reshape(n_tok, d)
    gate_l, idx_l = _router(xs, w_router, b_router, l)
    gates = gate_l[:, :TOP_K]
    e_flat = idx_l[:, :TOP_K].reshape(-1)
    n_asg = n_tok * TOP_K
    tm = 512 if n_asg >= 512 * N_EXPERTS else 16
    order = jnp.argsort(e_flat, stable=True)
    counts = jnp.sum(jax.nn.one_hot(e_flat, N_EXPERTS, dtype=jnp.int32), axis=0)
    padded = (counts + tm - 1) // tm * tm
    pad_end = jnp.cumsum(padded)
    pad_start = pad_end - padded
    start = jnp.cumsum(counts) - counts
    e_sorted = e_flat[order]
    dest_sorted = pad_start[e_sorted] + jnp.arange(n_asg, dtype=jnp.int32) - start[e_sorted]
    n_rows = (n_asg // tm + N_EXPERTS) * tm
    nt = n_rows // tm
    row_tok = jnp.full((n_rows,), n_tok, jnp.int32).at[dest_sorted].set((order // TOP_K).astype(jnp.int32))
    pos = jnp.zeros((n_asg,), jnp.int32).at[order].set(dest_sorted.astype(jnp.int32))
    tile_start = jnp.arange(nt, dtype=jnp.int32) * tm
    tile_e = jnp.minimum(jnp.searchsorted(pad_end, tile_start, side="right"), N_EXPERTS - 1).astype(jnp.int32)
    tile_v = (tile_start < pad_end[-1]).astype(jnp.int32)
    x_rows = jnp.concatenate([xs, jnp.zeros((1, d), xs.dtype)], 0)[row_tok]
    y_rows = _moe_experts(x_rows, tile_e, tile_v, w_gu, b_gu, w_dn, b_dn, l, tm)
    y_slots = y_rows[pos].reshape(b, t, TOP_K, d)
    return y_slots, gates.reshape(b, t, TOP_K)


def _t5_bucket(dist):
    n = jnp.maximum(dist, 0)
    exact = N_BUCKETS // 2
    nf = jnp.maximum(n, 1).astype(F32)
    large = exact + (jnp.log(nf / exact) / math.log(MAX_DISTANCE / exact) * (N_BUCKETS - exact)).astype(jnp.int32)
    return jnp.where(n < exact, n, jnp.minimum(large, N_BUCKETS - 1))


def _t5_tiles(table, n_off):
    o = jnp.arange(n_off)[:, None, None]
    r = jnp.arange(TB)[None, :, None]
    c = jnp.arange(TB)[None, None, :]
    return jnp.moveaxis(table[_t5_bucket(o * TB + r - c)], -1, 0)


def _lam_of(dl_ref, lam_init):
    a = jnp.sum(dl_ref[0:1, :] * dl_ref[1:2, :], axis=-1, keepdims=True)
    b = jnp.sum(dl_ref[2:3, :] * dl_ref[3:4, :], axis=-1, keepdims=True)
    return jnp.exp(a) - jnp.exp(b) + lam_init


def _online_update(s, mask, v, m_ref, l_ref, acc_ref):
    s = jnp.where(mask, s, NEG)
    m_prev = m_ref[...]
    m_new = jnp.maximum(m_prev, jnp.max(s, axis=-1, keepdims=True))
    alpha = jnp.exp(m_prev - m_new)
    p = jnp.where(mask, jnp.exp(s - m_new), 0.0)
    l_ref[...] = alpha * l_ref[...] + jnp.sum(p, axis=-1, keepdims=True)
    acc_ref[...] = alpha * acc_ref[...] + _dot(p.astype(BF), v)
    m_ref[...] = m_new


def _normalized(l_ref, acc_ref):
    l = l_ref[...]
    return acc_ref[...] / jnp.where(l > 0, l, 1.0)


def _flash_kernel(*refs, nq, band, fox, sel, diff, gated, addend, lam_init):
    it = iter(refs)
    q_ref, k_ref, v_ref = next(it), next(it), next(it)
    if fox:
        fq_ref, fk_ref = next(it), next(it)
    else:
        tab_ref = next(it)
    if sel:
        sel_ref, e_ref = next(it), next(it)
    if gated:
        g_ref = next(it)
    if addend:
        add_ref = next(it)
    if diff:
        dl_ref, gn_ref = next(it), next(it)
    o_ref = next(it)
    n_maps = 2 if diff else 1
    stats = [(next(it), next(it), next(it)) for _ in range(n_maps)]

    i = pl.program_id(2)
    jj = pl.program_id(3)
    if band:
        j = i - 1 + jj
        active = j >= 0
        last = jj == 1
    else:
        j = jj
        active = jj <= i
        last = jj == nq - 1

    @pl.when(jj == 0)
    def _():
        for m_ref, l_ref, acc_ref in stats:
            m_ref[...] = jnp.full_like(m_ref, NEG)
            l_ref[...] = jnp.zeros_like(l_ref)
            acc_ref[...] = jnp.zeros_like(acc_ref)

    @pl.when(active)
    def _():
        q = q_ref[...]
        k = k_ref[...].astype(BF)
        v = v_ref[...].astype(BF)
        row = lax.broadcasted_iota(jnp.int32, (TB, TB), 0)
        col = lax.broadcasted_iota(jnp.int32, (TB, TB), 1)
        dpos = (i - j) * TB + row - col
        mask = dpos >= 0
        if band:
            mask = mask & (dpos <= NSA_WINDOW)
        if sel:
            mask = mask & (_dot(sel_ref[...].astype(BF), e_ref[...]) > 0.5)
        bias = (fq_ref[...] - fk_ref[...]) if fox else tab_ref[...]
        if diff:
            lane = lax.broadcasted_iota(jnp.int32, q.shape, 1)
            for mi, (m_ref, l_ref, acc_ref) in enumerate(stats):
                half = (lane < DIFF_DH) if mi == 0 else (lane >= DIFF_DH)
                qs = jnp.where(half, q * DIFF_SCALE, 0.0).astype(BF)
                _online_update(_dot_nt(qs, k) + bias, mask, v, m_ref, l_ref, acc_ref)
        else:
            m_ref, l_ref, acc_ref = stats[0]
            _online_update(_dot_nt((q * SCALE).astype(BF), k) + bias, mask, v, m_ref, l_ref, acc_ref)

    @pl.when(last)
    def _():
        o = _normalized(stats[0][1], stats[0][2])
        if diff:
            o = o - _lam_of(dl_ref, lam_init) * _normalized(stats[1][1], stats[1][2])
            o = o * lax.rsqrt(jnp.mean(o * o, axis=-1, keepdims=True) + LN_EPS) * gn_ref[...] * (1.0 - lam_init)
        if gated:
            o = _sigmoid(g_ref[...]) * o
        if addend:
            o = o + add_ref[...]
        o_ref[...] = o.astype(o_ref.dtype)


def _flash_prefill(proj, qcol, kvcol, *, fq=None, fk=None, tab=None, sel=None, sel_e=None, sel_per_kv=False,
                   band=False, gates=None, gate_i=0, addend=None, diff=None, out_dtype=BF):
    b, t, _ = proj.shape
    nq = t // TB
    nkk = 2 if band else nq
    if band:
        jmap = lambda i, jj: jnp.maximum(i - 1 + jj, 0)
    else:
        jmap = lambda i, jj: jnp.minimum(jj, i)
    in_specs = [pl.BlockSpec((None, TB, LANE), lambda bi, h, i, jj: (bi, i, qcol + h)),
                pl.BlockSpec((None, TB, LANE), lambda bi, h, i, jj: (bi, jmap(i, jj), kvcol + h // GROUP)),
                pl.BlockSpec((None, TB, LANE), lambda bi, h, i, jj: (bi, jmap(i, jj), kvcol + N_KV + h // GROUP))]
    args = [proj, proj, proj]
    if fq is not None:
        in_specs += [pl.BlockSpec((None, None, TB, 1), lambda bi, h, i, jj: (bi, h, i, 0)),
                     pl.BlockSpec((None, None, 1, TB), lambda bi, h, i, jj: (bi, h, 0, jmap(i, jj)))]
        args += [fq, fk]
    else:
        in_specs.append(pl.BlockSpec((None, None, TB, TB), lambda bi, h, i, jj: (h, i - jmap(i, jj), 0, 0)))
        args.append(tab)
    if sel is not None:
        hs = (lambda h: h // GROUP) if sel_per_kv else (lambda h: h)
        in_specs += [pl.BlockSpec((None, None, TB, LANE), lambda bi, h, i, jj: (bi, hs(h), i, 0)),
                     pl.BlockSpec((None, LANE, TB), lambda bi, h, i, jj: (jmap(i, jj), 0, 0))]
        args += [sel, sel_e]
    if gates is not None:
        in_specs.append(pl.BlockSpec((None, None, None, TB, 1), lambda bi, h, i, jj: (bi, gate_i, h, i, 0)))
        args.append(gates)
    if addend is not None:
        in_specs.append(pl.BlockSpec((None, TB, LANE), lambda bi, h, i, jj: (bi, i, h)))
        args.append(addend)
    lam_init = 0.0
    if diff is not None:
        dl, gn, lam_init = diff
        in_specs += [pl.BlockSpec((4, DIFF_DH), lambda bi, h, i, jj: (0, 0)),
                     pl.BlockSpec((1, HEAD_DIM), lambda bi, h, i, jj: (0, 0))]
        args += [dl, gn]
    n_maps = 2 if diff is not None else 1
    scratch = []
    for _ in range(n_maps):
        scratch += [pltpu.VMEM((TB, 1), F32), pltpu.VMEM((TB, 1), F32), pltpu.VMEM((TB, HEAD_DIM), F32)]
    kern = functools.partial(_flash_kernel, nq=nq, band=band, fox=fq is not None, sel=sel is not None,
                             diff=diff is not None, gated=gates is not None, addend=addend is not None,
                             lam_init=lam_init)
    return pl.pallas_call(
        kern, grid=(b, N_HEADS, nq, nkk), in_specs=in_specs,
        out_specs=pl.BlockSpec((None, TB, LANE), lambda bi, h, i, jj: (bi, i, h)),
        out_shape=jax.ShapeDtypeStruct((b, t, MIX_W), out_dtype),
        scratch_shapes=scratch, compiler_params=_cparams(4), name="flash_prefill",
    )(*args)


def _logsig_kernel(x_ref, o_ref):
    x = x_ref[...]
    o_ref[...] = jnp.minimum(x, 0.0) - jnp.log(1.0 + jnp.exp(-jnp.abs(x)))


def _log_sigmoid(x):
    b, h, t = x.shape
    return pl.pallas_call(
        _logsig_kernel, grid=(b,),
        in_specs=[pl.BlockSpec((None, h, t), lambda i: (i, 0, 0))],
        out_specs=pl.BlockSpec((None, h, t), lambda i: (i, 0, 0)),
        out_shape=jax.ShapeDtypeStruct((b, h, t), F32), compiler_params=_cparams(1), name="log_sigmoid",
    )(x)


CS_CHUNK = 512


def _cumsum_kernel(x_ref, o_ref, carry_ref):
    @pl.when(pl.program_id(1) == 0)
    def _():
        carry_ref[...] = jnp.zeros_like(carry_ref)
    r = lax.broadcasted_iota(jnp.int32, (CS_CHUNK, CS_CHUNK), 0)
    c = lax.broadcasted_iota(jnp.int32, (CS_CHUNK, CS_CHUNK), 1)
    tri = (r <= c).astype(F32)
    y = _dot_hi(x_ref[...], tri) + carry_ref[...]
    o_ref[...] = y
    carry_ref[...] = y[:, CS_CHUNK - 1:CS_CHUNK]


def _cumsum(x):
    b, h, n = x.shape
    return pl.pallas_call(
        _cumsum_kernel, grid=(b, n // CS_CHUNK),
        in_specs=[pl.BlockSpec((None, h, CS_CHUNK), lambda i, j: (i, 0, j))],
        out_specs=pl.BlockSpec((None, h, CS_CHUNK), lambda i, j: (i, 0, j)),
        out_shape=jax.ShapeDtypeStruct((b, h, n), F32),
        scratch_shapes=[pltpu.VMEM((h, 1), F32)], compiler_params=_cparams(2), name="cumsum",
    )(x)


def _moba_gate_kernel(q_ref, k_ref, sel_ref, *, t):
    nblk = t // MOBA_BLOCK
    r = lax.broadcasted_iota(jnp.int32, (LANE, t), 0)
    c = lax.broadcasted_iota(jnp.int32, (LANE, t), 1)
    avg = jnp.where(c // MOBA_BLOCK == r, 1.0 / MOBA_BLOCK, 0.0)
    kmean = _dot_hi(avg, k_ref[...])
    gs = lax.dot_general(q_ref[...], kmean, (((1,), (1,)), ((), ())), precision=lax.Precision.HIGHEST,
                         preferred_element_type=F32)
    lane = lax.broadcasted_iota(jnp.int32, gs.shape, 1)
    own = lax.broadcasted_iota(jnp.int32, gs.shape, 0) // MOBA_BLOCK
    vals = jnp.where(lane < own, gs, NEG)
    sel = (lane == own).astype(F32)
    for mx, idx in _topk_lanes(vals, lane, min(MOBA_TOPK, nblk), -3e38):
        sel = jnp.where((lane.astype(F32) == idx) & (mx > NEG_TEST), 1.0, sel)
    sel_ref[...] = sel


def _moba_gate(proj):
    b, t, _ = proj.shape
    return pl.pallas_call(
        functools.partial(_moba_gate_kernel, t=t), grid=(b, N_HEADS),
        in_specs=[pl.BlockSpec((None, t, LANE), lambda bi, h: (bi, 0, QM + h)),
                  pl.BlockSpec((None, t, LANE), lambda bi, h: (bi, 0, KVM + h // GROUP))],
        out_specs=pl.BlockSpec((None, None, t, LANE), lambda bi, h: (bi, h, 0, 0)),
        out_shape=jax.ShapeDtypeStruct((b, N_HEADS, t, LANE), F32),
        compiler_params=_cparams(2), name="moba_gate",
    )(proj, proj)


def _compress_kernel(c_ref, w_ref, o_ref):
    o_ref[...] = _dot(c_ref[...].astype(BF), w_ref[...].astype(BF))


def _nsa_compress(chunks, w_cmp, l):
    b, _, _, n, sd = chunks.shape
    rt = min(n, 512)
    r_len = NSA_CMP_LEN // NSA_CMP_STRIDE
    w = w_cmp.reshape(DEPTH, 2, r_len, sd, HEAD_DIM)
    y = pl.pallas_call(
        _compress_kernel, grid=(b, 2, N_KV, r_len, n // rt),
        in_specs=[pl.BlockSpec((None, None, None, rt, sd), lambda bi, j, k, r, i: (bi, j, k, i, 0)),
                  pl.BlockSpec((None, None, None, sd, HEAD_DIM), lambda bi, j, k, r, i: (l, j, r, 0, 0))],
        out_specs=pl.BlockSpec((None, None, None, None, rt, HEAD_DIM), lambda bi, j, k, r, i: (bi, j, k, r, i, 0)),
        out_shape=jax.ShapeDtypeStruct((b, 2, N_KV, r_len, n, HEAD_DIM), F32),
        compiler_params=_cparams(5), name="nsa_compress",
    )(chunks, w)
    return y[:, :, :, 0] + jnp.pad(y[:, :, :, 1, 1:], ((0, 0), (0, 0), (0, 0), (0, 1), (0, 0)))


def _nsa_cmp_kernel(q_ref, kc_ref, vc_ref, tab_ref, cover_ref, g_ref, o_ref, sel_ref, *, n_cmp, n_slc):
    i = pl.program_id(2)
    shape = (TB, LANE)
    qpos = i * TB + lax.broadcasted_iota(jnp.int32, shape, 0)
    lane = lax.broadcasted_iota(jnp.int32, shape, 1)
    maskc = (lane * NSA_CMP_STRIDE + (NSA_CMP_LEN - 1) <= qpos) & (lane < n_cmp)
    kc = kc_ref[...].astype(BF)
    vc = vc_ref[...].astype(BF)
    imp = jnp.zeros(shape, F32)
    for g in range(GROUP):
        q = (q_ref[:, g * HEAD_DIM:(g + 1) * HEAD_DIM] * SCALE).astype(BF)
        s = jnp.where(maskc, _dot_nt(q, kc) + tab_ref[g], NEG)
        mx = jnp.max(s, axis=-1, keepdims=True)
        p = jnp.where(maskc, jnp.exp(s - mx), 0.0)
        z = jnp.sum(p, axis=-1, keepdims=True)
        pc = p / jnp.where(z > 0, z, 1.0)
        o_ref[:, g * HEAD_DIM:(g + 1) * HEAD_DIM] = _sigmoid(g_ref[g]) * _dot(pc.astype(BF), vc)
        imp = imp + _dot_hi(pc, cover_ref[...])
    cur = qpos // NSA_SLC_BLOCK
    forced = (lane == 0) | ((lane <= cur) & (lane > cur - NSA_N_LOCAL))
    vals = jnp.where(forced, NSA_FORCE, jnp.where(lane <= cur, imp, -1.0))
    vals = jnp.where(lane < n_slc, vals, NEG)
    sel = jnp.zeros(shape, F32)
    for mx, idx in _topk_lanes(vals, lane, min(NSA_TOPN, n_slc), -3e38):
        sel = jnp.where((lane.astype(F32) == idx) & (mx >= 0.0), 1.0, sel)
    sel_ref[...] = sel


def _nsa_cmp_prefill(proj, kvc, tabc, cover, gates, n_cmp, n_slc):
    b, t, _ = proj.shape
    nq = t // TB
    return pl.pallas_call(
        functools.partial(_nsa_cmp_kernel, n_cmp=n_cmp, n_slc=n_slc), grid=(b, N_KV, nq),
        in_specs=[pl.BlockSpec((None, TB, GROUP * LANE), lambda bi, k, i: (bi, i, QN // GROUP + k)),
                  pl.BlockSpec((None, None, None, LANE, HEAD_DIM), lambda bi, k, i: (bi, 0, k, 0, 0)),
                  pl.BlockSpec((None, None, None, LANE, HEAD_DIM), lambda bi, k, i: (bi, 1, k, 0, 0)),
                  pl.BlockSpec((GROUP, TB, LANE), lambda bi, k, i: (k, i, 0)),
                  pl.BlockSpec((LANE, LANE), lambda bi, k, i: (0, 0)),
                  pl.BlockSpec((None, None, GROUP, TB, 1), lambda bi, k, i: (bi, 0, k, i, 0))],
        out_specs=[pl.BlockSpec((None, TB, GROUP * LANE), lambda bi, k, i: (bi, i, k)),
                   pl.BlockSpec((None, None, TB, LANE), lambda bi, k, i: (bi, k, i, 0))],
        out_shape=[jax.ShapeDtypeStruct((b, t, MIX_W), F32), jax.ShapeDtypeStruct((b, N_KV, t, LANE), F32)],
        compiler_params=_cparams(3), name="nsa_cmp_prefill",
    )(proj, kvc, kvc, tabc, cover, gates)


def _cover_matrix(n_cmp, n_slc):
    c_start = np.arange(n_cmp) * NSA_CMP_STRIDE
    s_ids = np.arange(n_slc)
    return ((c_start[:, None] < (s_ids[None, :] + 1) * NSA_SLC_BLOCK)
            & (c_start[:, None] + NSA_CMP_LEN > s_ids[None, :] * NSA_SLC_BLOCK)).astype(np.float32)


def _sel_expand(n_tiles, block):
    j = np.arange(n_tiles)[:, None, None]
    m = np.arange(LANE)[None, :, None]
    c = np.arange(TB)[None, None, :]
    return jnp.asarray(m == (j * TB + c) // block, BF)


def _mixers_prompt(l, h, P, tabs):
    b, t, d = h.shape
    n = b * t
    proj = _mm(h.reshape(n, d), P["w_in_r"], P["b_in_r"], l).reshape(b, t, PROJ_W)
    misc = proj[:, :, MISC * LANE:MISC * LANE + N_HEADS + 3 * N_HEADS]
    logf = _log_sigmoid(jnp.swapaxes(misc[..., :N_HEADS], 1, 2))
    gates = jnp.transpose(misc[..., N_HEADS:].reshape(b, t, 3, N_HEADS), (0, 2, 3, 1))[..., None]
    kvs = lambda c: proj[:, :, c * LANE:(c + 4) * LANE].reshape(b, t, 2, N_KV, HEAD_DIM)
    rows = dict(fox_kv=kvs(KVF), fox_logf=jnp.swapaxes(logf, 1, 2), diff_kv=kvs(KVD), moba_kv=kvs(KVM),
                nsa_cmp_kv=kvs(KVC), nsa_slc_kv=kvs(KVS))
    win = kvs(KVW)
    rows["nsa_win_kv"] = win[:, -min(NSA_WINDOW, t):]

    f_cum = _cumsum(logf)
    o_fox = _flash_prefill(proj, QF, KVF, fq=f_cum[..., None], fk=f_cum[:, :, None, :])

    lam_init = 0.8 - 0.6 * math.exp(-0.3 * l)
    o_diff = _flash_prefill(proj, QD, KVD, tab=tabs[0],
                            diff=(P["diff_lambda"][l], P["diff_norm_g"][l].reshape(1, HEAD_DIM), lam_init))

    nk = t // TB
    o_moba = _flash_prefill(proj, QM, KVM, tab=tabs[1], sel=_moba_gate(proj), sel_e=_sel_expand(nk, MOBA_BLOCK))

    n_chunk = t // NSA_CMP_STRIDE
    n_cmp = n_chunk - NSA_CMP_LEN // NSA_CMP_STRIDE + 1
    n_slc = -(-t // NSA_SLC_BLOCK)
    chunks = jnp.transpose(rows["nsa_cmp_kv"], (0, 2, 3, 1, 4)).reshape(b, 2, N_KV, n_chunk, NSA_CMP_STRIDE * HEAD_DIM)
    kvc = _nsa_compress(chunks, P["w_cmp"], l)
    cmp_end = jnp.arange(LANE) * NSA_CMP_STRIDE + (NSA_CMP_LEN - 1)
    tabc = jnp.moveaxis(P["rel_bias"][:, 2][_t5_bucket(jnp.arange(t)[:, None] - cmp_end[None, :])], -1, 0)
    cover = jnp.asarray(np.pad(_cover_matrix(n_cmp, n_slc), ((0, LANE - n_cmp), (0, LANE - n_slc))))
    o1, sel_n = _nsa_cmp_prefill(proj, kvc, tabc, cover, gates, n_cmp, n_slc)
    o2 = _flash_prefill(proj, QN, KVS, tab=tabs[2], sel=sel_n, sel_e=_sel_expand(nk, NSA_SLC_BLOCK), sel_per_kv=True,
                        gates=gates, gate_i=1, addend=o1, out_dtype=F32)
    o_nsa = _flash_prefill(proj, QN, KVW, tab=tabs[2], band=True, gates=gates, gate_i=2, addend=o2)
    o_all = jnp.stack([o_fox, o_diff, o_moba, o_nsa]).reshape(4, n, MIX_W)
    return o_all, rows


def _decode_kernel(*refs, pp, ns, diff, lam_init):
    it = iter(refs)
    _pt_ref = next(it)
    q_ref = next(it)
    page_refs = [next(it) for _ in range(pp)]
    new_ref, bias_ref = next(it), next(it)
    if diff:
        dl_ref, gn_ref = next(it), next(it)
    o_ref = next(it)
    m_ref, l_ref, acc_ref = next(it), next(it), next(it)
    n_maps = 2 if diff else 1
    s_id = pl.program_id(1)

    @pl.when(s_id == 0)
    def _():
        m_ref[...] = jnp.full_like(m_ref, NEG)
        l_ref[...] = jnp.zeros_like(l_ref)
        acc_ref[...] = jnp.zeros_like(acc_ref)

    def process(ref, i):
        for kv in range(N_KV):
            k = ref[:, 0, kv, :].astype(BF)
            v = ref[:, 1, kv, :].astype(BF)
            bias = bias_ref[kv, :, i * PAGE_SIZE:(i + 1) * PAGE_SIZE]
            mask = bias > NEG_TEST
            q = q_ref[kv]
            for mi in range(n_maps):
                if diff:
                    lane = lax.broadcasted_iota(jnp.int32, q.shape, 1)
                    half = (lane < DIFF_DH) if mi == 0 else (lane >= DIFF_DH)
                    qs = jnp.where(half, q * DIFF_SCALE, 0.0).astype(BF)
                else:
                    qs = (q * SCALE).astype(BF)
                _online_update(_dot_nt(qs, k) + bias, mask, v, m_ref.at[mi, kv], l_ref.at[mi, kv], acc_ref.at[mi, kv])

    @pl.when(s_id < ns)
    def _():
        for i, ref in enumerate(page_refs):
            process(ref, i)

    @pl.when(s_id == ns)
    def _():
        process(new_ref, 0)
        for kv in range(N_KV):
            o = _normalized(l_ref.at[0, kv], acc_ref.at[0, kv])
            if diff:
                o = o - _lam_of(dl_ref, lam_init) * _normalized(l_ref.at[1, kv], acc_ref.at[1, kv])
                o = o * lax.rsqrt(jnp.mean(o * o, axis=-1, keepdims=True) + LN_EPS) * gn_ref[...] * (1.0 - lam_init)
            o_ref[kv] = o


def _decode_attn(q, pool, l, page_table, new_kv, bias, pp, diff=None):
    b, _, r, _ = q.shape
    n_pages = page_table.shape[1]
    ns = n_pages // pp
    bb = bias.shape[0]
    blk = (None, None, PAGE_SIZE, 2, N_KV, HEAD_DIM)
    page_specs = [pl.BlockSpec(blk, (lambda bi, s, pt, i=i: (l, pt[bi, jnp.minimum(s, ns - 1) * pp + i], 0, 0, 0, 0)))
                  for i in range(pp)]
    in_specs = ([pl.BlockSpec((None, N_KV, r, HEAD_DIM), lambda bi, s, pt: (bi, 0, 0, 0))] + page_specs +
                [pl.BlockSpec((None, PAGE_SIZE, 2, N_KV, HEAD_DIM), lambda bi, s, pt: (bi, 0, 0, 0, 0)),
                 pl.BlockSpec((None, N_KV, r, pp * PAGE_SIZE), lambda bi, s, pt: (bi if bb > 1 else 0, 0, 0, s))])
    args = [q] + [pool] * pp + [new_kv, bias]
    lam_init = 0.0
    if diff is not None:
        dl, gn, lam_init = diff
        in_specs += [pl.BlockSpec((4, DIFF_DH), lambda bi, s, pt: (0, 0)),
                     pl.BlockSpec((1, HEAD_DIM), lambda bi, s, pt: (0, 0))]
        args += [dl, gn]
    n_maps = 2 if diff is not None else 1
    return pl.pallas_call(
        functools.partial(_decode_kernel, pp=pp, ns=ns, diff=diff is not None, lam_init=lam_init),
        grid_spec=pltpu.PrefetchScalarGridSpec(
            num_scalar_prefetch=1, grid=(b, ns + 1), in_specs=in_specs,
            out_specs=pl.BlockSpec((None, N_KV, r, HEAD_DIM), lambda bi, s, pt: (bi, 0, 0, 0)),
            scratch_shapes=[pltpu.VMEM((n_maps, N_KV, r, 1), F32), pltpu.VMEM((n_maps, N_KV, r, 1), F32),
                            pltpu.VMEM((n_maps, N_KV, r, HEAD_DIM), F32)]),
        out_shape=jax.ShapeDtypeStruct((b, N_KV, r, HEAD_DIM), F32),
        compiler_params=_cparams(2), name="decode_attn",
    )(page_table, *args)


def _page_sum_kernel(*refs, pp):
    _pt_ref = refs[0]
    page_refs = refs[1:1 + pp]
    o_ref = refs[1 + pp]
    for i, ref in enumerate(page_refs):
        for kv in range(N_KV):
            o_ref[i, kv:kv + 1, :] = jnp.sum(ref[:, 0, kv, :], axis=0, keepdims=True)


def _page_key_sums(pool, l, page_table, pp):
    b, n_pages = page_table.shape
    blk = (None, None, PAGE_SIZE, 2, N_KV, HEAD_DIM)
    page_specs = [pl.BlockSpec(blk, (lambda bi, s, pt, i=i: (l, pt[bi, s * pp + i], 0, 0, 0, 0))) for i in range(pp)]
    return pl.pallas_call(
        functools.partial(_page_sum_kernel, pp=pp),
        grid_spec=pltpu.PrefetchScalarGridSpec(
            num_scalar_prefetch=1, grid=(b, n_pages // pp), in_specs=page_specs,
            out_specs=pl.BlockSpec((None, pp, N_KV, HEAD_DIM), lambda bi, s, pt: (bi, s, 0, 0))),
        out_shape=jax.ShapeDtypeStruct((b, n_pages, N_KV, HEAD_DIM), F32),
        compiler_params=_cparams(2), name="page_key_sums",
    )(page_table, *([pool] * pp))


def _cmp_decode_kernel(q_ref, kvc_ref, bias_ref, o_ref, p_ref):
    for kv in range(N_KV):
        q = (q_ref[kv] * SCALE).astype(BF)
        bias = bias_ref[kv]
        mask = bias > NEG_TEST
        s = jnp.where(mask, _dot_nt(q, kvc_ref[0, kv].astype(BF)) + bias, NEG)
        mx = jnp.max(s, axis=-1, keepdims=True)
        p = jnp.where(mask, jnp.exp(s - mx), 0.0)
        z = jnp.sum(p, axis=-1, keepdims=True)
        pc = p / jnp.where(z > 0, z, 1.0)
        p_ref[kv] = pc
        o_ref[kv] = _dot(pc.astype(BF), kvc_ref[1, kv].astype(BF))


def _cmp_decode(q, kvc, bias):
    b, _, r, _ = q.shape
    n = kvc.shape[3]
    return pl.pallas_call(
        _cmp_decode_kernel, grid=(b,),
        in_specs=[pl.BlockSpec((None, N_KV, r, HEAD_DIM), lambda bi: (bi, 0, 0, 0)),
                  pl.BlockSpec((None, 2, N_KV, n, HEAD_DIM), lambda bi: (bi, 0, 0, 0, 0)),
                  pl.BlockSpec((N_KV, r, n), lambda bi: (0, 0, 0))],
        out_specs=[pl.BlockSpec((None, N_KV, r, HEAD_DIM), lambda bi: (bi, 0, 0, 0)),
                   pl.BlockSpec((None, N_KV, r, n), lambda bi: (bi, 0, 0, 0))],
        out_shape=[jax.ShapeDtypeStruct((b, N_KV, r, HEAD_DIM), F32), jax.ShapeDtypeStruct((b, N_KV, r, n), F32)],
        compiler_params=_cparams(1), name="cmp_decode",
    )(q, kvc, bias)


def _rows_of(x, t):
    b = x.shape[0]
    return jnp.transpose(x.reshape(b, t, N_KV, GROUP, HEAD_DIM), (0, 2, 3, 1, 4)).reshape(b, N_KV, GROUP * t, HEAD_DIM)


def _unrows(o, t):
    b = o.shape[0]
    return jnp.transpose(o.reshape(b, N_KV, GROUP, t, HEAD_DIM), (0, 3, 1, 2, 4)).reshape(b, t, MIX_W)


def _head_rows(a, t):
    b = a.shape[0]
    return a.reshape(b, N_KV, GROUP * t, a.shape[-1])


def _mixers_sample(l, h, P, caches, page_table, past_len):
    b, t, d = h.shape
    n = b * t
    n_pages = page_table.shape[1]
    pp = 8
    ns = n_pages // pp
    lkp = (ns + 1) * pp * PAGE_SIZE
    proj = _mm(h.reshape(n, d), P["w_in_r"], P["b_in_r"], l).reshape(b, t, PROJ_W)
    seg = lambda c, w: proj[:, :, c * LANE:(c + w) * LANE]
    misc = proj[:, :, MISC * LANE:MISC * LANE + 4 * N_HEADS]
    kvs = lambda c: seg(c, 4).reshape(b, t, 2, N_KV, HEAD_DIM)
    pad_new = lambda kv: jnp.pad(kv, ((0, 0), (0, PAGE_SIZE - t), (0, 0), (0, 0), (0, 0)))
    logf_new = _log_sigmoid(jnp.pad(jnp.swapaxes(misc[..., :N_HEADS], 1, 2), ((0, 0), (0, 0), (0, LANE - t))))[..., :t]
    rows = dict(fox_kv=kvs(KVF), fox_logf=jnp.swapaxes(logf_new, 1, 2), diff_kv=kvs(KVD), moba_kv=kvs(KVM),
                nsa_cmp_kv=kvs(KVC), nsa_slc_kv=kvs(KVS))
    win_new = kvs(KVW)
    win_past = caches["nsa_win_kv"][l]
    w_len = win_past.shape[1]
    rows["nsa_win_kv"] = jnp.concatenate([win_past, win_new], axis=1)[:, -min(NSA_WINDOW, w_len + t):]

    qpos = past_len + jnp.arange(t)
    kpos = jnp.arange(lkp)
    kvalid = (kpos[None, :] <= qpos[:, None]) & (kpos[None, :] < past_len + t)
    dist = qpos[:, None] - kpos[None, :]

    def t5_bias(ti):
        tb = jnp.moveaxis(P["rel_bias"][:, ti][_t5_bucket(dist)], -1, 0)
        return jnp.where(kvalid[None], tb, NEG)

    logf_past = caches["fox_logf"][l][page_table].reshape(b, past_len, N_HEADS)
    lf = jnp.concatenate([jnp.swapaxes(logf_past, 1, 2), logf_new], axis=2)
    lf = jnp.pad(lf, ((0, 0), (0, 0), (0, lkp - past_len - t)))
    f_cum = _cumsum(lf)
    f_q = f_cum[:, :, past_len:past_len + t]
    bias_fox = jnp.where(kvalid[None, None], f_q[..., None] - f_cum[:, :, None, :], NEG)
    o_fox = _decode_attn(_rows_of(seg(QF, 8), t), caches["fox_kv"], l, page_table, pad_new(rows["fox_kv"]),
                         _head_rows(bias_fox, t), pp)

    lam_init = 0.8 - 0.6 * math.exp(-0.3 * l)
    o_diff = _decode_attn(_rows_of(seg(QD, 8), t), caches["diff_kv"], l, page_table, pad_new(rows["diff_kv"]),
                          _head_rows(t5_bias(0)[None], t), pp,
                          diff=(P["diff_lambda"][l], P["diff_norm_g"][l].reshape(1, HEAD_DIM), lam_init))

    lk = past_len + t
    nblk = -(-lk // MOBA_BLOCK)
    ppb = MOBA_BLOCK // PAGE_SIZE
    psum = _page_key_sums(caches["moba_kv"], l, page_table, pp)
    kmean = psum.reshape(b, n_pages // ppb, ppb, N_KV, HEAD_DIM).sum(2) / MOBA_BLOCK
    q_m = seg(QM, 8).reshape(b, t, N_KV, GROUP, HEAD_DIM)
    gs = jnp.einsum("btkgd,bmkd->bkgtm", q_m, kmean, precision=lax.Precision.HIGHEST)
    own = qpos // MOBA_BLOCK
    blk_ids = jnp.arange(n_pages // ppb)
    gs = jnp.where(blk_ids[None, :] < own[:, None], gs, -jnp.inf)
    top_v, top_i = lax.top_k(gs, min(MOBA_TOPK, nblk))
    chosen = jnp.any((top_i[..., None] == blk_ids) & jnp.isfinite(top_v)[..., None], axis=-2)
    kblk = kpos // MOBA_BLOCK
    in_chosen = jnp.take_along_axis(
        jnp.pad(chosen, ((0, 0),) * 4 + ((0, 1),)),
        jnp.broadcast_to(jnp.minimum(kblk, n_pages // ppb), chosen.shape[:4] + (lkp,)), axis=-1)
    keep = in_chosen | (kblk[None, :] == own[:, None])
    bias_moba = jnp.where(keep, t5_bias(1).reshape(N_KV, GROUP, t, lkp)[None], NEG)
    o_moba = _decode_attn(_rows_of(seg(QM, 8), t), caches["moba_kv"], l, page_table, pad_new(rows["moba_kv"]),
                          bias_moba.reshape(b, N_KV, GROUP * t, lkp), pp)

    n_chunk = lk // NSA_CMP_STRIDE
    r_len = NSA_CMP_LEN // NSA_CMP_STRIDE
    n_cmp = n_chunk - r_len + 1
    n_slc = -(-lk // NSA_SLC_BLOCK)
    cmp_past = caches["nsa_cmp_kv"][l][page_table].reshape(b, past_len, 2, N_KV, HEAD_DIM)
    cmp_full = jnp.concatenate([cmp_past, rows["nsa_cmp_kv"]], axis=1)[:, :n_chunk * NSA_CMP_STRIDE]
    chunks = jnp.transpose(cmp_full, (0, 2, 3, 1, 4)).reshape(b, 2, N_KV, n_chunk, NSA_CMP_STRIDE * HEAD_DIM)
    kvc = _nsa_compress(chunks, P["w_cmp"], l)
    cmp_end = jnp.arange(n_chunk) * NSA_CMP_STRIDE + (NSA_CMP_LEN - 1)
    cvalid = (cmp_end[None, :] <= qpos[:, None]) & (jnp.arange(n_chunk)[None, :] < n_cmp)
    bias_c = jnp.moveaxis(P["rel_bias"][:, 2][_t5_bucket(qpos[:, None] - cmp_end[None, :])], -1, 0)
    bias_c = jnp.where(cvalid[None], bias_c, NEG).reshape(N_KV, GROUP * t, n_chunk)
    q_n = _rows_of(seg(QN, 8), t)
    o_cmp, pc = _cmp_decode(q_n, kvc, bias_c)
    cover = jnp.asarray(_cover_matrix(n_cmp, n_slc))
    imp = jnp.einsum("bkgtm,mj->bktj", pc.reshape(b, N_KV, GROUP, t, n_chunk)[..., :n_cmp], cover,
                     precision=lax.Precision.HIGHEST)
    cur = (qpos // NSA_SLC_BLOCK)[:, None]
    sl_ids = jnp.arange(n_slc)
    forced = (sl_ids[None, :] == 0) | ((sl_ids[None, :] <= cur) & (sl_ids[None, :] > cur - NSA_N_LOCAL))
    imp = jnp.where(forced, NSA_FORCE, jnp.where(sl_ids[None, :] <= cur, imp, -1.0))
    top_v, top_i = lax.top_k(imp, min(NSA_TOPN, n_slc))
    picked = jnp.any((top_i[..., None] == sl_ids) & (top_v >= 0)[..., None], axis=-2)
    kslc = jnp.minimum(kpos // NSA_SLC_BLOCK, n_slc - 1)
    keep_s = jnp.take_along_axis(picked, jnp.broadcast_to(kslc, picked.shape[:3] + (lkp,)), axis=-1)
    bias_t5n = t5_bias(2).reshape(N_KV, GROUP, t, lkp)
    bias_slc = jnp.where(keep_s[:, :, None], bias_t5n[None], NEG).reshape(b, N_KV, GROUP * t, lkp)
    o_slc = _decode_attn(q_n, caches["nsa_slc_kv"], l, page_table, pad_new(rows["nsa_slc_kv"]), bias_slc, pp)

    wp = w_len // PAGE_SIZE
    pool_w = caches["nsa_win_kv"].reshape(DEPTH, b * wp, PAGE_SIZE, 2, N_KV, HEAD_DIM)
    pt_w = (jnp.arange(b, dtype=jnp.int32)[:, None] * wp + jnp.arange(wp, dtype=jnp.int32)[None, :])
    lkw = 2 * wp * PAGE_SIZE
    kidx = jnp.arange(lkw)
    qidx = w_len + jnp.arange(t)
    dw = qidx[:, None] - kidx[None, :]
    wvalid = (dw >= 0) & (dw <= NSA_WINDOW) & (kidx[None, :] < w_len + t)
    bias_w = jnp.moveaxis(P["rel_bias"][:, 2][_t5_bucket(dw)], -1, 0)
    bias_w = jnp.where(wvalid[None], bias_w, NEG).reshape(1, N_KV, GROUP * t, lkw)
    o_win = _decode_attn(q_n, pool_w, l, pt_w, pad_new(win_new), bias_w, wp)

    g = _sigmoid(misc[..., N_HEADS:].reshape(b, t, 3, N_HEADS))[..., None]
    hd = lambda o: _unrows(o, t).reshape(b, t, N_HEADS, HEAD_DIM)
    o_nsa = (g[:, :, 0] * hd(o_cmp) + g[:, :, 1] * hd(o_slc) + g[:, :, 2] * hd(o_win)).reshape(b, t, MIX_W)
    o_all = jnp.stack([_unrows(o_fox, t), _unrows(o_diff, t), _unrows(o_moba, t), o_nsa]).astype(BF).reshape(4, n, MIX_W)
    return o_all, rows


STATE_NAMES = ("fox_kv", "fox_logf", "diff_kv", "moba_kv", "nsa_cmp_kv", "nsa_slc_kv", "nsa_win_kv")


def _run_group(x, mod, P, mixers):
    b, t, d = x.shape
    n = b * t
    rows = {name: [] for name in STATE_NAMES}
    h = _modulate(x, mod, 0)
    for l in range(DEPTH):
        o_all, new_rows = mixers(l, h)
        merged = _gate_merge(h.reshape(n, d), o_all, P["w_gate"], P["b_gate"], P["w_branch"], l)
        y = _mm(merged, P["w_o"], P["zero_bias"], l).reshape(b, t, d)
        x, h2 = _ln_mod(x, y, None, mod, P["ln_g"], P["ln_b"], l, 0, l, 3)
        y_slots, gates = _moe(h2, P["w_router"], P["b_router"], P["w_gu"], P["b_gu"], P["w_dn"], P["b_dn"], l)
        x, h = _ln_mod(x, y_slots, gates, mod, P["ln_g"], P["ln_b"], l, 1, l + 1 if l + 1 < DEPTH else None, 0)
        for name in STATE_NAMES:
            rows[name].append(new_rows[name])
    return x, {name: jnp.stack(rows[name]) for name in STATE_NAMES}


def kernel(x_prompt, x_sample, cache_fox_kv, cache_fox_logf, cache_diff_kv, cache_moba_kv, cache_nsa_cmp_kv,
           cache_nsa_slc_kv, cache_nsa_win_kv, page_table, c_prompt, c_sample, w_ada, b_ada, ln_g, ln_b, w_in, b_in,
           diff_lambda, diff_norm_g, rel_bias, w_cmp, w_gate, b_gate, w_branch, w_o, w_router, b_router,
           w_gu, b_gu, w_dn, b_dn):
    bp = x_prompt.shape[0]
    bs = x_sample.shape[0]
    past_len = page_table.shape[1] * PAGE_SIZE

    o_f = MIX_W + KV_W
    o_g = D_IN - 3 * N_HEADS

    def reorder(w):
        pad = jnp.zeros(w.shape[:-1] + (PROJ_W - D_IN,), w.dtype)
        return jnp.concatenate([w[..., :o_f], w[..., o_f + N_HEADS:o_g], w[..., o_f:o_f + N_HEADS], w[..., o_g:], pad], -1)

    P = dict(w_in_r=reorder(w_in), b_in_r=reorder(b_in).reshape(DEPTH, 1, PROJ_W), diff_lambda=diff_lambda,
             diff_norm_g=diff_norm_g, rel_bias=rel_bias, w_cmp=w_cmp, w_gate=w_gate, b_gate=b_gate, w_branch=w_branch,
             w_o=w_o, zero_bias=jnp.zeros((DEPTH, 1, D_MODEL), F32), w_router=w_router, b_router=b_router,
             w_gu=w_gu, b_gu=b_gu, w_dn=w_dn, b_dn=b_dn, ln_g=ln_g, ln_b=ln_b)

    r_pad = -(-(bp + bs) // 8) * 8
    c_all = jnp.pad(jnp.concatenate([c_prompt, c_sample], 0), ((0, r_pad - bp - bs), (0, 0)))
    mod_all = _ada_mod(c_all, w_ada, b_ada).reshape(DEPTH, r_pad, 6, D_MODEL)
    mod_p = mod_all[:, :bp]
    mod_s = mod_all[:, bp:bp + bs]

    n_off = x_prompt.shape[1] // TB
    tabs = [_t5_tiles(rel_bias[:, i], n_off) for i in range(3)]
    y_prompt, sp = _run_group(x_prompt, mod_p, P, lambda l, h: _mixers_prompt(l, h, P, tabs))

    caches = dict(fox_kv=cache_fox_kv, fox_logf=cache_fox_logf, diff_kv=cache_diff_kv, moba_kv=cache_moba_kv,
                  nsa_cmp_kv=cache_nsa_cmp_kv, nsa_slc_kv=cache_nsa_slc_kv, nsa_win_kv=cache_nsa_win_kv)
    y_sample, ss = _run_group(x_sample, mod_s, P, lambda l, h: _mixers_sample(l, h, P, caches, page_table, past_len))
    return (y_prompt, y_sample,
            sp["fox_kv"], sp["fox_logf"], sp["diff_kv"], sp["moba_kv"], sp["nsa_cmp_kv"], sp["nsa_slc_kv"], sp["nsa_win_kv"],
            ss["fox_kv"], ss["fox_logf"], ss["diff_kv"], ss["moba_kv"], ss["nsa_cmp_kv"], ss["nsa_slc_kv"], ss["nsa_win_kv"])
```

```python
import functools
import math

import jax
import jax.numpy as jnp
import numpy as np
from jax import lax
from jax.experimental import pallas as pl
from jax.experimental.pallas import tpu as pltpu

D_MODEL = 4096
DEPTH = 2
PAGE_SIZE = 128
HEAD_DIM = 128
N_HEADS = 8
N_KV = 2
GROUP = N_HEADS // N_KV
MIX_W = N_HEADS * HEAD_DIM
KV_W = 2 * N_KV * HEAD_DIM
DIFF_DH = HEAD_DIM // 2
SCALE = HEAD_DIM ** -0.5
DIFF_SCALE = DIFF_DH ** -0.5
MOBA_BLOCK = 256
MOBA_TOPK = 3
NSA_CMP_LEN = 32
NSA_CMP_STRIDE = 16
NSA_SLC_BLOCK = 64
NSA_TOPN = 16
NSA_N_LOCAL = 2
NSA_WINDOW = 512
NSA_FORCE = 1e9
N_BUCKETS = 32
MAX_DISTANCE = 4096
N_EXPERTS = 32
TOP_K = 4
D_EXPERT = D_MODEL // 4
SWIGLU_ALPHA = 1.702
SWIGLU_LIMIT = 7.0
DN_ALPHA = (2 * DEPTH) ** 0.25
LN_EPS = 1e-5
IN_SPLITS = (MIX_W, KV_W, N_HEADS, MIX_W, KV_W, MIX_W, KV_W, MIX_W, KV_W, KV_W, KV_W, 3 * N_HEADS)
D_IN = sum(IN_SPLITS)

LANE = 128
QF, KVF, QD, KVD, QM, KVM, QN, KVC, KVS, KVW, MISC = 0, 8, 12, 20, 24, 32, 36, 44, 48, 52, 56
PROJ_W = 60 * LANE
NEG = -1e30
NEG_TEST = -1e29
TB = 512
VMEM_LIMIT = 56 * 1024 * 1024
BF = jnp.bfloat16
F32 = jnp.float32


def _cparams(n_axes):
    return pltpu.CompilerParams(dimension_semantics=("arbitrary",) * n_axes, vmem_limit_bytes=VMEM_LIMIT)


def _dot(a, b):
    return jnp.dot(a, b, preferred_element_type=F32)


def _dot_nt(a, b):
    return lax.dot_general(a, b, (((1,), (1,)), ((), ())), preferred_element_type=F32)


def _dot_hi(a, b):
    return jnp.dot(a, b, precision=lax.Precision.HIGHEST, preferred_element_type=F32)


def _sigmoid(x):
    return 1.0 / (1.0 + jnp.exp(-x))


def _cast_rows(src_ref, dst_ref, rows, chunk=256):
    def body(r, c):
        sl = pl.ds(pl.multiple_of(r * chunk, chunk), chunk)
        dst_ref[sl, :] = src_ref[sl, :].astype(BF)
        return c
    lax.fori_loop(0, rows // chunk, body, 0)


def _ada_kernel(c_ref, w_ref, b_ref, o_ref):
    c = c_ref[...]
    a = (c * _sigmoid(c)).astype(BF)
    o_ref[...] = _dot(a, w_ref[...].astype(BF)) + b_ref[...]


def _ada_mod(c_all, w_ada, b_ada):
    r = c_all.shape[0]
    tn = 512
    n6 = 6 * D_MODEL
    return pl.pallas_call(
        _ada_kernel,
        grid=(DEPTH, n6 // tn),
        in_specs=[pl.BlockSpec((r, D_MODEL), lambda l, n: (0, 0)),
                  pl.BlockSpec((None, D_MODEL, tn), lambda l, n: (l, 0, n)),
                  pl.BlockSpec((None, 1, tn), lambda l, n: (l, 0, n))],
        out_specs=pl.BlockSpec((None, r, tn), lambda l, n: (l, 0, n)),
        out_shape=jax.ShapeDtypeStruct((DEPTH, r, n6), F32),
        compiler_params=_cparams(2), name="ada_mod",
    )(c_all, w_ada, b_ada.reshape(DEPTH, 1, n6))


def _modulate_kernel(x_ref, m_ref, o_ref):
    o_ref[...] = (x_ref[...] * (1.0 + m_ref[1:2, :]) + m_ref[0:1, :]).astype(BF)


def _modulate(x, mod, l):
    b, t, d = x.shape
    tm = min(t, 512)
    return pl.pallas_call(
        _modulate_kernel,
        grid=(b, t // tm),
        in_specs=[pl.BlockSpec((None, tm, d), lambda i, j: (i, j, 0)),
                  pl.BlockSpec((None, None, 6, d), lambda i, j: (l, i, 0, 0))],
        out_specs=pl.BlockSpec((None, tm, d), lambda i, j: (i, j, 0)),
        out_shape=jax.ShapeDtypeStruct((b, t, d), BF),
        compiler_params=_cparams(2), name="modulate",
    )(x, mod)


def _mm_kernel(x_ref, w_ref, b_ref, o_ref, wbf_ref):
    @pl.when(pl.program_id(1) == 0)
    def _():
        _cast_rows(w_ref, wbf_ref, w_ref.shape[0])
    o_ref[...] = (_dot(x_ref[...], wbf_ref[...]) + b_ref[...]).astype(o_ref.dtype)


def _mm(x, w, bias, l, out_dtype=F32, tn=512):
    m, k = x.shape
    n = w.shape[-1]
    tm = min(m, 512)
    return pl.pallas_call(
        _mm_kernel,
        grid=(n // tn, m // tm),
        in_specs=[pl.BlockSpec((tm, k), lambda j, i: (i, 0)),
                  pl.BlockSpec((None, k, tn), lambda j, i: (l, 0, j)),
                  pl.BlockSpec((None, 1, tn), lambda j, i: (l, 0, j))],
        out_specs=pl.BlockSpec((tm, tn), lambda j, i: (i, j)),
        out_shape=jax.ShapeDtypeStruct((m, n), out_dtype),
        scratch_shapes=[pltpu.VMEM((k, tn), BF)],
        compiler_params=_cparams(2), name="mm",
    )(x, w, bias)


def _gate_merge_kernel(h_ref, o_ref, wg_ref, bg_ref, wb_ref, out_ref, acc_ref, wgbf_ref, wbbf_ref):
    br = pl.program_id(2)

    @pl.when(br == 0)
    def _():
        acc_ref[...] = jnp.zeros_like(acc_ref)

    _cast_rows(wg_ref, wgbf_ref, wg_ref.shape[0])
    _cast_rows(wb_ref, wbbf_ref, wb_ref.shape[0])
    g = _dot(h_ref[...], wgbf_ref[...]) + bg_ref[...]
    u = _dot(o_ref[...], wbbf_ref[...])
    acc_ref[...] += _sigmoid(g) * u

    @pl.when(br == 3)
    def _():
        out_ref[...] = acc_ref[...].astype(out_ref.dtype)


def _gate_merge(h, o_all, w_gate, b_gate, w_branch, l):
    m = h.shape[0]
    tm = min(m, 1024)
    tn = 256
    return pl.pallas_call(
        _gate_merge_kernel,
        grid=(D_MODEL // tn, m // tm, 4),
        in_specs=[pl.BlockSpec((tm, D_MODEL), lambda j, i, b: (i, 0)),
                  pl.BlockSpec((None, tm, MIX_W), lambda j, i, b: (b, i, 0)),
                  pl.BlockSpec((None, None, D_MODEL, tn), lambda j, i, b: (l, b, 0, j)),
                  pl.BlockSpec((None, None, 1, tn), lambda j, i, b: (l, b, 0, j)),
                  pl.BlockSpec((None, None, MIX_W, tn), lambda j, i, b: (l, b, 0, j))],
        out_specs=pl.BlockSpec((tm, tn), lambda j, i, b: (i, j)),
        out_shape=jax.ShapeDtypeStruct((m, D_MODEL), BF),
        scratch_shapes=[pltpu.VMEM((tm, tn), F32), pltpu.VMEM((D_MODEL, tn), BF), pltpu.VMEM((MIX_W, tn), BF)],
        compiler_params=_cparams(3), name="gate_merge",
    )(h, o_all, w_gate, b_gate.reshape(DEPTH, 4, 1, D_MODEL), w_branch)


def _pack_halves(hb):
    u = lax.bitcast_convert_type(hb.astype(F32), jnp.uint32)
    w = hb.shape[-1] // 2
    return (u[:, :w] >> 16) | (u[:, w:] & jnp.uint32(0xFFFF0000))


def _unpack_halves(xw):
    lo = lax.bitcast_convert_type(xw << 16, F32).astype(BF)
    hi = lax.bitcast_convert_type(xw & jnp.uint32(0xFFFF0000), F32).astype(BF)
    return lo, hi


def _ln_kernel(*refs, n_slots, gate_row, shift_row, emit_h, emit_packed):
    it = iter(refs)
    x_ref, y_ref = next(it), next(it)
    wt_ref = next(it) if n_slots > 1 else None
    ma_ref, mb_ref, g_ref, b_ref = next(it), next(it), next(it), next(it)
    xo_ref = next(it)
    ho_ref = next(it) if emit_h else None
    po_ref = next(it) if emit_packed else None
    if n_slots > 1:
        y = y_ref[:, 0, :] * wt_ref[:, 0:1]
        for s in range(1, n_slots):
            y = y + y_ref[:, s, :] * wt_ref[:, s:s + 1]
    else:
        y = y_ref[...]
    z = DN_ALPHA * x_ref[...] + (1.0 + ma_ref[gate_row:gate_row + 1, :]) * y
    mu = jnp.mean(z, axis=-1, keepdims=True)
    zc = z - mu
    var = jnp.mean(zc * zc, axis=-1, keepdims=True)
    xn = zc * lax.rsqrt(var + LN_EPS) * g_ref[...] + b_ref[...]
    xo_ref[...] = xn
    if emit_h:
        hb = (xn * (1.0 + mb_ref[shift_row + 1:shift_row + 2, :]) + mb_ref[shift_row:shift_row + 1, :]).astype(BF)
        ho_ref[...] = hb
        if emit_packed:
            po_ref[...] = _pack_halves(hb)


def _ln_mod(x, y, wts, mod, ln_g, ln_b, l, which, l_next, shift_row):
    b, t, d = x.shape
    n_slots = 1 if wts is None else y.shape[2]
    tm = min(t, 256 if n_slots == 1 else 128)
    emit_h = l_next is not None
    emit_packed = emit_h and which == 0
    gate_row = 2 if which == 0 else 5
    ln_i = which
    in_specs = [pl.BlockSpec((None, tm, d), lambda i, j: (i, j, 0))]
    args = [x, y]
    if n_slots > 1:
        in_specs += [pl.BlockSpec((None, tm, n_slots, d), lambda i, j: (i, j, 0, 0)),
                     pl.BlockSpec((None, tm, n_slots), lambda i, j: (i, j, 0))]
        args.append(wts)
    else:
        in_specs.append(pl.BlockSpec((None, tm, d), lambda i, j: (i, j, 0)))
    lb = l if l_next is None else l_next
    in_specs += [pl.BlockSpec((None, None, 6, d), lambda i, j: (l, i, 0, 0)),
                 pl.BlockSpec((None, None, 6, d), lambda i, j: (lb, i, 0, 0)),
                 pl.BlockSpec((None, None, 1, d), lambda i, j: (l, ln_i, 0, 0)),
                 pl.BlockSpec((None, None, 1, d), lambda i, j: (l, ln_i, 0, 0))]
    args += [mod, mod, ln_g.reshape(DEPTH, 2, 1, d), ln_b.reshape(DEPTH, 2, 1, d)]
    out_specs = [pl.BlockSpec((None, tm, d), lambda i, j: (i, j, 0))]
    out_shape = [jax.ShapeDtypeStruct((b, t, d), F32)]
    if emit_h:
        out_specs.append(pl.BlockSpec((None, tm, d), lambda i, j: (i, j, 0)))
        out_shape.append(jax.ShapeDtypeStruct((b, t, d), BF))
    if emit_packed:
        out_specs.append(pl.BlockSpec((None, tm, d // 2), lambda i, j: (i, j, 0)))
        out_shape.append(jax.ShapeDtypeStruct((b, t, d // 2), jnp.uint32))
    res = pl.pallas_call(
        functools.partial(_ln_kernel, n_slots=n_slots, gate_row=gate_row, shift_row=shift_row, emit_h=emit_h,
                          emit_packed=emit_packed),
        grid=(b, t // tm), in_specs=in_specs, out_specs=out_specs, out_shape=out_shape,
        compiler_params=_cparams(2), name="ln_mod",
    )(*args)
    return tuple(res) + (None,) * (3 - len(res))


def _gather_rows_kernel(idx_ref, src_ref, out_ref, sem, *, tr):
    def row_copy(r, src_row):
        return pltpu.make_async_copy(src_ref.at[pl.ds(src_row, 1)], out_ref.at[pl.ds(r, 1)], sem)

    def issue(r, c):
        row_copy(r, idx_ref[0, r]).start()
        return c
    lax.fori_loop(0, tr, issue, 0, unroll=8)

    def wait(r, c):
        row_copy(r, 0).wait()
        return c
    lax.fori_loop(0, tr, wait, 0, unroll=8)


def _gather_rows(src, idx):
    n_out = idx.shape[0]
    w = src.shape[1]
    tr = next(c for c in (256, 128, 64, 32, 16, 8) if n_out % c == 0)
    nt = n_out // tr
    return pl.pallas_call(
        functools.partial(_gather_rows_kernel, tr=tr), grid=(nt,),
        in_specs=[pl.BlockSpec((None, 1, tr), lambda t: (t, 0, 0), memory_space=pltpu.SMEM),
                  pl.BlockSpec(memory_space=pl.ANY)],
        out_specs=pl.BlockSpec((tr, w), lambda t: (t, 0)),
        out_shape=jax.ShapeDtypeStruct((n_out, w), src.dtype),
        scratch_shapes=[pltpu.SemaphoreType.DMA(())],
        compiler_params=_cparams(1), name="gather_rows",
    )(idx.reshape(nt, 1, tr), src)


def _topk_lanes(vals, lane, k, floor):
    picks = []
    lane_f = lane.astype(F32)
    for _ in range(k):
        mx = jnp.max(vals, axis=-1, keepdims=True)
        idx = jnp.min(jnp.where(vals == mx, lane_f, 4096.0), axis=-1, keepdims=True)
        picks.append((mx, idx))
        vals = jnp.where(lane_f == idx, floor, vals)
    return picks


def _router_kernel(x_ref, w_ref, b_ref, gate_ref, idx_ref):
    logits = _dot_hi(x_ref[...].astype(F32), w_ref[...]) + b_ref[...]
    lane = lax.broadcasted_iota(jnp.int32, logits.shape, 1)
    vals = jnp.where(lane < N_EXPERTS, logits, NEG)
    picks = _topk_lanes(vals, lane, TOP_K, -3e38)
    v0 = picks[0][0]
    es = [jnp.exp(v - v0) for v, _ in picks]
    z = es[0] + es[1] + es[2] + es[3]
    gates = jnp.zeros(logits.shape, F32)
    idxs = jnp.zeros(logits.shape, F32)
    for k in range(TOP_K):
        gates = jnp.where(lane == k, es[k] / z, gates)
        idxs = jnp.where(lane == k, picks[k][1], idxs)
    gate_ref[...] = gates
    idx_ref[...] = idxs.astype(jnp.int32)


def _router(h2, w_router, b_router, l):
    m = h2.shape[0]
    tm = min(m, 512)
    w = jnp.pad(w_router, ((0, 0), (0, 0), (0, LANE - N_EXPERTS)))
    bb = jnp.pad(b_router, ((0, 0), (0, LANE - N_EXPERTS))).reshape(DEPTH, 1, LANE)
    return pl.pallas_call(
        _router_kernel,
        grid=(m // tm,),
        in_specs=[pl.BlockSpec((tm, D_MODEL), lambda i: (i, 0)),
                  pl.BlockSpec((None, D_MODEL, LANE), lambda i: (l, 0, 0)),
                  pl.BlockSpec((None, 1, LANE), lambda i: (l, 0, 0))],
        out_specs=[pl.BlockSpec((tm, LANE), lambda i: (i, 0)), pl.BlockSpec((tm, LANE), lambda i: (i, 0))],
        out_shape=[jax.ShapeDtypeStruct((m, LANE), F32), jax.ShapeDtypeStruct((m, LANE), jnp.int32)],
        compiler_params=_cparams(1), name="router",
    )(h2, w, bb)


def _moe_up_kernel(te_ref, tv_ref, x_ref, wg_ref, wu_ref, bg_ref, bu_ref, o_ref, xlo_ref, xhi_ref):
    t = pl.program_id(0)
    half = D_MODEL // 2

    @pl.when(pl.program_id(1) == 0)
    def _():
        xlo_ref[...], xhi_ref[...] = _unpack_halves(x_ref[...])

    @pl.when(tv_ref[t] > 0)
    def _():
        xlo = xlo_ref[...]
        xhi = xhi_ref[...]
        g = _dot(xlo, wg_ref[:half, :].astype(BF)) + _dot(xhi, wg_ref[half:, :].astype(BF)) + bg_ref[...]
        u = _dot(xlo, wu_ref[:half, :].astype(BF)) + _dot(xhi, wu_ref[half:, :].astype(BF)) + bu_ref[...]
        g = jnp.minimum(g, SWIGLU_LIMIT)
        u = jnp.clip(u, -SWIGLU_LIMIT, SWIGLU_LIMIT)
        o_ref[...] = ((u + 1.0) * g * _sigmoid(SWIGLU_ALPHA * g)).astype(o_ref.dtype)

    @pl.when(tv_ref[t] == 0)
    def _():
        o_ref[...] = jnp.zeros_like(o_ref)


def _moe_dn_kernel(te_ref, tv_ref, a_ref, w_ref, b_ref, o_ref):
    t = pl.program_id(0)

    @pl.when(tv_ref[t] > 0)
    def _():
        o_ref[...] = _dot(a_ref[...], w_ref[...].astype(BF)) + b_ref[...]

    @pl.when(tv_ref[t] == 0)
    def _():
        o_ref[...] = jnp.zeros_like(o_ref)


def _moe_experts(x_rows, tile_e, tile_v, w_gu, b_gu, w_dn, b_dn, l, tm):
    r = x_rows.shape[0]
    nt = r // tm
    tn = 256
    nj = D_EXPERT // tn
    b_gu4 = b_gu.reshape(DEPTH, N_EXPERTS, 1, 2 * D_EXPERT)
    act = pl.pallas_call(
        _moe_up_kernel,
        grid_spec=pltpu.PrefetchScalarGridSpec(
            num_scalar_prefetch=2, grid=(nt, nj),
            in_specs=[pl.BlockSpec((tm, D_MODEL // 2), lambda t, j, te, tv: (t, 0)),
                      pl.BlockSpec((None, None, D_MODEL, tn), lambda t, j, te, tv: (l, te[t], 0, j * tv[t])),
                      pl.BlockSpec((None, None, D_MODEL, tn), lambda t, j, te, tv: (l, te[t], 0, nj + j * tv[t])),
                      pl.BlockSpec((None, None, 1, tn), lambda t, j, te, tv: (l, te[t], 0, j * tv[t])),
                      pl.BlockSpec((None, None, 1, tn), lambda t, j, te, tv: (l, te[t], 0, nj + j * tv[t]))],
            out_specs=pl.BlockSpec((tm, tn), lambda t, j, te, tv: (t, j)),
            scratch_shapes=[pltpu.VMEM((tm, D_MODEL // 2), BF), pltpu.VMEM((tm, D_MODEL // 2), BF)]),
        out_shape=jax.ShapeDtypeStruct((r, D_EXPERT), BF),
        compiler_params=_cparams(2), name="moe_up",
    )(tile_e, tile_v, x_rows, w_gu, w_gu, b_gu4, b_gu4)
    tn2 = 1024
    return pl.pallas_call(
        _moe_dn_kernel,
        grid_spec=pltpu.PrefetchScalarGridSpec(
            num_scalar_prefetch=2, grid=(nt, D_MODEL // tn2),
            in_specs=[pl.BlockSpec((tm, D_EXPERT), lambda t, j, te, tv: (t, 0)),
                      pl.BlockSpec((None, None, D_EXPERT, tn2), lambda t, j, te, tv: (l, te[t], 0, j * tv[t])),
                      pl.BlockSpec((None, None, 1, tn2), lambda t, j, te, tv: (l, te[t], 0, j * tv[t]))],
            out_specs=pl.BlockSpec((tm, tn2), lambda t, j, te, tv: (t, j))),
        out_shape=jax.ShapeDtypeStruct((r, D_MODEL), F32),
        compiler_params=_cparams(2), name="moe_dn",
    )(tile_e, tile_v, act, w_dn, b_dn.reshape(DEPTH, N_EXPERTS, 1, D_MODEL))


def _moe(h2, h2_packed, w_router, b_router, w_gu, b_gu, w_dn, b_dn, l):
    b, t, d = h2.shape
    n_tok = b * t
    xs = h2.reshape(n_tok, d)
    gate_l, idx_l = _router(xs, w_router, b_router, l)
    gates = gate_l[:, :TOP_K]
    e_flat = idx_l[:, :TOP_K].reshape(-1)
    n_asg = n_tok * TOP_K
    tm = 512 if n_asg >= 512 * N_EXPERTS else 16
    order = jnp.argsort(e_flat, stable=True)
    counts = jnp.sum(jax.nn.one_hot(e_flat, N_EXPERTS, dtype=jnp.int32), axis=0)
    padded = (counts + tm - 1) // tm * tm
    pad_end = jnp.cumsum(padded)
    pad_start = pad_end - padded
    start = jnp.cumsum(counts) - counts
    e_sorted = e_flat[order]
    dest_sorted = pad_start[e_sorted] + jnp.arange(n_asg, dtype=jnp.int32) - start[e_sorted]
    n_rows = (n_asg // tm + N_EXPERTS) * tm
    nt = n_rows // tm
    row_tok = jnp.zeros((n_rows,), jnp.int32).at[dest_sorted].set((order // TOP_K).astype(jnp.int32))
    pos = jnp.zeros((n_asg,), jnp.int32).at[order].set(dest_sorted.astype(jnp.int32))
    tile_start = jnp.arange(nt, dtype=jnp.int32) * tm
    tile_e = jnp.minimum(jnp.searchsorted(pad_end, tile_start, side="right"), N_EXPERTS - 1).astype(jnp.int32)
    tile_v = (tile_start < pad_end[-1]).astype(jnp.int32)
    x_rows = _gather_rows(h2_packed.reshape(n_tok, d // 2), row_tok)
    y_rows = _moe_experts(x_rows, tile_e, tile_v, w_gu, b_gu, w_dn, b_dn, l, tm)
    y_slots = _gather_rows(y_rows, pos).reshape(b, t, TOP_K, d)
    return y_slots, gates.reshape(b, t, TOP_K)


def _t5_bucket(dist):
    n = jnp.maximum(dist, 0)
    exact = N_BUCKETS // 2
    nf = jnp.maximum(n, 1).astype(F32)
    large = exact + (jnp.log(nf / exact) / math.log(MAX_DISTANCE / exact) * (N_BUCKETS - exact)).astype(jnp.int32)
    return jnp.where(n < exact, n, jnp.minimum(large, N_BUCKETS - 1))


def _toeplitz_kernel(g_ref, o_ref):
    x = jnp.broadcast_to(g_ref[...], (TB, 2 * TB))
    o_ref[...] = pltpu.roll(x, TB + 1, 1, stride=1, stride_axis=0)[:, :TB]


def _t5_tiles(rel_bias, n_off):
    o = jnp.arange(n_off)[:, None]
    y = jnp.arange(2 * TB)[None, :]
    g = jnp.transpose(rel_bias[_t5_bucket(o * TB + (TB - 1) - y)], (2, 3, 0, 1))
    n_mh = 3 * N_HEADS
    return pl.pallas_call(
        _toeplitz_kernel, grid=(n_mh, n_off),
        in_specs=[pl.BlockSpec((None, None, 1, 2 * TB), lambda m, i: (m, i, 0, 0))],
        out_specs=pl.BlockSpec((None, None, TB, TB), lambda m, i: (m, i, 0, 0)),
        out_shape=jax.ShapeDtypeStruct((n_mh, n_off, TB, TB), F32),
        compiler_params=_cparams(2), name="t5_tiles",
    )(g.reshape(n_mh, n_off, 1, 2 * TB))


def _lam_of(dl_ref, lam_init):
    a = jnp.sum(dl_ref[0:1, :] * dl_ref[1:2, :], axis=-1, keepdims=True)
    b = jnp.sum(dl_ref[2:3, :] * dl_ref[3:4, :], axis=-1, keepdims=True)
    return jnp.exp(a) - jnp.exp(b) + lam_init


def _online_update(s, mask, v, m_ref, l_ref, acc_ref):
    s = jnp.where(mask, s, NEG)
    m_prev = m_ref[...]
    m_new = jnp.maximum(m_prev, jnp.max(s, axis=-1, keepdims=True))
    alpha = jnp.exp(m_prev - m_new)
    p = jnp.where(mask, jnp.exp(s - m_new), 0.0)
    l_ref[...] = alpha * l_ref[...] + jnp.sum(p, axis=-1, keepdims=True)
    acc_ref[...] = alpha * acc_ref[...] + _dot(p.astype(BF), v)
    m_ref[...] = m_new


def _normalized(l_ref, acc_ref):
    l = l_ref[...]
    return acc_ref[...] / jnp.where(l > 0, l, 1.0)


def _flash_kernel(*refs, nq, band, fox, sel, diff, gated, addend, lam_init):
    it = iter(refs)
    q_ref, k_ref, v_ref = next(it), next(it), next(it)
    if fox:
        fq_ref, fk_ref = next(it), next(it)
    else:
        tab_ref = next(it)
    if sel:
        sel_ref, e_ref = next(it), next(it)
    if gated:
        g_ref = next(it)
    if addend:
        add_ref = next(it)
    if diff:
        dl_ref, gn_ref = next(it), next(it)
    o_ref = next(it)
    n_maps = 2 if diff else 1
    stats = [(next(it), next(it), next(it)) for _ in range(n_maps)]

    i = pl.program_id(2)
    jj = pl.program_id(3)
    if band:
        j = i - 1 + jj
        active = j >= 0
        last = jj == 1
    else:
        j = jj
        active = jj <= i
        last = jj == nq - 1

    @pl.when(jj == 0)
    def _():
        for m_ref, l_ref, acc_ref in stats:
            m_ref[...] = jnp.full_like(m_ref, NEG)
            l_ref[...] = jnp.zeros_like(l_ref)
            acc_ref[...] = jnp.zeros_like(acc_ref)

    @pl.when(active)
    def _():
        q = q_ref[...]
        k = k_ref[...].astype(BF)
        v = v_ref[...].astype(BF)
        row = lax.broadcasted_iota(jnp.int32, (TB, TB), 0)
        col = lax.broadcasted_iota(jnp.int32, (TB, TB), 1)
        dpos = (i - j) * TB + row - col
        mask = dpos >= 0
        if band:
            mask = mask & (dpos <= NSA_WINDOW)
        if sel:
            mask = mask & (_dot(sel_ref[...].astype(BF), e_ref[...]) > 0.5)
        bias = (fq_ref[...] - fk_ref[...]) if fox else tab_ref[...]
        if diff:
            lane = lax.broadcasted_iota(jnp.int32, q.shape, 1)
            for mi, (m_ref, l_ref, acc_ref) in enumerate(stats):
                half = (lane < DIFF_DH) if mi == 0 else (lane >= DIFF_DH)
                qs = jnp.where(half, q * DIFF_SCALE, 0.0).astype(BF)
                _online_update(_dot_nt(qs, k) + bias, mask, v, m_ref, l_ref, acc_ref)
        else:
            m_ref, l_ref, acc_ref = stats[0]
            _online_update(_dot_nt((q * SCALE).astype(BF), k) + bias, mask, v, m_ref, l_ref, acc_ref)

    @pl.when(last)
    def _():
        o = _normalized(stats[0][1], stats[0][2])
        if diff:
            o = o - _lam_of(dl_ref, lam_init) * _normalized(stats[1][1], stats[1][2])
            o = o * lax.rsqrt(jnp.mean(o * o, axis=-1, keepdims=True) + LN_EPS) * gn_ref[...] * (1.0 - lam_init)
        if gated:
            o = _sigmoid(g_ref[...]) * o
        if addend:
            o = o + add_ref[...]
        o_ref[...] = o.astype(o_ref.dtype)


def _flash_prefill(proj, qcol, kvcol, *, fq=None, fk=None, tab=None, tab_i=0, sel=None, sel_e=None, sel_per_kv=False,
                   band=False, gates=None, gate_i=0, addend=None, diff=None, out_dtype=BF):
    b, t, _ = proj.shape
    nq = t // TB
    nkk = 2 if band else nq
    if band:
        jmap = lambda i, jj: jnp.maximum(i - 1 + jj, 0)
    else:
        jmap = lambda i, jj: jnp.minimum(jj, i)
    in_specs = [pl.BlockSpec((None, TB, LANE), lambda bi, h, i, jj: (bi, i, qcol + h)),
                pl.BlockSpec((None, TB, LANE), lambda bi, h, i, jj: (bi, jmap(i, jj), kvcol + h // GROUP)),
                pl.BlockSpec((None, TB, LANE), lambda bi, h, i, jj: (bi, jmap(i, jj), kvcol + N_KV + h // GROUP))]
    args = [proj, proj, proj]
    if fq is not None:
        in_specs += [pl.BlockSpec((None, None, TB, 1), lambda bi, h, i, jj: (bi, h, i, 0)),
                     pl.BlockSpec((None, None, 1, TB), lambda bi, h, i, jj: (bi, h, 0, jmap(i, jj)))]
        args += [fq, fk]
    else:
        in_specs.append(pl.BlockSpec((None, None, TB, TB),
                                     lambda bi, h, i, jj: (tab_i * N_HEADS + h, i - jmap(i, jj), 0, 0)))
        args.append(tab)
    if sel is not None:
        hs = (lambda h: h // GROUP) if sel_per_kv else (lambda h: h)
        in_specs += [pl.BlockSpec((None, None, TB, LANE), lambda bi, h, i, jj: (bi, hs(h), i, 0)),
                     pl.BlockSpec((None, LANE, TB), lambda bi, h, i, jj: (jmap(i, jj), 0, 0))]
        args += [sel, sel_e]
    if gates is not None:
        in_specs.append(pl.BlockSpec((None, None, None, TB, 1), lambda bi, h, i, jj: (bi, gate_i, h, i, 0)))
        args.append(gates)
    if addend is not None:
        in_specs.append(pl.BlockSpec((None, TB, LANE), lambda bi, h, i, jj: (bi, i, h)))
        args.append(addend)
    lam_init = 0.0
    if diff is not None:
        dl, gn, lam_init = diff
        in_specs += [pl.BlockSpec((4, DIFF_DH), lambda bi, h, i, jj: (0, 0)),
                     pl.BlockSpec((1, HEAD_DIM), lambda bi, h, i, jj: (0, 0))]
        args += [dl, gn]
    n_maps = 2 if diff is not None else 1
    scratch = []
    for _ in range(n_maps):
        scratch += [pltpu.VMEM((TB, 1), F32), pltpu.VMEM((TB, 1), F32), pltpu.VMEM((TB, HEAD_DIM), F32)]
    kern = functools.partial(_flash_kernel, nq=nq, band=band, fox=fq is not None, sel=sel is not None,
                             diff=diff is not None, gated=gates is not None, addend=addend is not None,
                             lam_init=lam_init)
    return pl.pallas_call(
        kern, grid=(b, N_HEADS, nq, nkk), in_specs=in_specs,
        out_specs=pl.BlockSpec((None, TB, LANE), lambda bi, h, i, jj: (bi, i, h)),
        out_shape=jax.ShapeDtypeStruct((b, t, MIX_W), out_dtype),
        scratch_shapes=scratch, compiler_params=_cparams(4), name="flash_prefill",
    )(*args)


def _logsig_kernel(x_ref, o_ref):
    x = x_ref[...]
    o_ref[...] = jnp.minimum(x, 0.0) - jnp.log(1.0 + jnp.exp(-jnp.abs(x)))


def _log_sigmoid(x):
    b, h, t = x.shape
    return pl.pallas_call(
        _logsig_kernel, grid=(b,),
        in_specs=[pl.BlockSpec((None, h, t), lambda i: (i, 0, 0))],
        out_specs=pl.BlockSpec((None, h, t), lambda i: (i, 0, 0)),
        out_shape=jax.ShapeDtypeStruct((b, h, t), F32), compiler_params=_cparams(1), name="log_sigmoid",
    )(x)


CS_CHUNK = 512


def _cumsum_kernel(x_ref, o_ref, carry_ref):
    @pl.when(pl.program_id(1) == 0)
    def _():
        carry_ref[...] = jnp.zeros_like(carry_ref)
    r = lax.broadcasted_iota(jnp.int32, (CS_CHUNK, CS_CHUNK), 0)
    c = lax.broadcasted_iota(jnp.int32, (CS_CHUNK, CS_CHUNK), 1)
    tri = (r <= c).astype(F32)
    y = _dot_hi(x_ref[...], tri) + carry_ref[...]
    o_ref[...] = y
    carry_ref[...] = y[:, CS_CHUNK - 1:CS_CHUNK]


def _cumsum(x):
    b, h, n = x.shape
    return pl.pallas_call(
        _cumsum_kernel, grid=(b, n // CS_CHUNK),
        in_specs=[pl.BlockSpec((None, h, CS_CHUNK), lambda i, j: (i, 0, j))],
        out_specs=pl.BlockSpec((None, h, CS_CHUNK), lambda i, j: (i, 0, j)),
        out_shape=jax.ShapeDtypeStruct((b, h, n), F32),
        scratch_shapes=[pltpu.VMEM((h, 1), F32)], compiler_params=_cparams(2), name="cumsum",
    )(x)


def _moba_gate_kernel(q_ref, k_ref, sel_ref, *, t):
    nblk = t // MOBA_BLOCK
    r = lax.broadcasted_iota(jnp.int32, (LANE, t), 0)
    c = lax.broadcasted_iota(jnp.int32, (LANE, t), 1)
    avg = jnp.where(c // MOBA_BLOCK == r, 1.0 / MOBA_BLOCK, 0.0)
    kmean = _dot_hi(avg, k_ref[...])
    gs = lax.dot_general(q_ref[...], kmean, (((1,), (1,)), ((), ())), precision=lax.Precision.HIGHEST,
                         preferred_element_type=F32)
    lane = lax.broadcasted_iota(jnp.int32, gs.shape, 1)
    own = lax.broadcasted_iota(jnp.int32, gs.shape, 0) // MOBA_BLOCK
    vals = jnp.where(lane < own, gs, NEG)
    sel = (lane == own).astype(F32)
    for mx, idx in _topk_lanes(vals, lane, min(MOBA_TOPK, nblk), -3e38):
        sel = jnp.where((lane.astype(F32) == idx) & (mx > NEG_TEST), 1.0, sel)
    sel_ref[...] = sel


def _moba_gate(proj):
    b, t, _ = proj.shape
    return pl.pallas_call(
        functools.partial(_moba_gate_kernel, t=t), grid=(b, N_HEADS),
        in_specs=[pl.BlockSpec((None, t, LANE), lambda bi, h: (bi, 0, QM + h)),
                  pl.BlockSpec((None, t, LANE), lambda bi, h: (bi, 0, KVM + h // GROUP))],
        out_specs=pl.BlockSpec((None, None, t, LANE), lambda bi, h: (bi, h, 0, 0)),
        out_shape=jax.ShapeDtypeStruct((b, N_HEADS, t, LANE), F32),
        compiler_params=_cparams(2), name="moba_gate",
    )(proj, proj)


def _compress_kernel(c_ref, w_ref, o_ref):
    o_ref[...] = _dot(c_ref[...].astype(BF), w_ref[...].astype(BF))


def _nsa_compress(chunks, w_cmp, l):
    b, _, _, n, sd = chunks.shape
    rt = min(n, 512)
    r_len = NSA_CMP_LEN // NSA_CMP_STRIDE
    w = w_cmp.reshape(DEPTH, 2, r_len, sd, HEAD_DIM)
    y = pl.pallas_call(
        _compress_kernel, grid=(b, 2, N_KV, r_len, n // rt),
        in_specs=[pl.BlockSpec((None, None, None, rt, sd), lambda bi, j, k, r, i: (bi, j, k, i, 0)),
                  pl.BlockSpec((None, None, None, sd, HEAD_DIM), lambda bi, j, k, r, i: (l, j, r, 0, 0))],
        out_specs=pl.BlockSpec((None, None, None, None, rt, HEAD_DIM), lambda bi, j, k, r, i: (bi, j, k, r, i, 0)),
        out_shape=jax.ShapeDtypeStruct((b, 2, N_KV, r_len, n, HEAD_DIM), F32),
        compiler_params=_cparams(5), name="nsa_compress",
    )(chunks, w)
    return y[:, :, :, 0] + jnp.pad(y[:, :, :, 1, 1:], ((0, 0), (0, 0), (0, 0), (0, 1), (0, 0)))


def _nsa_cmp_kernel(q_ref, kc_ref, vc_ref, tab_ref, cover_ref, g_ref, o_ref, sel_ref, *, n_cmp, n_slc):
    i = pl.program_id(2)
    shape = (TB, LANE)
    qpos = i * TB + lax.broadcasted_iota(jnp.int32, shape, 0)
    lane = lax.broadcasted_iota(jnp.int32, shape, 1)
    maskc = (lane * NSA_CMP_STRIDE + (NSA_CMP_LEN - 1) <= qpos) & (lane < n_cmp)
    kc = kc_ref[...].astype(BF)
    vc = vc_ref[...].astype(BF)
    imp = jnp.zeros(shape, F32)
    for g in range(GROUP):
        q = (q_ref[:, g * HEAD_DIM:(g + 1) * HEAD_DIM] * SCALE).astype(BF)
        s = jnp.where(maskc, _dot_nt(q, kc) + tab_ref[g], NEG)
        mx = jnp.max(s, axis=-1, keepdims=True)
        p = jnp.where(maskc, jnp.exp(s - mx), 0.0)
        z = jnp.sum(p, axis=-1, keepdims=True)
        pc = p / jnp.where(z > 0, z, 1.0)
        o_ref[:, g * HEAD_DIM:(g + 1) * HEAD_DIM] = _sigmoid(g_ref[g]) * _dot(pc.astype(BF), vc)
        imp = imp + _dot_hi(pc, cover_ref[...])
    cur = qpos // NSA_SLC_BLOCK
    forced = (lane == 0) | ((lane <= cur) & (lane > cur - NSA_N_LOCAL))
    vals = jnp.where(forced, NSA_FORCE, jnp.where(lane <= cur, imp, -1.0))
    vals = jnp.where(lane < n_slc, vals, NEG)
    sel = jnp.zeros(shape, F32)
    for mx, idx in _topk_lanes(vals, lane, min(NSA_TOPN, n_slc), -3e38):
        sel = jnp.where((lane.astype(F32) == idx) & (mx >= 0.0), 1.0, sel)
    sel_ref[...] = sel


def _nsa_cmp_prefill(proj, kvc, tabc, cover, gates, n_cmp, n_slc):
    b, t, _ = proj.shape
    nq = t // TB
    return pl.pallas_call(
        functools.partial(_nsa_cmp_kernel, n_cmp=n_cmp, n_slc=n_slc), grid=(b, N_KV, nq),
        in_specs=[pl.BlockSpec((None, TB, GROUP * LANE), lambda bi, k, i: (bi, i, QN // GROUP + k)),
                  pl.BlockSpec((None, None, None, LANE, HEAD_DIM), lambda bi, k, i: (bi, 0, k, 0, 0)),
                  pl.BlockSpec((None, None, None, LANE, HEAD_DIM), lambda bi, k, i: (bi, 1, k, 0, 0)),
                  pl.BlockSpec((GROUP, TB, LANE), lambda bi, k, i: (k, i, 0)),
                  pl.BlockSpec((LANE, LANE), lambda bi, k, i: (0, 0)),
                  pl.BlockSpec((None, None, GROUP, TB, 1), lambda bi, k, i: (bi, 0, k, i, 0))],
        out_specs=[pl.BlockSpec((None, TB, GROUP * LANE), lambda bi, k, i: (bi, i, k)),
                   pl.BlockSpec((None, None, TB, LANE), lambda bi, k, i: (bi, k, i, 0))],
        out_shape=[jax.ShapeDtypeStruct((b, t, MIX_W), F32), jax.ShapeDtypeStruct((b, N_KV, t, LANE), F32)],
        compiler_params=_cparams(3), name="nsa_cmp_prefill",
    )(proj, kvc, kvc, tabc, cover, gates)


def _cover_matrix(n_cmp, n_slc):
    c_start = np.arange(n_cmp) * NSA_CMP_STRIDE
    s_ids = np.arange(n_slc)
    return ((c_start[:, None] < (s_ids[None, :] + 1) * NSA_SLC_BLOCK)
            & (c_start[:, None] + NSA_CMP_LEN > s_ids[None, :] * NSA_SLC_BLOCK)).astype(np.float32)


def _sel_expand(n_tiles, block):
    j = np.arange(n_tiles)[:, None, None]
    m = np.arange(LANE)[None, :, None]
    c = np.arange(TB)[None, None, :]
    return jnp.asarray(m == (j * TB + c) // block, BF)


def _mixers_prompt(l, h, P, tabs):
    b, t, d = h.shape
    n = b * t
    proj = _mm(h.reshape(n, d), P["w_in_r"], P["b_in_r"], l).reshape(b, t, PROJ_W)
    misc = proj[:, :, MISC * LANE:MISC * LANE + N_HEADS + 3 * N_HEADS]
    logf = _log_sigmoid(jnp.swapaxes(misc[..., :N_HEADS], 1, 2))
    gates = jnp.transpose(misc[..., N_HEADS:].reshape(b, t, 3, N_HEADS), (0, 2, 3, 1))[..., None]
    kvs = lambda c: proj[:, :, c * LANE:(c + 4) * LANE].reshape(b, t, 2, N_KV, HEAD_DIM)
    rows = dict(fox_kv=kvs(KVF), fox_logf=jnp.swapaxes(logf, 1, 2), diff_kv=kvs(KVD), moba_kv=kvs(KVM),
                nsa_cmp_kv=kvs(KVC), nsa_slc_kv=kvs(KVS))
    win = kvs(KVW)
    rows["nsa_win_kv"] = win[:, -min(NSA_WINDOW, t):]

    f_cum = _cumsum(logf)
    o_fox = _flash_prefill(proj, QF, KVF, fq=f_cum[..., None], fk=f_cum[:, :, None, :])

    lam_init = 0.8 - 0.6 * math.exp(-0.3 * l)
    o_diff = _flash_prefill(proj, QD, KVD, tab=tabs, tab_i=0,
                            diff=(P["diff_lambda"][l], P["diff_norm_g"][l].reshape(1, HEAD_DIM), lam_init))

    nk = t // TB
    o_moba = _flash_prefill(proj, QM, KVM, tab=tabs, tab_i=1, sel=_moba_gate(proj), sel_e=_sel_expand(nk, MOBA_BLOCK))

    n_chunk = t // NSA_CMP_STRIDE
    n_cmp = n_chunk - NSA_CMP_LEN // NSA_CMP_STRIDE + 1
    n_slc = -(-t // NSA_SLC_BLOCK)
    chunks = jnp.transpose(rows["nsa_cmp_kv"], (0, 2, 3, 1, 4)).reshape(b, 2, N_KV, n_chunk, NSA_CMP_STRIDE * HEAD_DIM)
    kvc = _nsa_compress(chunks, P["w_cmp"], l)
    cmp_end = jnp.arange(LANE) * NSA_CMP_STRIDE + (NSA_CMP_LEN - 1)
    tabc = jnp.moveaxis(P["rel_bias"][:, 2][_t5_bucket(jnp.arange(t)[:, None] - cmp_end[None, :])], -1, 0)
    cover = jnp.asarray(np.pad(_cover_matrix(n_cmp, n_slc), ((0, LANE - n_cmp), (0, LANE - n_slc))))
    o1, sel_n = _nsa_cmp_prefill(proj, kvc, tabc, cover, gates, n_cmp, n_slc)
    o2 = _flash_prefill(proj, QN, KVS, tab=tabs, tab_i=2, sel=sel_n, sel_e=_sel_expand(nk, NSA_SLC_BLOCK), sel_per_kv=True,
                        gates=gates, gate_i=1, addend=o1, out_dtype=F32)
    o_nsa = _flash_prefill(proj, QN, KVW, tab=tabs, tab_i=2, band=True, gates=gates, gate_i=2, addend=o2)
    o_all = jnp.stack([o_fox, o_diff, o_moba, o_nsa]).reshape(4, n, MIX_W)
    return o_all, rows


def _decode_kernel(*refs, pp, ns, diff, lam_init):
    it = iter(refs)
    _pt_ref = next(it)
    q_ref = next(it)
    page_refs = [next(it) for _ in range(pp)]
    new_ref, bias_ref = next(it), next(it)
    if diff:
        dl_ref, gn_ref = next(it), next(it)
    o_ref = next(it)
    m_ref, l_ref, acc_ref = next(it), next(it), next(it)
    n_maps = 2 if diff else 1
    s_id = pl.program_id(1)

    @pl.when(s_id == 0)
    def _():
        m_ref[...] = jnp.full_like(m_ref, NEG)
        l_ref[...] = jnp.zeros_like(l_ref)
        acc_ref[...] = jnp.zeros_like(acc_ref)

    def process(page_list):
        n_keys = len(page_list) * PAGE_SIZE
        for kv in range(N_KV):
            k = jnp.concatenate([ref[:, 0, kv, :] for ref in page_list], axis=0).astype(BF)
            v = jnp.concatenate([ref[:, 1, kv, :] for ref in page_list], axis=0).astype(BF)
            bias = bias_ref[kv, :, :n_keys]
            mask = bias > NEG_TEST
            q = q_ref[kv]
            for mi in range(n_maps):
                if diff:
                    lane = lax.broadcasted_iota(jnp.int32, q.shape, 1)
                    half = (lane < DIFF_DH) if mi == 0 else (lane >= DIFF_DH)
                    qs = jnp.where(half, q * DIFF_SCALE, 0.0).astype(BF)
                else:
                    qs = (q * SCALE).astype(BF)
                _online_update(_dot_nt(qs, k) + bias, mask, v, m_ref.at[mi, kv], l_ref.at[mi, kv], acc_ref.at[mi, kv])

    @pl.when(s_id < ns)
    def _():
        process(page_refs)

    @pl.when(s_id == ns)
    def _():
        process([new_ref])
        for kv in range(N_KV):
            o = _normalized(l_ref.at[0, kv], acc_ref.at[0, kv])
            if diff:
                o = o - _lam_of(dl_ref, lam_init) * _normalized(l_ref.at[1, kv], acc_ref.at[1, kv])
                o = o * lax.rsqrt(jnp.mean(o * o, axis=-1, keepdims=True) + LN_EPS) * gn_ref[...] * (1.0 - lam_init)
            o_ref[kv] = o


def _decode_attn(q, pool, l, page_table, new_kv, bias, pp, diff=None):
    b, _, r, _ = q.shape
    n_pages = page_table.shape[1]
    ns = n_pages // pp
    bb = bias.shape[0]
    blk = (None, None, PAGE_SIZE, 2, N_KV, HEAD_DIM)
    page_specs = [pl.BlockSpec(blk, (lambda bi, s, pt, i=i: (l, pt[bi, jnp.minimum(s, ns - 1) * pp + i], 0, 0, 0, 0)))
                  for i in range(pp)]
    in_specs = ([pl.BlockSpec((None, N_KV, r, HEAD_DIM), lambda bi, s, pt: (bi, 0, 0, 0))] + page_specs +
                [pl.BlockSpec((None, PAGE_SIZE, 2, N_KV, HEAD_DIM), lambda bi, s, pt: (bi, 0, 0, 0, 0)),
                 pl.BlockSpec((None, N_KV, r, pp * PAGE_SIZE), lambda bi, s, pt: (bi if bb > 1 else 0, 0, 0, s))])
    args = [q] + [pool] * pp + [new_kv, bias]
    lam_init = 0.0
    if diff is not None:
        dl, gn, lam_init = diff
        in_specs += [pl.BlockSpec((4, DIFF_DH), lambda bi, s, pt: (0, 0)),
                     pl.BlockSpec((1, HEAD_DIM), lambda bi, s, pt: (0, 0))]
        args += [dl, gn]
    n_maps = 2 if diff is not None else 1
    return pl.pallas_call(
        functools.partial(_decode_kernel, pp=pp, ns=ns, diff=diff is not None, lam_init=lam_init),
        grid_spec=pltpu.PrefetchScalarGridSpec(
            num_scalar_prefetch=1, grid=(b, ns + 1), in_specs=in_specs,
            out_specs=pl.BlockSpec((None, N_KV, r, HEAD_DIM), lambda bi, s, pt: (bi, 0, 0, 0)),
            scratch_shapes=[pltpu.VMEM((n_maps, N_KV, r, 1), F32), pltpu.VMEM((n_maps, N_KV, r, 1), F32),
                            pltpu.VMEM((n_maps, N_KV, r, HEAD_DIM), F32)]),
        out_shape=jax.ShapeDtypeStruct((b, N_KV, r, HEAD_DIM), F32),
        compiler_params=_cparams(2), name="decode_attn",
    )(page_table, *args)


def _page_sum_kernel(*refs, pp):
    _pt_ref = refs[0]
    page_refs = refs[1:1 + pp]
    o_ref = refs[1 + pp]
    for i, ref in enumerate(page_refs):
        for kv in range(N_KV):
            o_ref[i, kv:kv + 1, :] = jnp.sum(ref[:, kv, :], axis=0, keepdims=True)


def _page_key_sums(pool, l, page_table, pp):
    b, n_pages = page_table.shape
    blk = (None, None, PAGE_SIZE, None, N_KV, HEAD_DIM)
    page_specs = [pl.BlockSpec(blk, (lambda bi, s, pt, i=i: (l, pt[bi, s * pp + i], 0, 0, 0, 0))) for i in range(pp)]
    return pl.pallas_call(
        functools.partial(_page_sum_kernel, pp=pp),
        grid_spec=pltpu.PrefetchScalarGridSpec(
            num_scalar_prefetch=1, grid=(b, n_pages // pp), in_specs=page_specs,
            out_specs=pl.BlockSpec((None, pp, N_KV, HEAD_DIM), lambda bi, s, pt: (bi, s, 0, 0))),
        out_shape=jax.ShapeDtypeStruct((b, n_pages, N_KV, HEAD_DIM), F32),
        compiler_params=_cparams(2), name="page_key_sums",
    )(page_table, *([pool] * pp))


def _cmp_decode_kernel(q_ref, kvc_ref, bias_ref, o_ref, p_ref):
    for kv in range(N_KV):
        q = (q_ref[kv] * SCALE).astype(BF)
        bias = bias_ref[kv]
        mask = bias > NEG_TEST
        s = jnp.where(mask, _dot_nt(q, kvc_ref[0, kv].astype(BF)) + bias, NEG)
        mx = jnp.max(s, axis=-1, keepdims=True)
        p = jnp.where(mask, jnp.exp(s - mx), 0.0)
        z = jnp.sum(p, axis=-1, keepdims=True)
        pc = p / jnp.where(z > 0, z, 1.0)
        p_ref[kv] = pc
        o_ref[kv] = _dot(pc.astype(BF), kvc_ref[1, kv].astype(BF))


def _cmp_decode(q, kvc, bias):
    b, _, r, _ = q.shape
    n = kvc.shape[3]
    return pl.pallas_call(
        _cmp_decode_kernel, grid=(b,),
        in_specs=[pl.BlockSpec((None, N_KV, r, HEAD_DIM), lambda bi: (bi, 0, 0, 0)),
                  pl.BlockSpec((None, 2, N_KV, n, HEAD_DIM), lambda bi: (bi, 0, 0, 0, 0)),
                  pl.BlockSpec((N_KV, r, n), lambda bi: (0, 0, 0))],
        out_specs=[pl.BlockSpec((None, N_KV, r, HEAD_DIM), lambda bi: (bi, 0, 0, 0)),
                   pl.BlockSpec((None, N_KV, r, n), lambda bi: (bi, 0, 0, 0))],
        out_shape=[jax.ShapeDtypeStruct((b, N_KV, r, HEAD_DIM), F32), jax.ShapeDtypeStruct((b, N_KV, r, n), F32)],
        compiler_params=_cparams(1), name="cmp_decode",
    )(q, kvc, bias)


def _rows_of(x, t):
    b = x.shape[0]
    return jnp.transpose(x.reshape(b, t, N_KV, GROUP, HEAD_DIM), (0, 2, 3, 1, 4)).reshape(b, N_KV, GROUP * t, HEAD_DIM)


def _unrows(o, t):
    b = o.shape[0]
    return jnp.transpose(o.reshape(b, N_KV, GROUP, t, HEAD_DIM), (0, 3, 1, 2, 4)).reshape(b, t, MIX_W)


def _head_rows(a, t):
    b = a.shape[0]
    return a.reshape(b, N_KV, GROUP * t, a.shape[-1])


DEC_PP = 16


def _sample_tables(rel_bias, t, n_pages, w_len, past_len):
    lkp = (n_pages // DEC_PP + 1) * DEC_PP * PAGE_SIZE
    qpos = past_len + jnp.arange(t)
    kpos = jnp.arange(lkp)
    kvalid = (kpos[None, :] <= qpos[:, None]) & (kpos[None, :] < past_len + t)
    t5 = jnp.transpose(rel_bias[_t5_bucket(qpos[:, None] - kpos[None, :])], (2, 3, 0, 1))
    t5 = jnp.where(kvalid[None, None], t5, NEG)
    n_chunk = (past_len + t) // NSA_CMP_STRIDE
    n_cmp = n_chunk - NSA_CMP_LEN // NSA_CMP_STRIDE + 1
    cmp_end = jnp.arange(n_chunk) * NSA_CMP_STRIDE + (NSA_CMP_LEN - 1)
    cvalid = (cmp_end[None, :] <= qpos[:, None]) & (jnp.arange(n_chunk)[None, :] < n_cmp)
    bias_c = jnp.moveaxis(rel_bias[:, 2][_t5_bucket(qpos[:, None] - cmp_end[None, :])], -1, 0)
    bias_c = jnp.where(cvalid[None], bias_c, NEG).reshape(N_KV, GROUP * t, n_chunk)
    wp = w_len // PAGE_SIZE
    lkw = 2 * wp * PAGE_SIZE
    kidx = jnp.arange(lkw)
    dw = (w_len + jnp.arange(t))[:, None] - kidx[None, :]
    wvalid = (dw >= 0) & (dw <= NSA_WINDOW) & (kidx[None, :] < w_len + t)
    bias_w = jnp.moveaxis(rel_bias[:, 2][_t5_bucket(dw)], -1, 0)
    bias_w = jnp.where(wvalid[None], bias_w, NEG).reshape(1, N_KV, GROUP * t, lkw)
    return dict(kvalid=kvalid, t5=t5, bias_c=bias_c, bias_w=bias_w)


def _mixers_sample(l, h, P, caches, page_table, past_len, tables):
    b, t, d = h.shape
    n = b * t
    n_pages = page_table.shape[1]
    pp = DEC_PP
    ns = n_pages // pp
    lkp = (ns + 1) * pp * PAGE_SIZE
    proj = _mm(h.reshape(n, d), P["w_in_r"], P["b_in_r"], l).reshape(b, t, PROJ_W)
    seg = lambda c, w: proj[:, :, c * LANE:(c + w) * LANE]
    misc = proj[:, :, MISC * LANE:MISC * LANE + 4 * N_HEADS]
    kvs = lambda c: seg(c, 4).reshape(b, t, 2, N_KV, HEAD_DIM)
    pad_new = lambda kv: jnp.pad(kv, ((0, 0), (0, PAGE_SIZE - t), (0, 0), (0, 0), (0, 0)))
    logf_new = _log_sigmoid(jnp.pad(jnp.swapaxes(misc[..., :N_HEADS], 1, 2), ((0, 0), (0, 0), (0, LANE - t))))[..., :t]
    rows = dict(fox_kv=kvs(KVF), fox_logf=jnp.swapaxes(logf_new, 1, 2), diff_kv=kvs(KVD), moba_kv=kvs(KVM),
                nsa_cmp_kv=kvs(KVC), nsa_slc_kv=kvs(KVS))
    win_new = kvs(KVW)
    win_past = caches["nsa_win_kv"][l]
    w_len = win_past.shape[1]
    rows["nsa_win_kv"] = jnp.concatenate([win_past, win_new], axis=1)[:, -min(NSA_WINDOW, w_len + t):]

    qpos = past_len + jnp.arange(t)
    kpos = jnp.arange(lkp)
    kvalid = tables["kvalid"]
    t5_bias = lambda ti: tables["t5"][ti]

    logf_past = caches["fox_logf"][l][page_table].reshape(b, past_len, N_HEADS)
    lf = jnp.concatenate([jnp.swapaxes(logf_past, 1, 2), logf_new], axis=2)
    lf = jnp.pad(lf, ((0, 0), (0, 0), (0, lkp - past_len - t)))
    f_cum = _cumsum(lf)
    f_q = f_cum[:, :, past_len:past_len + t]
    bias_fox = jnp.where(kvalid[None, None], f_q[..., None] - f_cum[:, :, None, :], NEG)
    o_fox = _decode_attn(_rows_of(seg(QF, 8), t), caches["fox_kv"], l, page_table, pad_new(rows["fox_kv"]),
                         _head_rows(bias_fox, t), pp)

    lam_init = 0.8 - 0.6 * math.exp(-0.3 * l)
    o_diff = _decode_attn(_rows_of(seg(QD, 8), t), caches["diff_kv"], l, page_table, pad_new(rows["diff_kv"]),
                          _head_rows(t5_bias(0)[None], t), pp,
                          diff=(P["diff_lambda"][l], P["diff_norm_g"][l].reshape(1, HEAD_DIM), lam_init))

    lk = past_len + t
    nblk = -(-lk // MOBA_BLOCK)
    ppb = MOBA_BLOCK // PAGE_SIZE
    psum = _page_key_sums(caches["moba_kv"], l, page_table, pp)
    kmean = psum.reshape(b, n_pages // ppb, ppb, N_KV, HEAD_DIM).sum(2) / MOBA_BLOCK
    q_m = seg(QM, 8).reshape(b, t, N_KV, GROUP, HEAD_DIM)
    gs = jnp.einsum("btkgd,bmkd->bkgtm", q_m, kmean, precision=lax.Precision.HIGHEST)
    own = qpos // MOBA_BLOCK
    blk_ids = jnp.arange(n_pages // ppb)
    gs = jnp.where(blk_ids[None, :] < own[:, None], gs, -jnp.inf)
    top_v, top_i = lax.top_k(gs, min(MOBA_TOPK, nblk))
    chosen = jnp.any((top_i[..., None] == blk_ids) & jnp.isfinite(top_v)[..., None], axis=-2)
    kblk = kpos // MOBA_BLOCK
    in_chosen = jnp.take_along_axis(
        jnp.pad(chosen, ((0, 0),) * 4 + ((0, 1),)),
        jnp.broadcast_to(jnp.minimum(kblk, n_pages // ppb), chosen.shape[:4] + (lkp,)), axis=-1)
    keep = in_chosen | (kblk[None, :] == own[:, None])
    bias_moba = jnp.where(keep, t5_bias(1).reshape(N_KV, GROUP, t, lkp)[None], NEG)
    o_moba = _decode_attn(_rows_of(seg(QM, 8), t), caches["moba_kv"], l, page_table, pad_new(rows["moba_kv"]),
                          bias_moba.reshape(b, N_KV, GROUP * t, lkp), pp)

    n_chunk = lk // NSA_CMP_STRIDE
    r_len = NSA_CMP_LEN // NSA_CMP_STRIDE
    n_cmp = n_chunk - r_len + 1
    n_slc = -(-lk // NSA_SLC_BLOCK)
    cmp_past = caches["nsa_cmp_kv"][l][page_table].reshape(b, past_len, 2, N_KV, HEAD_DIM)
    cmp_full = jnp.concatenate([cmp_past, rows["nsa_cmp_kv"]], axis=1)[:, :n_chunk * NSA_CMP_STRIDE]
    chunks = jnp.transpose(cmp_full, (0, 2, 3, 1, 4)).reshape(b, 2, N_KV, n_chunk, NSA_CMP_STRIDE * HEAD_DIM)
    kvc = _nsa_compress(chunks, P["w_cmp"], l)
    q_n = _rows_of(seg(QN, 8), t)
    o_cmp, pc = _cmp_decode(q_n, kvc, tables["bias_c"])
    cover = jnp.asarray(_cover_matrix(n_cmp, n_slc))
    imp = jnp.einsum("bkgtm,mj->bktj", pc.reshape(b, N_KV, GROUP, t, n_chunk)[..., :n_cmp], cover,
                     precision=lax.Precision.HIGHEST)
    cur = (qpos // NSA_SLC_BLOCK)[:, None]
    sl_ids = jnp.arange(n_slc)
    forced = (sl_ids[None, :] == 0) | ((sl_ids[None, :] <= cur) & (sl_ids[None, :] > cur - NSA_N_LOCAL))
    imp = jnp.where(forced, NSA_FORCE, jnp.where(sl_ids[None, :] <= cur, imp, -1.0))
    top_v, top_i = lax.top_k(imp, min(NSA_TOPN, n_slc))
    picked = jnp.any((top_i[..., None] == sl_ids) & (top_v >= 0)[..., None], axis=-2)
    kslc = jnp.minimum(kpos // NSA_SLC_BLOCK, n_slc - 1)
    keep_s = jnp.take_along_axis(picked, jnp.broadcast_to(kslc, picked.shape[:3] + (lkp,)), axis=-1)
    bias_t5n = t5_bias(2).reshape(N_KV, GROUP, t, lkp)
    bias_slc = jnp.where(keep_s[:, :, None], bias_t5n[None], NEG).reshape(b, N_KV, GROUP * t, lkp)
    o_slc = _decode_attn(q_n, caches["nsa_slc_kv"], l, page_table, pad_new(rows["nsa_slc_kv"]), bias_slc, pp)

    wp = w_len // PAGE_SIZE
    pool_w = caches["nsa_win_kv"].reshape(DEPTH, b * wp, PAGE_SIZE, 2, N_KV, HEAD_DIM)
    pt_w = (jnp.arange(b, dtype=jnp.int32)[:, None] * wp + jnp.arange(wp, dtype=jnp.int32)[None, :])
    o_win = _decode_attn(q_n, pool_w, l, pt_w, pad_new(win_new), tables["bias_w"], wp)

    g = _sigmoid(misc[..., N_HEADS:].reshape(b, t, 3, N_HEADS))[..., None]
    hd = lambda o: _unrows(o, t).reshape(b, t, N_HEADS, HEAD_DIM)
    o_nsa = (g[:, :, 0] * hd(o_cmp) + g[:, :, 1] * hd(o_slc) + g[:, :, 2] * hd(o_win)).reshape(b, t, MIX_W)
    o_all = jnp.stack([_unrows(o_fox, t), _unrows(o_diff, t), _unrows(o_moba, t), o_nsa]).astype(BF).reshape(4, n, MIX_W)
    return o_all, rows


STATE_NAMES = ("fox_kv", "fox_logf", "diff_kv", "moba_kv", "nsa_cmp_kv", "nsa_slc_kv", "nsa_win_kv")


def _run_group(x, mod, P, mixers):
    b, t, d = x.shape
    n = b * t
    rows = {name: [] for name in STATE_NAMES}
    h = _modulate(x, mod, 0)
    for l in range(DEPTH):
        o_all, new_rows = mixers(l, h)
        merged = _gate_merge(h.reshape(n, d), o_all, P["w_gate"], P["b_gate"], P["w_branch"], l)
        y = _mm(merged, P["w_o"], P["zero_bias"], l).reshape(b, t, d)
        x, h2, h2_packed = _ln_mod(x, y, None, mod, P["ln_g"], P["ln_b"], l, 0, l, 3)
        y_slots, gates = _moe(h2, h2_packed, P["w_router"], P["b_router"], P["w_gu"], P["b_gu"], P["w_dn"], P["b_dn"], l)
        x, h, _ = _ln_mod(x, y_slots, gates, mod, P["ln_g"], P["ln_b"], l, 1, l + 1 if l + 1 < DEPTH else None, 0)
        for name in STATE_NAMES:
            rows[name].append(new_rows[name])
    return x, {name: jnp.stack(rows[name]) for name in STATE_NAMES}


def kernel(x_prompt, x_sample, cache_fox_kv, cache_fox_logf, cache_diff_kv, cache_moba_kv, cache_nsa_cmp_kv,
           cache_nsa_slc_kv, cache_nsa_win_kv, page_table, c_prompt, c_sample, w_ada, b_ada, ln_g, ln_b, w_in, b_in,
           diff_lambda, diff_norm_g, rel_bias, w_cmp, w_gate, b_gate, w_branch, w_o, w_router, b_router,
           w_gu, b_gu, w_dn, b_dn):
    bp = x_prompt.shape[0]
    bs = x_sample.shape[0]
    past_len = page_table.shape[1] * PAGE_SIZE

    o_f = MIX_W + KV_W
    o_g = D_IN - 3 * N_HEADS

    def reorder(w):
        pad = jnp.zeros(w.shape[:-1] + (PROJ_W - D_IN,), w.dtype)
        return jnp.concatenate([w[..., :o_f], w[..., o_f + N_HEADS:o_g], w[..., o_f:o_f + N_HEADS], w[..., o_g:], pad], -1)

    P = dict(w_in_r=reorder(w_in), b_in_r=reorder(b_in).reshape(DEPTH, 1, PROJ_W), diff_lambda=diff_lambda,
             diff_norm_g=diff_norm_g, rel_bias=rel_bias, w_cmp=w_cmp, w_gate=w_gate, b_gate=b_gate, w_branch=w_branch,
             w_o=w_o, zero_bias=jnp.zeros((DEPTH, 1, D_MODEL), F32), w_router=w_router, b_router=b_router,
             w_gu=w_gu, b_gu=b_gu, w_dn=w_dn, b_dn=b_dn, ln_g=ln_g, ln_b=ln_b)

    r_pad = -(-(bp + bs) // 8) * 8
    c_all = jnp.pad(jnp.concatenate([c_prompt, c_sample], 0), ((0, r_pad - bp - bs), (0, 0)))
    mod_all = _ada_mod(c_all, w_ada, b_ada).reshape(DEPTH, r_pad, 6, D_MODEL)
    mod_p = mod_all[:, :bp]
    mod_s = mod_all[:, bp:bp + bs]

    n_off = x_prompt.shape[1] // TB
    tabs = _t5_tiles(rel_bias, n_off)
    y_prompt, sp = _run_group(x_prompt, mod_p, P, lambda l, h: _mixers_prompt(l, h, P, tabs))

    caches = dict(fox_kv=cache_fox_kv, fox_logf=cache_fox_logf, diff_kv=cache_diff_kv, moba_kv=cache_moba_kv,
                  nsa_cmp_kv=cache_nsa_cmp_kv, nsa_slc_kv=cache_nsa_slc_kv, nsa_win_kv=cache_nsa_win_kv)
    tables = _sample_tables(rel_bias, x_sample.shape[1], page_table.shape[1], cache_nsa_win_kv.shape[2], past_len)
    y_sample, ss = _run_group(x_sample, mod_s, P,
                              lambda l, h: _mixers_sample(l, h, P, caches, page_table, past_len, tables))
    return (y_prompt, y_sample,
            sp["fox_kv"], sp["fox_logf"], sp["diff_kv"], sp["moba_kv"], sp["nsa_cmp_kv"], sp["nsa_slc_kv"], sp["nsa_win_kv"],
            ss["fox_kv"], ss["fox_logf"], ss["diff_kv"], ss["moba_kv"], ss["nsa_cmp_kv"], ss["nsa_slc_kv"], ss["nsa_win_kv"])
```

```python
import functools
import math

import jax
import jax.numpy as jnp
import numpy as np
from jax import lax
from jax.experimental import pallas as pl
from jax.experimental.pallas import tpu as pltpu

D_MODEL = 4096
DEPTH = 2
PAGE_SIZE = 128
HEAD_DIM = 128
N_HEADS = 8
N_KV = 2
GROUP = N_HEADS // N_KV
MIX_W = N_HEADS * HEAD_DIM
KV_W = 2 * N_KV * HEAD_DIM
DIFF_DH = HEAD_DIM // 2
SCALE = HEAD_DIM ** -0.5
DIFF_SCALE = DIFF_DH ** -0.5
MOBA_BLOCK = 256
MOBA_TOPK = 3
NSA_CMP_LEN = 32
NSA_CMP_STRIDE = 16
NSA_SLC_BLOCK = 64
NSA_TOPN = 16
NSA_N_LOCAL = 2
NSA_WINDOW = 512
NSA_FORCE = 1e9
N_BUCKETS = 32
MAX_DISTANCE = 4096
N_EXPERTS = 32
TOP_K = 4
D_EXPERT = D_MODEL // 4
SWIGLU_ALPHA = 1.702
SWIGLU_LIMIT = 7.0
DN_ALPHA = (2 * DEPTH) ** 0.25
LN_EPS = 1e-5
IN_SPLITS = (MIX_W, KV_W, N_HEADS, MIX_W, KV_W, MIX_W, KV_W, MIX_W, KV_W, KV_W, KV_W, 3 * N_HEADS)
D_IN = sum(IN_SPLITS)

LANE = 128
QF, KVF, QD, KVD, QM, KVM, QN, KVC, KVS, KVW, MISC = 0, 8, 12, 20, 24, 32, 36, 44, 48, 52, 56
PROJ_W = 60 * LANE
NEG = -1e30
NEG_TEST = -1e29
TB = 512
VMEM_LIMIT = 56 * 1024 * 1024
BF = jnp.bfloat16
F32 = jnp.float32


def _cparams(n_axes):
    return pltpu.CompilerParams(dimension_semantics=("arbitrary",) * n_axes, vmem_limit_bytes=VMEM_LIMIT)


def _dot(a, b):
    return jnp.dot(a, b, preferred_element_type=F32)


def _dot_nt(a, b):
    return lax.dot_general(a, b, (((1,), (1,)), ((), ())), preferred_element_type=F32)


def _dot_hi(a, b):
    return jnp.dot(a, b, precision=lax.Precision.HIGHEST, preferred_element_type=F32)


def _sigmoid(x):
    return 1.0 / (1.0 + jnp.exp(-x))


def _cast_rows(src_ref, dst_ref, rows, chunk=256):
    def body(r, c):
        sl = pl.ds(pl.multiple_of(r * chunk, chunk), chunk)
        dst_ref[sl, :] = src_ref[sl, :].astype(BF)
        return c
    lax.fori_loop(0, rows // chunk, body, 0)


def _ada_kernel(c_ref, w_ref, b_ref, o_ref):
    c = c_ref[...]
    a = (c * _sigmoid(c)).astype(BF)
    o_ref[...] = _dot(a, w_ref[...].astype(BF)) + b_ref[...]


def _ada_mod(c_all, w_ada, b_ada):
    r = c_all.shape[0]
    tn = 512
    n6 = 6 * D_MODEL
    return pl.pallas_call(
        _ada_kernel,
        grid=(DEPTH, n6 // tn),
        in_specs=[pl.BlockSpec((r, D_MODEL), lambda l, n: (0, 0)),
                  pl.BlockSpec((None, D_MODEL, tn), lambda l, n: (l, 0, n)),
                  pl.BlockSpec((None, 1, tn), lambda l, n: (l, 0, n))],
        out_specs=pl.BlockSpec((None, r, tn), lambda l, n: (l, 0, n)),
        out_shape=jax.ShapeDtypeStruct((DEPTH, r, n6), F32),
        compiler_params=_cparams(2), name="ada_mod",
    )(c_all, w_ada, b_ada.reshape(DEPTH, 1, n6))


def _modulate_kernel(x_ref, m_ref, o_ref):
    o_ref[...] = (x_ref[...] * (1.0 + m_ref[1:2, :]) + m_ref[0:1, :]).astype(BF)


def _modulate(x, mod, l):
    b, t, d = x.shape
    tm = min(t, 512)
    return pl.pallas_call(
        _modulate_kernel,
        grid=(b, t // tm),
        in_specs=[pl.BlockSpec((None, tm, d), lambda i, j: (i, j, 0)),
                  pl.BlockSpec((None, None, 6, d), lambda i, j: (l, i, 0, 0))],
        out_specs=pl.BlockSpec((None, tm, d), lambda i, j: (i, j, 0)),
        out_shape=jax.ShapeDtypeStruct((b, t, d), BF),
        compiler_params=_cparams(2), name="modulate",
    )(x, mod)


def _mm_kernel(x_ref, w_ref, b_ref, o_ref, wbf_ref):
    @pl.when(pl.program_id(1) == 0)
    def _():
        _cast_rows(w_ref, wbf_ref, w_ref.shape[0])
    o_ref[...] = (_dot(x_ref[...], wbf_ref[...]) + b_ref[...]).astype(o_ref.dtype)


def _mm(x, w, bias, l, out_dtype=F32, tn=512):
    m, k = x.shape
    n = w.shape[-1]
    tm = min(m, 512)
    return pl.pallas_call(
        _mm_kernel,
        grid=(n // tn, m // tm),
        in_specs=[pl.BlockSpec((tm, k), lambda j, i: (i, 0)),
                  pl.BlockSpec((None, k, tn), lambda j, i: (l, 0, j)),
                  pl.BlockSpec((None, 1, tn), lambda j, i: (l, 0, j))],
        out_specs=pl.BlockSpec((tm, tn), lambda j, i: (i, j)),
        out_shape=jax.ShapeDtypeStruct((m, n), out_dtype),
        scratch_shapes=[pltpu.VMEM((k, tn), BF)],
        compiler_params=_cparams(2), name="mm",
    )(x, w, bias)


def _gate_merge_kernel(h_ref, o_ref, wg_ref, bg_ref, wb_ref, out_ref, acc_ref, wgbf_ref, wbbf_ref):
    br = pl.program_id(2)

    @pl.when(br == 0)
    def _():
        acc_ref[...] = jnp.zeros_like(acc_ref)

    _cast_rows(wg_ref, wgbf_ref, wg_ref.shape[0])
    _cast_rows(wb_ref, wbbf_ref, wb_ref.shape[0])
    g = _dot(h_ref[...], wgbf_ref[...]) + bg_ref[...]
    u = _dot(o_ref[...], wbbf_ref[...])
    acc_ref[...] += _sigmoid(g) * u

    @pl.when(br == 3)
    def _():
        out_ref[...] = acc_ref[...].astype(out_ref.dtype)


def _gate_merge(h, o_all, w_gate, b_gate, w_branch, l):
    m = h.shape[0]
    tm = min(m, 1024)
    tn = 256
    return pl.pallas_call(
        _gate_merge_kernel,
        grid=(D_MODEL // tn, m // tm, 4),
        in_specs=[pl.BlockSpec((tm, D_MODEL), lambda j, i, b: (i, 0)),
                  pl.BlockSpec((None, tm, MIX_W), lambda j, i, b: (b, i, 0)),
                  pl.BlockSpec((None, None, D_MODEL, tn), lambda j, i, b: (l, b, 0, j)),
                  pl.BlockSpec((None, None, 1, tn), lambda j, i, b: (l, b, 0, j)),
                  pl.BlockSpec((None, None, MIX_W, tn), lambda j, i, b: (l, b, 0, j))],
        out_specs=pl.BlockSpec((tm, tn), lambda j, i, b: (i, j)),
        out_shape=jax.ShapeDtypeStruct((m, D_MODEL), BF),
        scratch_shapes=[pltpu.VMEM((tm, tn), F32), pltpu.VMEM((D_MODEL, tn), BF), pltpu.VMEM((MIX_W, tn), BF)],
        compiler_params=_cparams(3), name="gate_merge",
    )(h, o_all, w_gate, b_gate.reshape(DEPTH, 4, 1, D_MODEL), w_branch)


def _pack_halves(hb):
    u = lax.bitcast_convert_type(hb.astype(F32), jnp.uint32)
    w = hb.shape[-1] // 2
    return (u[:, :w] >> 16) | (u[:, w:] & jnp.uint32(0xFFFF0000))


def _unpack_halves(xw):
    lo = lax.bitcast_convert_type(xw << 16, F32).astype(BF)
    hi = lax.bitcast_convert_type(xw & jnp.uint32(0xFFFF0000), F32).astype(BF)
    return lo, hi


def _ln_kernel(*refs, n_slots, gate_row, shift_row, emit_h, emit_packed):
    it = iter(refs)
    x_ref, y_ref = next(it), next(it)
    wt_ref = next(it) if n_slots > 1 else None
    ma_ref, mb_ref, g_ref, b_ref = next(it), next(it), next(it), next(it)
    xo_ref = next(it)
    ho_ref = next(it) if emit_h else None
    po_ref = next(it) if emit_packed else None
    if n_slots > 1:
        y = y_ref[:, 0, :] * wt_ref[:, 0:1]
        for s in range(1, n_slots):
            y = y + y_ref[:, s, :] * wt_ref[:, s:s + 1]
    else:
        y = y_ref[...]
    z = DN_ALPHA * x_ref[...] + (1.0 + ma_ref[gate_row:gate_row + 1, :]) * y
    mu = jnp.mean(z, axis=-1, keepdims=True)
    zc = z - mu
    var = jnp.mean(zc * zc, axis=-1, keepdims=True)
    xn = zc * lax.rsqrt(var + LN_EPS) * g_ref[...] + b_ref[...]
    xo_ref[...] = xn
    if emit_h:
        hb = (xn * (1.0 + mb_ref[shift_row + 1:shift_row + 2, :]) + mb_ref[shift_row:shift_row + 1, :]).astype(BF)
        ho_ref[...] = hb
        if emit_packed:
            po_ref[...] = _pack_halves(hb)


def _ln_mod(x, y, wts, mod, ln_g, ln_b, l, which, l_next, shift_row):
    b, t, d = x.shape
    n_slots = 1 if wts is None else y.shape[2]
    tm = min(t, 256 if n_slots == 1 else 128)
    emit_h = l_next is not None
    emit_packed = emit_h and which == 0
    gate_row = 2 if which == 0 else 5
    ln_i = which
    in_specs = [pl.BlockSpec((None, tm, d), lambda i, j: (i, j, 0))]
    args = [x, y]
    if n_slots > 1:
        in_specs += [pl.BlockSpec((None, tm, n_slots, d), lambda i, j: (i, j, 0, 0)),
                     pl.BlockSpec((None, tm, n_slots), lambda i, j: (i, j, 0))]
        args.append(wts)
    else:
        in_specs.append(pl.BlockSpec((None, tm, d), lambda i, j: (i, j, 0)))
    lb = l if l_next is None else l_next
    in_specs += [pl.BlockSpec((None, None, 6, d), lambda i, j: (l, i, 0, 0)),
                 pl.BlockSpec((None, None, 6, d), lambda i, j: (lb, i, 0, 0)),
                 pl.BlockSpec((None, None, 1, d), lambda i, j: (l, ln_i, 0, 0)),
                 pl.BlockSpec((None, None, 1, d), lambda i, j: (l, ln_i, 0, 0))]
    args += [mod, mod, ln_g.reshape(DEPTH, 2, 1, d), ln_b.reshape(DEPTH, 2, 1, d)]
    out_specs = [pl.BlockSpec((None, tm, d), lambda i, j: (i, j, 0))]
    out_shape = [jax.ShapeDtypeStruct((b, t, d), F32)]
    if emit_h:
        out_specs.append(pl.BlockSpec((None, tm, d), lambda i, j: (i, j, 0)))
        out_shape.append(jax.ShapeDtypeStruct((b, t, d), BF))
    if emit_packed:
        out_specs.append(pl.BlockSpec((None, tm, d // 2), lambda i, j: (i, j, 0)))
        out_shape.append(jax.ShapeDtypeStruct((b, t, d // 2), jnp.uint32))
    res = pl.pallas_call(
        functools.partial(_ln_kernel, n_slots=n_slots, gate_row=gate_row, shift_row=shift_row, emit_h=emit_h,
                          emit_packed=emit_packed),
        grid=(b, t // tm), in_specs=in_specs, out_specs=out_specs, out_shape=out_shape,
        compiler_params=_cparams(2), name="ln_mod",
    )(*args)
    return tuple(res) + (None,) * (3 - len(res))


def _gather_rows_kernel(idx_ref, src_ref, out_ref, sem, *, tr):
    def row_copy(r, src_row):
        return pltpu.make_async_copy(src_ref.at[pl.ds(src_row, 1)], out_ref.at[pl.ds(r, 1)], sem)

    def issue(r, c):
        row_copy(r, idx_ref[0, r]).start()
        return c
    lax.fori_loop(0, tr, issue, 0, unroll=8)

    def wait(r, c):
        row_copy(r, 0).wait()
        return c
    lax.fori_loop(0, tr, wait, 0, unroll=8)


def _gather_rows(src, idx):
    n_out = idx.shape[0]
    w = src.shape[1]
    tr = next(c for c in (256, 128, 64, 32, 16, 8) if n_out % c == 0)
    nt = n_out // tr
    return pl.pallas_call(
        functools.partial(_gather_rows_kernel, tr=tr), grid=(nt,),
        in_specs=[pl.BlockSpec((None, 1, tr), lambda t: (t, 0, 0), memory_space=pltpu.SMEM),
                  pl.BlockSpec(memory_space=pl.ANY)],
        out_specs=pl.BlockSpec((tr, w), lambda t: (t, 0)),
        out_shape=jax.ShapeDtypeStruct((n_out, w), src.dtype),
        scratch_shapes=[pltpu.SemaphoreType.DMA(())],
        compiler_params=_cparams(1), name="gather_rows",
    )(idx.reshape(nt, 1, tr), src)


def _topk_lanes(vals, lane, k, floor):
    picks = []
    lane_f = lane.astype(F32)
    for _ in range(k):
        mx = jnp.max(vals, axis=-1, keepdims=True)
        idx = jnp.min(jnp.where(vals == mx, lane_f, 4096.0), axis=-1, keepdims=True)
        picks.append((mx, idx))
        vals = jnp.where(lane_f == idx, floor, vals)
    return picks


def _router_kernel(x_ref, w_ref, b_ref, gate_ref, idx_ref):
    logits = _dot_hi(x_ref[...].astype(F32), w_ref[...]) + b_ref[...]
    lane = lax.broadcasted_iota(jnp.int32, logits.shape, 1)
    vals = jnp.where(lane < N_EXPERTS, logits, NEG)
    picks = _topk_lanes(vals, lane, TOP_K, -3e38)
    v0 = picks[0][0]
    es = [jnp.exp(v - v0) for v, _ in picks]
    z = es[0] + es[1] + es[2] + es[3]
    gates = jnp.zeros(logits.shape, F32)
    idxs = jnp.zeros(logits.shape, F32)
    for k in range(TOP_K):
        gates = jnp.where(lane == k, es[k] / z, gates)
        idxs = jnp.where(lane == k, picks[k][1], idxs)
    gate_ref[...] = gates
    idx_ref[...] = idxs.astype(jnp.int32)


def _router(h2, w_router, b_router, l):
    m = h2.shape[0]
    tm = min(m, 512)
    w = jnp.pad(w_router, ((0, 0), (0, 0), (0, LANE - N_EXPERTS)))
    bb = jnp.pad(b_router, ((0, 0), (0, LANE - N_EXPERTS))).reshape(DEPTH, 1, LANE)
    return pl.pallas_call(
        _router_kernel,
        grid=(m // tm,),
        in_specs=[pl.BlockSpec((tm, D_MODEL), lambda i: (i, 0)),
                  pl.BlockSpec((None, D_MODEL, LANE), lambda i: (l, 0, 0)),
                  pl.BlockSpec((None, 1, LANE), lambda i: (l, 0, 0))],
        out_specs=[pl.BlockSpec((tm, LANE), lambda i: (i, 0)), pl.BlockSpec((tm, LANE), lambda i: (i, 0))],
        out_shape=[jax.ShapeDtypeStruct((m, LANE), F32), jax.ShapeDtypeStruct((m, LANE), jnp.int32)],
        compiler_params=_cparams(1), name="router",
    )(h2, w, bb)


def _moe_up_kernel(te_ref, tv_ref, x_ref, wg_ref, wu_ref, bg_ref, bu_ref, o_ref, xlo_ref, xhi_ref):
    t = pl.program_id(0)
    half = D_MODEL // 2

    @pl.when(pl.program_id(1) == 0)
    def _():
        xlo_ref[...], xhi_ref[...] = _unpack_halves(x_ref[...])

    @pl.when(tv_ref[t] > 0)
    def _():
        xlo = xlo_ref[...]
        xhi = xhi_ref[...]
        g = _dot(xlo, wg_ref[:half, :].astype(BF)) + _dot(xhi, wg_ref[half:, :].astype(BF)) + bg_ref[...]
        u = _dot(xlo, wu_ref[:half, :].astype(BF)) + _dot(xhi, wu_ref[half:, :].astype(BF)) + bu_ref[...]
        g = jnp.minimum(g, SWIGLU_LIMIT)
        u = jnp.clip(u, -SWIGLU_LIMIT, SWIGLU_LIMIT)
        o_ref[...] = ((u + 1.0) * g * _sigmoid(SWIGLU_ALPHA * g)).astype(o_ref.dtype)

    @pl.when(tv_ref[t] == 0)
    def _():
        o_ref[...] = jnp.zeros_like(o_ref)


def _moe_dn_kernel(te_ref, tv_ref, a_ref, w_ref, b_ref, o_ref):
    t = pl.program_id(0)

    @pl.when(tv_ref[t] > 0)
    def _():
        o_ref[...] = _dot(a_ref[...], w_ref[...].astype(BF)) + b_ref[...]

    @pl.when(tv_ref[t] == 0)
    def _():
        o_ref[...] = jnp.zeros_like(o_ref)


def _moe_experts(x_rows, tile_e, tile_v, w_gu, b_gu, w_dn, b_dn, l, tm):
    r = x_rows.shape[0]
    nt = r // tm
    tn = 256
    nj = D_EXPERT // tn
    b_gu4 = b_gu.reshape(DEPTH, N_EXPERTS, 1, 2 * D_EXPERT)
    act = pl.pallas_call(
        _moe_up_kernel,
        grid_spec=pltpu.PrefetchScalarGridSpec(
            num_scalar_prefetch=2, grid=(nt, nj),
            in_specs=[pl.BlockSpec((tm, D_MODEL // 2), lambda t, j, te, tv: (t, 0)),
                      pl.BlockSpec((None, None, D_MODEL, tn), lambda t, j, te, tv: (l, te[t], 0, j * tv[t])),
                      pl.BlockSpec((None, None, D_MODEL, tn), lambda t, j, te, tv: (l, te[t], 0, nj + j * tv[t])),
                      pl.BlockSpec((None, None, 1, tn), lambda t, j, te, tv: (l, te[t], 0, j * tv[t])),
                      pl.BlockSpec((None, None, 1, tn), lambda t, j, te, tv: (l, te[t], 0, nj + j * tv[t]))],
            out_specs=pl.BlockSpec((tm, tn), lambda t, j, te, tv: (t, j)),
            scratch_shapes=[pltpu.VMEM((tm, D_MODEL // 2), BF), pltpu.VMEM((tm, D_MODEL // 2), BF)]),
        out_shape=jax.ShapeDtypeStruct((r, D_EXPERT), BF),
        compiler_params=_cparams(2), name="moe_up",
    )(tile_e, tile_v, x_rows, w_gu, w_gu, b_gu4, b_gu4)
    tn2 = 1024
    return pl.pallas_call(
        _moe_dn_kernel,
        grid_spec=pltpu.PrefetchScalarGridSpec(
            num_scalar_prefetch=2, grid=(nt, D_MODEL // tn2),
            in_specs=[pl.BlockSpec((tm, D_EXPERT), lambda t, j, te, tv: (t, 0)),
                      pl.BlockSpec((None, None, D_EXPERT, tn2), lambda t, j, te, tv: (l, te[t], 0, j * tv[t])),
                      pl.BlockSpec((None, None, 1, tn2), lambda t, j, te, tv: (l, te[t], 0, j * tv[t]))],
            out_specs=pl.BlockSpec((tm, tn2), lambda t, j, te, tv: (t, j))),
        out_shape=jax.ShapeDtypeStruct((r, D_MODEL), F32),
        compiler_params=_cparams(2), name="moe_dn",
    )(tile_e, tile_v, act, w_dn, b_dn.reshape(DEPTH, N_EXPERTS, 1, D_MODEL))


def _moe(h2, h2_packed, w_router, b_router, w_gu, b_gu, w_dn, b_dn, l):
    b, t, d = h2.shape
    n_tok = b * t
    xs = h2.reshape(n_tok, d)
    gate_l, idx_l = _router(xs, w_router, b_router, l)
    gates = gate_l[:, :TOP_K]
    e_flat = idx_l[:, :TOP_K].reshape(-1)
    n_asg = n_tok * TOP_K
    tm = 512 if n_asg >= 512 * N_EXPERTS else 16
    order = jnp.argsort(e_flat, stable=True)
    counts = jnp.sum(jax.nn.one_hot(e_flat, N_EXPERTS, dtype=jnp.int32), axis=0)
    padded = (counts + tm - 1) // tm * tm
    pad_end = jnp.cumsum(padded)
    pad_start = pad_end - padded
    start = jnp.cumsum(counts) - counts
    e_sorted = e_flat[order]
    dest_sorted = pad_start[e_sorted] + jnp.arange(n_asg, dtype=jnp.int32) - start[e_sorted]
    n_rows = (n_asg // tm + N_EXPERTS) * tm
    nt = n_rows // tm
    row_tok = jnp.zeros((n_rows,), jnp.int32).at[dest_sorted].set((order // TOP_K).astype(jnp.int32))
    pos = jnp.zeros((n_asg,), jnp.int32).at[order].set(dest_sorted.astype(jnp.int32))
    tile_start = jnp.arange(nt, dtype=jnp.int32) * tm
    tile_e = jnp.minimum(jnp.searchsorted(pad_end, tile_start, side="right"), N_EXPERTS - 1).astype(jnp.int32)
    tile_v = (tile_start < pad_end[-1]).astype(jnp.int32)
    x_rows = _gather_rows(h2_packed.reshape(n_tok, d // 2), row_tok)
    y_rows = _moe_experts(x_rows, tile_e, tile_v, w_gu, b_gu, w_dn, b_dn, l, tm)
    y_slots = _gather_rows(y_rows, pos).reshape(b, t, TOP_K, d)
    return y_slots, gates.reshape(b, t, TOP_K)


def _t5_bucket(dist):
    n = jnp.maximum(dist, 0)
    exact = N_BUCKETS // 2
    nf = jnp.maximum(n, 1).astype(F32)
    large = exact + (jnp.log(nf / exact) / math.log(MAX_DISTANCE / exact) * (N_BUCKETS - exact)).astype(jnp.int32)
    return jnp.where(n < exact, n, jnp.minimum(large, N_BUCKETS - 1))


def _bias_lookup(table, bucket):
    out = jnp.zeros((table.shape[1],) + bucket.shape, F32)
    for b in range(N_BUCKETS):
        out = jnp.where(bucket == b, table[b].reshape((-1,) + (1,) * bucket.ndim), out)
    return out


def _toeplitz_kernel(g_ref, o_ref):
    x = jnp.broadcast_to(g_ref[...], (TB, 2 * TB))
    o_ref[...] = pltpu.roll(x, TB + 1, 1, stride=1, stride_axis=0)[:, :TB]


def _t5_tiles(rel_bias, n_off):
    o = jnp.arange(n_off)[:, None]
    y = jnp.arange(2 * TB)[None, :]
    g = jnp.transpose(rel_bias[_t5_bucket(o * TB + (TB - 1) - y)], (2, 3, 0, 1))
    n_mh = 3 * N_HEADS
    return pl.pallas_call(
        _toeplitz_kernel, grid=(n_mh, n_off),
        in_specs=[pl.BlockSpec((None, None, 1, 2 * TB), lambda m, i: (m, i, 0, 0))],
        out_specs=pl.BlockSpec((None, None, TB, TB), lambda m, i: (m, i, 0, 0)),
        out_shape=jax.ShapeDtypeStruct((n_mh, n_off, TB, TB), F32),
        compiler_params=_cparams(2), name="t5_tiles",
    )(g.reshape(n_mh, n_off, 1, 2 * TB))


def _lam_of(dl_ref, lam_init):
    a = jnp.sum(dl_ref[0:1, :] * dl_ref[1:2, :], axis=-1, keepdims=True)
    b = jnp.sum(dl_ref[2:3, :] * dl_ref[3:4, :], axis=-1, keepdims=True)
    return jnp.exp(a) - jnp.exp(b) + lam_init


def _online_update(s, mask, v, m_ref, l_ref, acc_ref):
    if mask is not None:
        s = jnp.where(mask, s, NEG)
    m_prev = m_ref[...]
    m_new = jnp.maximum(m_prev, jnp.max(s, axis=-1, keepdims=True))
    alpha = jnp.exp(m_prev - m_new)
    p = jnp.exp(s - m_new)
    if mask is not None:
        p = jnp.where(mask, p, 0.0)
    l_ref[...] = alpha * l_ref[...] + jnp.sum(p, axis=-1, keepdims=True)
    acc_ref[...] = alpha * acc_ref[...] + _dot(p.astype(BF), v)
    m_ref[...] = m_new


def _normalized(l_ref, acc_ref):
    l = l_ref[...]
    return acc_ref[...] / jnp.where(l > 0, l, 1.0)


def _flash_kernel(*refs, nq, band, fox, sel, sel_shared, diff, gated, addend, lam_init):
    it = iter(refs)
    q_ref, k_ref, v_ref = next(it), next(it), next(it)
    if fox:
        fq_ref, fk_ref = next(it), next(it)
    else:
        tab_ref = next(it)
    if sel:
        sel_ref, e_ref = next(it), next(it)
    if gated:
        g_ref = next(it)
    if addend:
        add_ref = next(it)
    if diff:
        dl_ref, gn_ref = next(it), next(it)
    o_ref = next(it)
    n_maps = 2 if diff else 1
    m_ref, l_ref, acc_ref = next(it), next(it), next(it)

    i = pl.program_id(2)
    jj = pl.program_id(3)
    if band:
        j = i - 1 + jj
        active = j >= 0
        last = jj == 1
    else:
        j = jj
        active = jj <= i
        last = jj == nq - 1

    @pl.when(jj == 0)
    def _():
        m_ref[...] = jnp.full_like(m_ref, NEG)
        l_ref[...] = jnp.zeros_like(l_ref)
        acc_ref[...] = jnp.zeros_like(acc_ref)

    def tile(positional):
        k = k_ref[...].astype(BF)
        v = v_ref[...].astype(BF)
        pos_mask = None
        if positional:
            row = lax.broadcasted_iota(jnp.int32, (TB, TB), 0)
            col = lax.broadcasted_iota(jnp.int32, (TB, TB), 1)
            dpos = (i - j) * TB + row - col
            pos_mask = dpos >= 0
            if band:
                pos_mask = pos_mask & (dpos <= NSA_WINDOW)
        for g in range(GROUP):
            q = q_ref[:, g * HEAD_DIM:(g + 1) * HEAD_DIM]
            mask = pos_mask
            if sel:
                sm = _dot(sel_ref[0 if sel_shared else g].astype(BF), e_ref[...]) > 0.5
                mask = sm if mask is None else (mask & sm)
            bias = (fq_ref[g] - fk_ref[g]) if fox else tab_ref[g]
            for mi in range(n_maps):
                if diff:
                    lane = lax.broadcasted_iota(jnp.int32, q.shape, 1)
                    half = (lane < DIFF_DH) if mi == 0 else (lane >= DIFF_DH)
                    qs = jnp.where(half, q * DIFF_SCALE, 0.0).astype(BF)
                else:
                    qs = (q * SCALE).astype(BF)
                _online_update(_dot_nt(qs, k) + bias, mask, v, m_ref.at[mi, g], l_ref.at[mi, g], acc_ref.at[mi, g])

    if band:
        pl.when(active)(lambda: tile(True))
    else:
        pl.when(active & (j == i))(lambda: tile(True))
        pl.when(active & (j != i))(lambda: tile(False))

    @pl.when(last)
    def _():
        for g in range(GROUP):
            o = _normalized(l_ref.at[0, g], acc_ref.at[0, g])
            if diff:
                o = o - _lam_of(dl_ref, lam_init) * _normalized(l_ref.at[1, g], acc_ref.at[1, g])
                o = o * lax.rsqrt(jnp.mean(o * o, axis=-1, keepdims=True) + LN_EPS) * gn_ref[...] * (1.0 - lam_init)
            if gated:
                o = _sigmoid(g_ref[g]) * o
            if addend:
                o = o + add_ref[:, g * HEAD_DIM:(g + 1) * HEAD_DIM]
            o_ref[:, g * HEAD_DIM:(g + 1) * HEAD_DIM] = o.astype(o_ref.dtype)


def _flash_prefill(proj, qcol, kvcol, *, fq=None, fk=None, tab=None, tab_i=0, sel=None, sel_e=None, sel_per_kv=False,
                   band=False, gates=None, gate_i=0, addend=None, diff=None, out_dtype=BF):
    b, t, _ = proj.shape
    nq = t // TB
    nkk = 2 if band else nq
    if band:
        jmap = lambda i, jj: jnp.maximum(i - 1 + jj, 0)
    else:
        jmap = lambda i, jj: jnp.minimum(jj, i)
    gw = GROUP * LANE
    in_specs = [pl.BlockSpec((None, TB, gw), lambda bi, kv, i, jj: (bi, i, qcol // GROUP + kv)),
                pl.BlockSpec((None, TB, LANE), lambda bi, kv, i, jj: (bi, jmap(i, jj), kvcol + kv)),
                pl.BlockSpec((None, TB, LANE), lambda bi, kv, i, jj: (bi, jmap(i, jj), kvcol + N_KV + kv))]
    args = [proj, proj, proj]
    if fq is not None:
        in_specs += [pl.BlockSpec((None, GROUP, TB, 1), lambda bi, kv, i, jj: (bi, kv, i, 0)),
                     pl.BlockSpec((None, GROUP, 1, TB), lambda bi, kv, i, jj: (bi, kv, 0, jmap(i, jj)))]
        args += [fq, fk]
    else:
        in_specs.append(pl.BlockSpec((GROUP, None, TB, TB),
                                     lambda bi, kv, i, jj: (tab_i * N_KV + kv, i - jmap(i, jj), 0, 0)))
        args.append(tab)
    if sel is not None:
        if sel_per_kv:
            in_specs.append(pl.BlockSpec((None, 1, TB, LANE), lambda bi, kv, i, jj: (bi, kv, i, 0)))
        else:
            in_specs.append(pl.BlockSpec((None, GROUP, TB, LANE), lambda bi, kv, i, jj: (bi, kv, i, 0)))
        in_specs.append(pl.BlockSpec((None, LANE, TB), lambda bi, kv, i, jj: (jmap(i, jj), 0, 0)))
        args += [sel, sel_e]
    if gates is not None:
        in_specs.append(pl.BlockSpec((None, None, GROUP, TB, 1), lambda bi, kv, i, jj: (bi, gate_i, kv, i, 0)))
        args.append(gates)
    if addend is not None:
        in_specs.append(pl.BlockSpec((None, TB, gw), lambda bi, kv, i, jj: (bi, i, kv)))
        args.append(addend)
    lam_init = 0.0
    if diff is not None:
        dl, gn, lam_init = diff
        in_specs += [pl.BlockSpec((4, DIFF_DH), lambda bi, kv, i, jj: (0, 0)),
                     pl.BlockSpec((1, HEAD_DIM), lambda bi, kv, i, jj: (0, 0))]
        args += [dl, gn]
    n_maps = 2 if diff is not None else 1
    scratch = [pltpu.VMEM((n_maps, GROUP, TB, 1), F32), pltpu.VMEM((n_maps, GROUP, TB, 1), F32),
               pltpu.VMEM((n_maps, GROUP, TB, HEAD_DIM), F32)]
    kern = functools.partial(_flash_kernel, nq=nq, band=band, fox=fq is not None, sel=sel is not None,
                             sel_shared=sel_per_kv, diff=diff is not None, gated=gates is not None,
                             addend=addend is not None, lam_init=lam_init)
    return pl.pallas_call(
        kern, grid=(b, N_KV, nq, nkk), in_specs=in_specs,
        out_specs=pl.BlockSpec((None, TB, gw), lambda bi, kv, i, jj: (bi, i, kv)),
        out_shape=jax.ShapeDtypeStruct((b, t, MIX_W), out_dtype),
        scratch_shapes=scratch, compiler_params=_cparams(4), name="flash_prefill",
    )(*args)


def _logsig_kernel(x_ref, o_ref):
    x = x_ref[...]
    o_ref[...] = jnp.minimum(x, 0.0) - jnp.log(1.0 + jnp.exp(-jnp.abs(x)))


def _log_sigmoid(x):
    b, h, t = x.shape
    return pl.pallas_call(
        _logsig_kernel, grid=(b,),
        in_specs=[pl.BlockSpec((None, h, t), lambda i: (i, 0, 0))],
        out_specs=pl.BlockSpec((None, h, t), lambda i: (i, 0, 0)),
        out_shape=jax.ShapeDtypeStruct((b, h, t), F32), compiler_params=_cparams(1), name="log_sigmoid",
    )(x)


CS_CHUNK = 512


def _cumsum_kernel(x_ref, o_ref, carry_ref):
    @pl.when(pl.program_id(1) == 0)
    def _():
        carry_ref[...] = jnp.zeros_like(carry_ref)
    r = lax.broadcasted_iota(jnp.int32, (CS_CHUNK, CS_CHUNK), 0)
    c = lax.broadcasted_iota(jnp.int32, (CS_CHUNK, CS_CHUNK), 1)
    tri = (r <= c).astype(F32)
    y = _dot_hi(x_ref[...], tri) + carry_ref[...]
    o_ref[...] = y
    carry_ref[...] = y[:, CS_CHUNK - 1:CS_CHUNK]


def _cumsum(x):
    b, h, n = x.shape
    return pl.pallas_call(
        _cumsum_kernel, grid=(b, n // CS_CHUNK),
        in_specs=[pl.BlockSpec((None, h, CS_CHUNK), lambda i, j: (i, 0, j))],
        out_specs=pl.BlockSpec((None, h, CS_CHUNK), lambda i, j: (i, 0, j)),
        out_shape=jax.ShapeDtypeStruct((b, h, n), F32),
        scratch_shapes=[pltpu.VMEM((h, 1), F32)], compiler_params=_cparams(2), name="cumsum",
    )(x)


def _moba_gate_kernel(q_ref, k_ref, sel_ref, *, t):
    nblk = t // MOBA_BLOCK
    r = lax.broadcasted_iota(jnp.int32, (LANE, t), 0)
    c = lax.broadcasted_iota(jnp.int32, (LANE, t), 1)
    avg = jnp.where(c // MOBA_BLOCK == r, 1.0 / MOBA_BLOCK, 0.0)
    kmean = _dot_hi(avg, k_ref[...])
    gs = lax.dot_general(q_ref[...], kmean, (((1,), (1,)), ((), ())), precision=lax.Precision.HIGHEST,
                         preferred_element_type=F32)
    lane = lax.broadcasted_iota(jnp.int32, gs.shape, 1)
    own = lax.broadcasted_iota(jnp.int32, gs.shape, 0) // MOBA_BLOCK
    vals = jnp.where(lane < own, gs, NEG)
    sel = (lane == own).astype(F32)
    for mx, idx in _topk_lanes(vals, lane, min(MOBA_TOPK, nblk), -3e38):
        sel = jnp.where((lane.astype(F32) == idx) & (mx > NEG_TEST), 1.0, sel)
    sel_ref[...] = sel


def _moba_gate(proj):
    b, t, _ = proj.shape
    return pl.pallas_call(
        functools.partial(_moba_gate_kernel, t=t), grid=(b, N_HEADS),
        in_specs=[pl.BlockSpec((None, t, LANE), lambda bi, h: (bi, 0, QM + h)),
                  pl.BlockSpec((None, t, LANE), lambda bi, h: (bi, 0, KVM + h // GROUP))],
        out_specs=pl.BlockSpec((None, None, t, LANE), lambda bi, h: (bi, h, 0, 0)),
        out_shape=jax.ShapeDtypeStruct((b, N_HEADS, t, LANE), F32),
        compiler_params=_cparams(2), name="moba_gate",
    )(proj, proj)


def _compress_kernel(c_ref, w_ref, o_ref):
    o_ref[...] = _dot(c_ref[...].astype(BF), w_ref[...].astype(BF))


def _nsa_compress(chunks, w_cmp, l):
    b, _, _, n, sd = chunks.shape
    rt = min(n, 512)
    r_len = NSA_CMP_LEN // NSA_CMP_STRIDE
    w = w_cmp.reshape(DEPTH, 2, r_len, sd, HEAD_DIM)
    y = pl.pallas_call(
        _compress_kernel, grid=(b, 2, N_KV, r_len, n // rt),
        in_specs=[pl.BlockSpec((None, None, None, rt, sd), lambda bi, j, k, r, i: (bi, j, k, i, 0)),
                  pl.BlockSpec((None, None, None, sd, HEAD_DIM), lambda bi, j, k, r, i: (l, j, r, 0, 0))],
        out_specs=pl.BlockSpec((None, None, None, None, rt, HEAD_DIM), lambda bi, j, k, r, i: (bi, j, k, r, i, 0)),
        out_shape=jax.ShapeDtypeStruct((b, 2, N_KV, r_len, n, HEAD_DIM), F32),
        compiler_params=_cparams(5), name="nsa_compress",
    )(chunks, w)
    return _combine_halves(y)


def _combine_halves(y):
    return y[:, :, :, 0] + jnp.pad(y[:, :, :, 1, 1:], ((0, 0), (0, 0), (0, 0), (0, 1), (0, 0)))


def _compress_paged_kernel(*refs, pp):
    _pt_ref = refs[0]
    page_refs = refs[1:1 + pp]
    w_ref, o_ref = refs[1 + pp], refs[2 + pp]
    cpp = PAGE_SIZE // NSA_CMP_STRIDE
    chunk_rows = NSA_CMP_STRIDE * 2 * N_KV
    r_len = NSA_CMP_LEN // NSA_CMP_STRIDE
    for j in range(2):
        for kv in range(N_KV):
            acc = [jnp.zeros((pp * cpp, HEAD_DIM), F32) for _ in range(r_len)]
            for s in range(NSA_CMP_STRIDE):
                start = s * 2 * N_KV + j * N_KV + kv
                xs = jnp.concatenate([ref[pl.ds(start, cpp, stride=chunk_rows), :] for ref in page_refs], axis=0)
                xs = xs.astype(BF)
                for r in range(r_len):
                    acc[r] = acc[r] + _dot(xs, w_ref[j, r * NSA_CMP_STRIDE + s].astype(BF))
            for r in range(r_len):
                o_ref[j, kv, r] = acc[r]


def _nsa_compress_paged(pool, l, page_table, w_cmp, pp):
    b, n_pages = page_table.shape
    cpp = PAGE_SIZE // NSA_CMP_STRIDE
    r_len = NSA_CMP_LEN // NSA_CMP_STRIDE
    page_rows = PAGE_SIZE * 2 * N_KV
    pool2 = pool.reshape(pool.shape[0], pool.shape[1], page_rows, HEAD_DIM)
    page_specs = [pl.BlockSpec((None, None, page_rows, HEAD_DIM), (lambda bi, s, pt, i=i: (l, pt[bi, s * pp + i], 0, 0)))
                  for i in range(pp)]
    y = pl.pallas_call(
        functools.partial(_compress_paged_kernel, pp=pp),
        grid_spec=pltpu.PrefetchScalarGridSpec(
            num_scalar_prefetch=1, grid=(b, n_pages // pp),
            in_specs=page_specs + [pl.BlockSpec((None, 2, NSA_CMP_LEN, HEAD_DIM, HEAD_DIM), lambda bi, s, pt: (l, 0, 0, 0, 0))],
            out_specs=pl.BlockSpec((None, 2, N_KV, r_len, pp * cpp, HEAD_DIM), lambda bi, s, pt: (bi, 0, 0, 0, s, 0))),
        out_shape=jax.ShapeDtypeStruct((b, 2, N_KV, r_len, n_pages * cpp, HEAD_DIM), F32),
        compiler_params=_cparams(2), name="nsa_compress_paged",
    )(page_table, *([pool2] * pp), w_cmp)
    return _combine_halves(y)


def _nsa_cmp_kernel(q_ref, kc_ref, vc_ref, tab_ref, cover_ref, g_ref, o_ref, sel_ref, *, n_cmp, n_slc):
    i = pl.program_id(2)
    shape = (TB, LANE)
    qpos = i * TB + lax.broadcasted_iota(jnp.int32, shape, 0)
    lane = lax.broadcasted_iota(jnp.int32, shape, 1)
    maskc = (lane * NSA_CMP_STRIDE + (NSA_CMP_LEN - 1) <= qpos) & (lane < n_cmp)
    kc = kc_ref[...].astype(BF)
    vc = vc_ref[...].astype(BF)
    imp = jnp.zeros(shape, F32)
    for g in range(GROUP):
        q = (q_ref[:, g * HEAD_DIM:(g + 1) * HEAD_DIM] * SCALE).astype(BF)
        s = jnp.where(maskc, _dot_nt(q, kc) + tab_ref[g], NEG)
        mx = jnp.max(s, axis=-1, keepdims=True)
        p = jnp.where(maskc, jnp.exp(s - mx), 0.0)
        z = jnp.sum(p, axis=-1, keepdims=True)
        pc = p / jnp.where(z > 0, z, 1.0)
        o_ref[:, g * HEAD_DIM:(g + 1) * HEAD_DIM] = _sigmoid(g_ref[g]) * _dot(pc.astype(BF), vc)
        imp = imp + _dot_hi(pc, cover_ref[...])
    cur = qpos // NSA_SLC_BLOCK
    forced = (lane == 0) | ((lane <= cur) & (lane > cur - NSA_N_LOCAL))
    vals = jnp.where(forced, NSA_FORCE, jnp.where(lane <= cur, imp, -1.0))
    vals = jnp.where(lane < n_slc, vals, NEG)
    sel = jnp.zeros(shape, F32)
    for mx, idx in _topk_lanes(vals, lane, min(NSA_TOPN, n_slc), -3e38):
        sel = jnp.where((lane.astype(F32) == idx) & (mx >= 0.0), 1.0, sel)
    sel_ref[...] = sel


def _nsa_cmp_prefill(proj, kvc, tabc, cover, gates, n_cmp, n_slc):
    b, t, _ = proj.shape
    nq = t // TB
    return pl.pallas_call(
        functools.partial(_nsa_cmp_kernel, n_cmp=n_cmp, n_slc=n_slc), grid=(b, N_KV, nq),
        in_specs=[pl.BlockSpec((None, TB, GROUP * LANE), lambda bi, k, i: (bi, i, QN // GROUP + k)),
                  pl.BlockSpec((None, None, None, LANE, HEAD_DIM), lambda bi, k, i: (bi, 0, k, 0, 0)),
                  pl.BlockSpec((None, None, None, LANE, HEAD_DIM), lambda bi, k, i: (bi, 1, k, 0, 0)),
                  pl.BlockSpec((GROUP, TB, LANE), lambda bi, k, i: (k, i, 0)),
                  pl.BlockSpec((LANE, LANE), lambda bi, k, i: (0, 0)),
                  pl.BlockSpec((None, None, GROUP, TB, 1), lambda bi, k, i: (bi, 0, k, i, 0))],
        out_specs=[pl.BlockSpec((None, TB, GROUP * LANE), lambda bi, k, i: (bi, i, k)),
                   pl.BlockSpec((None, None, TB, LANE), lambda bi, k, i: (bi, k, i, 0))],
        out_shape=[jax.ShapeDtypeStruct((b, t, MIX_W), F32), jax.ShapeDtypeStruct((b, N_KV, t, LANE), F32)],
        compiler_params=_cparams(3), name="nsa_cmp_prefill",
    )(proj, kvc, kvc, tabc, cover, gates)


def _cover_matrix(n_cmp, n_slc):
    c_start = np.arange(n_cmp) * NSA_CMP_STRIDE
    s_ids = np.arange(n_slc)
    return ((c_start[:, None] < (s_ids[None, :] + 1) * NSA_SLC_BLOCK)
            & (c_start[:, None] + NSA_CMP_LEN > s_ids[None, :] * NSA_SLC_BLOCK)).astype(np.float32)


def _sel_expand(n_tiles, block):
    j = np.arange(n_tiles)[:, None, None]
    m = np.arange(LANE)[None, :, None]
    c = np.arange(TB)[None, None, :]
    return jnp.asarray(m == (j * TB + c) // block, BF)


def _mixers_prompt(l, h, P, tabs):
    b, t, d = h.shape
    n = b * t
    proj = _mm(h.reshape(n, d), P["w_in_r"], P["b_in_r"], l).reshape(b, t, PROJ_W)
    misc = proj[:, :, MISC * LANE:MISC * LANE + N_HEADS + 3 * N_HEADS]
    logf = _log_sigmoid(jnp.swapaxes(misc[..., :N_HEADS], 1, 2))
    gates = jnp.transpose(misc[..., N_HEADS:].reshape(b, t, 3, N_HEADS), (0, 2, 3, 1))[..., None]
    kvs = lambda c: proj[:, :, c * LANE:(c + 4) * LANE].reshape(b, t, 2, N_KV, HEAD_DIM)
    rows = dict(fox_kv=kvs(KVF), fox_logf=jnp.swapaxes(logf, 1, 2), diff_kv=kvs(KVD), moba_kv=kvs(KVM),
                nsa_cmp_kv=kvs(KVC), nsa_slc_kv=kvs(KVS))
    win = kvs(KVW)
    rows["nsa_win_kv"] = win[:, -min(NSA_WINDOW, t):]

    f_cum = _cumsum(logf)
    o_fox = _flash_prefill(proj, QF, KVF, fq=f_cum[..., None], fk=f_cum[:, :, None, :])

    lam_init = 0.8 - 0.6 * math.exp(-0.3 * l)
    o_diff = _flash_prefill(proj, QD, KVD, tab=tabs, tab_i=0,
                            diff=(P["diff_lambda"][l], P["diff_norm_g"][l].reshape(1, HEAD_DIM), lam_init))

    nk = t // TB
    o_moba = _flash_prefill(proj, QM, KVM, tab=tabs, tab_i=1, sel=_moba_gate(proj), sel_e=_sel_expand(nk, MOBA_BLOCK))

    n_chunk = t // NSA_CMP_STRIDE
    n_cmp = n_chunk - NSA_CMP_LEN // NSA_CMP_STRIDE + 1
    n_slc = -(-t // NSA_SLC_BLOCK)
    chunks = jnp.transpose(rows["nsa_cmp_kv"], (0, 2, 3, 1, 4)).reshape(b, 2, N_KV, n_chunk, NSA_CMP_STRIDE * HEAD_DIM)
    kvc = _nsa_compress(chunks, P["w_cmp"], l)
    cmp_end = jnp.arange(LANE) * NSA_CMP_STRIDE + (NSA_CMP_LEN - 1)
    tabc = _bias_lookup(P["rel_bias"][:, 2], _t5_bucket(jnp.arange(t)[:, None] - cmp_end[None, :]))
    cover = jnp.asarray(np.pad(_cover_matrix(n_cmp, n_slc), ((0, LANE - n_cmp), (0, LANE - n_slc))))
    o1, sel_n = _nsa_cmp_prefill(proj, kvc, tabc, cover, gates, n_cmp, n_slc)
    o2 = _flash_prefill(proj, QN, KVS, tab=tabs, tab_i=2, sel=sel_n, sel_e=_sel_expand(nk, NSA_SLC_BLOCK), sel_per_kv=True,
                        gates=gates, gate_i=1, addend=o1, out_dtype=F32)
    o_nsa = _flash_prefill(proj, QN, KVW, tab=tabs, tab_i=2, band=True, gates=gates, gate_i=2, addend=o2)
    o_all = jnp.stack([o_fox, o_diff, o_moba, o_nsa]).reshape(4, n, MIX_W)
    return o_all, rows


def _decode_kernel(*refs, pp, ns, diff, lam_init):
    it = iter(refs)
    _pt_ref = next(it)
    q_ref = next(it)
    page_refs = [next(it) for _ in range(pp)]
    new_ref, bias_ref = next(it), next(it)
    if diff:
        dl_ref, gn_ref = next(it), next(it)
    o_ref = next(it)
    m_ref, l_ref, acc_ref = next(it), next(it), next(it)
    n_maps = 2 if diff else 1
    s_id = pl.program_id(1)

    @pl.when(s_id == 0)
    def _():
        m_ref[...] = jnp.full_like(m_ref, NEG)
        l_ref[...] = jnp.zeros_like(l_ref)
        acc_ref[...] = jnp.zeros_like(acc_ref)

    def process(page_list):
        n_keys = len(page_list) * PAGE_SIZE
        for kv in range(N_KV):
            rows_of = lambda ref, j: ref[pl.ds(j * N_KV + kv, PAGE_SIZE, stride=2 * N_KV), :]
            k = jnp.concatenate([rows_of(ref, 0) for ref in page_list], axis=0).astype(BF)
            v = jnp.concatenate([rows_of(ref, 1) for ref in page_list], axis=0).astype(BF)
            bias = bias_ref[kv, :, :n_keys]
            mask = bias > NEG_TEST
            q = q_ref[kv]
            for mi in range(n_maps):
                if diff:
                    lane = lax.broadcasted_iota(jnp.int32, q.shape, 1)
                    half = (lane < DIFF_DH) if mi == 0 else (lane >= DIFF_DH)
                    qs = jnp.where(half, q * DIFF_SCALE, 0.0).astype(BF)
                else:
                    qs = (q * SCALE).astype(BF)
                _online_update(_dot_nt(qs, k) + bias, mask, v, m_ref.at[mi, kv], l_ref.at[mi, kv], acc_ref.at[mi, kv])

    @pl.when(s_id < ns)
    def _():
        process(page_refs)

    @pl.when(s_id == ns)
    def _():
        process([new_ref])
        for kv in range(N_KV):
            o = _normalized(l_ref.at[0, kv], acc_ref.at[0, kv])
            if diff:
                o = o - _lam_of(dl_ref, lam_init) * _normalized(l_ref.at[1, kv], acc_ref.at[1, kv])
                o = o * lax.rsqrt(jnp.mean(o * o, axis=-1, keepdims=True) + LN_EPS) * gn_ref[...] * (1.0 - lam_init)
            o_ref[kv] = o


def _decode_attn(q, pool, l, page_table, new_kv, bias, pp, diff=None):
    b, _, r, _ = q.shape
    n_pages = page_table.shape[1]
    ns = n_pages // pp
    bb = bias.shape[0]
    page_rows = PAGE_SIZE * 2 * N_KV
    pool2 = pool.reshape(pool.shape[0], pool.shape[1], page_rows, HEAD_DIM)
    blk = (None, None, page_rows, HEAD_DIM)
    page_specs = [pl.BlockSpec(blk, (lambda bi, s, pt, i=i: (l, pt[bi, jnp.minimum(s, ns - 1) * pp + i], 0, 0)))
                  for i in range(pp)]
    in_specs = ([pl.BlockSpec((None, N_KV, r, HEAD_DIM), lambda bi, s, pt: (bi, 0, 0, 0))] + page_specs +
                [pl.BlockSpec((None, page_rows, HEAD_DIM), lambda bi, s, pt: (bi, 0, 0)),
                 pl.BlockSpec((None, N_KV, r, pp * PAGE_SIZE), lambda bi, s, pt: (bi if bb > 1 else 0, 0, 0, s))])
    args = [q] + [pool2] * pp + [new_kv.reshape(b, page_rows, HEAD_DIM), bias]
    lam_init = 0.0
    if diff is not None:
        dl, gn, lam_init = diff
        in_specs += [pl.BlockSpec((4, DIFF_DH), lambda bi, s, pt: (0, 0)),
                     pl.BlockSpec((1, HEAD_DIM), lambda bi, s, pt: (0, 0))]
        args += [dl, gn]
    n_maps = 2 if diff is not None else 1
    return pl.pallas_call(
        functools.partial(_decode_kernel, pp=pp, ns=ns, diff=diff is not None, lam_init=lam_init),
        grid_spec=pltpu.PrefetchScalarGridSpec(
            num_scalar_prefetch=1, grid=(b, ns + 1), in_specs=in_specs,
            out_specs=pl.BlockSpec((None, N_KV, r, HEAD_DIM), lambda bi, s, pt: (bi, 0, 0, 0)),
            scratch_shapes=[pltpu.VMEM((n_maps, N_KV, r, 1), F32), pltpu.VMEM((n_maps, N_KV, r, 1), F32),
                            pltpu.VMEM((n_maps, N_KV, r, HEAD_DIM), F32)]),
        out_shape=jax.ShapeDtypeStruct((b, N_KV, r, HEAD_DIM), F32),
        compiler_params=_cparams(2), name="decode_attn",
    )(page_table, *args)


def _page_sum_kernel(*refs, pp):
    _pt_ref = refs[0]
    page_refs = refs[1:1 + pp]
    o_ref = refs[1 + pp]
    for i, ref in enumerate(page_refs):
        for kv in range(N_KV):
            o_ref[i, kv:kv + 1, :] = jnp.sum(ref[:, kv, :], axis=0, keepdims=True)


def _page_key_sums(pool, l, page_table, pp):
    b, n_pages = page_table.shape
    blk = (None, None, PAGE_SIZE, None, N_KV, HEAD_DIM)
    page_specs = [pl.BlockSpec(blk, (lambda bi, s, pt, i=i: (l, pt[bi, s * pp + i], 0, 0, 0, 0))) for i in range(pp)]
    return pl.pallas_call(
        functools.partial(_page_sum_kernel, pp=pp),
        grid_spec=pltpu.PrefetchScalarGridSpec(
            num_scalar_prefetch=1, grid=(b, n_pages // pp), in_specs=page_specs,
            out_specs=pl.BlockSpec((None, pp, N_KV, HEAD_DIM), lambda bi, s, pt: (bi, s, 0, 0))),
        out_shape=jax.ShapeDtypeStruct((b, n_pages, N_KV, HEAD_DIM), F32),
        compiler_params=_cparams(2), name="page_key_sums",
    )(page_table, *([pool] * pp))


def _cmp_decode_kernel(q_ref, kvc_ref, bias_ref, o_ref, p_ref):
    for kv in range(N_KV):
        q = (q_ref[kv] * SCALE).astype(BF)
        bias = bias_ref[kv]
        mask = bias > NEG_TEST
        s = jnp.where(mask, _dot_nt(q, kvc_ref[0, kv].astype(BF)) + bias, NEG)
        mx = jnp.max(s, axis=-1, keepdims=True)
        p = jnp.where(mask, jnp.exp(s - mx), 0.0)
        z = jnp.sum(p, axis=-1, keepdims=True)
        pc = p / jnp.where(z > 0, z, 1.0)
        p_ref[kv] = pc
        o_ref[kv] = _dot(pc.astype(BF), kvc_ref[1, kv].astype(BF))


def _cmp_decode(q, kvc, bias):
    b, _, r, _ = q.shape
    n = kvc.shape[3]
    return pl.pallas_call(
        _cmp_decode_kernel, grid=(b,),
        in_specs=[pl.BlockSpec((None, N_KV, r, HEAD_DIM), lambda bi: (bi, 0, 0, 0)),
                  pl.BlockSpec((None, 2, N_KV, n, HEAD_DIM), lambda bi: (bi, 0, 0, 0, 0)),
                  pl.BlockSpec((N_KV, r, n), lambda bi: (0, 0, 0))],
        out_specs=[pl.BlockSpec((None, N_KV, r, HEAD_DIM), lambda bi: (bi, 0, 0, 0)),
                   pl.BlockSpec((None, N_KV, r, n), lambda bi: (bi, 0, 0, 0))],
        out_shape=[jax.ShapeDtypeStruct((b, N_KV, r, HEAD_DIM), F32), jax.ShapeDtypeStruct((b, N_KV, r, n), F32)],
        compiler_params=_cparams(1), name="cmp_decode",
    )(q, kvc, bias)


def _rows_of(x, t):
    b = x.shape[0]
    return jnp.transpose(x.reshape(b, t, N_KV, GROUP, HEAD_DIM), (0, 2, 3, 1, 4)).reshape(b, N_KV, GROUP * t, HEAD_DIM)


def _unrows(o, t):
    b = o.shape[0]
    return jnp.transpose(o.reshape(b, N_KV, GROUP, t, HEAD_DIM), (0, 3, 1, 2, 4)).reshape(b, t, MIX_W)


def _head_rows(a, t):
    b = a.shape[0]
    return a.reshape(b, N_KV, GROUP * t, a.shape[-1])


DEC_PP = 16


def _sample_tables(rel_bias, t, n_pages, w_len, past_len):
    lkp = (n_pages // DEC_PP + 1) * DEC_PP * PAGE_SIZE
    qpos = past_len + jnp.arange(t)
    kpos = jnp.arange(lkp)
    kvalid = (kpos[None, :] <= qpos[:, None]) & (kpos[None, :] < past_len + t)
    t5 = _bias_lookup(rel_bias.reshape(N_BUCKETS, 3 * N_HEADS), _t5_bucket(qpos[:, None] - kpos[None, :]))
    t5 = jnp.where(kvalid[None, None], t5.reshape(3, N_HEADS, t, lkp), NEG)
    n_chunk = (past_len + t) // NSA_CMP_STRIDE
    n_cmp = n_chunk - NSA_CMP_LEN // NSA_CMP_STRIDE + 1
    cmp_end = jnp.arange(n_chunk) * NSA_CMP_STRIDE + (NSA_CMP_LEN - 1)
    cvalid = (cmp_end[None, :] <= qpos[:, None]) & (jnp.arange(n_chunk)[None, :] < n_cmp)
    bias_c = _bias_lookup(rel_bias[:, 2], _t5_bucket(qpos[:, None] - cmp_end[None, :]))
    bias_c = jnp.where(cvalid[None], bias_c, NEG).reshape(N_KV, GROUP * t, n_chunk)
    wp = w_len // PAGE_SIZE
    lkw = 2 * wp * PAGE_SIZE
    kidx = jnp.arange(lkw)
    dw = (w_len + jnp.arange(t))[:, None] - kidx[None, :]
    wvalid = (dw >= 0) & (dw <= NSA_WINDOW) & (kidx[None, :] < w_len + t)
    bias_w = _bias_lookup(rel_bias[:, 2], _t5_bucket(dw))
    bias_w = jnp.where(wvalid[None], bias_w, NEG).reshape(1, N_KV, GROUP * t, lkw)
    return dict(kvalid=kvalid, t5=t5, bias_c=bias_c, bias_w=bias_w)


def _mixers_sample(l, h, P, caches, page_table, past_len, tables):
    b, t, d = h.shape
    n = b * t
    n_pages = page_table.shape[1]
    pp = DEC_PP
    ns = n_pages // pp
    lkp = (ns + 1) * pp * PAGE_SIZE
    proj = _mm(h.reshape(n, d), P["w_in_r"], P["b_in_r"], l).reshape(b, t, PROJ_W)
    seg = lambda c, w: proj[:, :, c * LANE:(c + w) * LANE]
    misc = proj[:, :, MISC * LANE:MISC * LANE + 4 * N_HEADS]
    kvs = lambda c: seg(c, 4).reshape(b, t, 2, N_KV, HEAD_DIM)
    pad_new = lambda kv: jnp.pad(kv, ((0, 0), (0, PAGE_SIZE - t), (0, 0), (0, 0), (0, 0)))
    logf_new = _log_sigmoid(jnp.pad(jnp.swapaxes(misc[..., :N_HEADS], 1, 2), ((0, 0), (0, 0), (0, LANE - t))))[..., :t]
    rows = dict(fox_kv=kvs(KVF), fox_logf=jnp.swapaxes(logf_new, 1, 2), diff_kv=kvs(KVD), moba_kv=kvs(KVM),
                nsa_cmp_kv=kvs(KVC), nsa_slc_kv=kvs(KVS))
    win_new = kvs(KVW)
    win_past = caches["nsa_win_kv"][l]
    w_len = win_past.shape[1]
    rows["nsa_win_kv"] = jnp.concatenate([win_past, win_new], axis=1)[:, -min(NSA_WINDOW, w_len + t):]

    qpos = past_len + jnp.arange(t)
    kpos = jnp.arange(lkp)
    kvalid = tables["kvalid"]
    t5_bias = lambda ti: tables["t5"][ti]

    logf_past = caches["fox_logf"][l][page_table].reshape(b, past_len, N_HEADS)
    lf = jnp.concatenate([jnp.swapaxes(logf_past, 1, 2), logf_new], axis=2)
    lf = jnp.pad(lf, ((0, 0), (0, 0), (0, lkp - past_len - t)))
    f_cum = _cumsum(lf)
    f_q = f_cum[:, :, past_len:past_len + t]
    bias_fox = jnp.where(kvalid[None, None], f_q[..., None] - f_cum[:, :, None, :], NEG)
    o_fox = _decode_attn(_rows_of(seg(QF, 8), t), caches["fox_kv"], l, page_table, pad_new(rows["fox_kv"]),
                         _head_rows(bias_fox, t), pp)

    lam_init = 0.8 - 0.6 * math.exp(-0.3 * l)
    o_diff = _decode_attn(_rows_of(seg(QD, 8), t), caches["diff_kv"], l, page_table, pad_new(rows["diff_kv"]),
                          _head_rows(t5_bias(0)[None], t), pp,
                          diff=(P["diff_lambda"][l], P["diff_norm_g"][l].reshape(1, HEAD_DIM), lam_init))

    lk = past_len + t
    nblk = -(-lk // MOBA_BLOCK)
    ppb = MOBA_BLOCK // PAGE_SIZE
    psum = _page_key_sums(caches["moba_kv"], l, page_table, pp)
    kmean = psum.reshape(b, n_pages // ppb, ppb, N_KV, HEAD_DIM).sum(2) / MOBA_BLOCK
    q_m = seg(QM, 8).reshape(b, t, N_KV, GROUP, HEAD_DIM)
    gs = jnp.einsum("btkgd,bmkd->bkgtm", q_m, kmean, precision=lax.Precision.HIGHEST)
    own = qpos // MOBA_BLOCK
    blk_ids = jnp.arange(n_pages // ppb)
    gs = jnp.where(blk_ids[None, :] < own[:, None], gs, -jnp.inf)
    top_v, top_i = lax.top_k(gs, min(MOBA_TOPK, nblk))
    kblk = kpos // MOBA_BLOCK
    in_chosen = jnp.any((top_i[..., None] == kblk) & jnp.isfinite(top_v)[..., None], axis=-2)
    keep = in_chosen | (kblk[None, :] == own[:, None])
    bias_moba = jnp.where(keep, t5_bias(1).reshape(N_KV, GROUP, t, lkp)[None], NEG)
    o_moba = _decode_attn(_rows_of(seg(QM, 8), t), caches["moba_kv"], l, page_table, pad_new(rows["moba_kv"]),
                          bias_moba.reshape(b, N_KV, GROUP * t, lkp), pp)

    n_chunk = lk // NSA_CMP_STRIDE
    r_len = NSA_CMP_LEN // NSA_CMP_STRIDE
    n_cmp = n_chunk - r_len + 1
    n_slc = -(-lk // NSA_SLC_BLOCK)
    assert t < NSA_CMP_STRIDE and past_len % NSA_CMP_STRIDE == 0
    kvc = _nsa_compress_paged(caches["nsa_cmp_kv"], l, page_table, P["w_cmp"], pp)
    q_n = _rows_of(seg(QN, 8), t)
    o_cmp, pc = _cmp_decode(q_n, kvc, tables["bias_c"])
    cover = jnp.asarray(_cover_matrix(n_cmp, n_slc))
    imp = jnp.einsum("bkgtm,mj->bktj", pc.reshape(b, N_KV, GROUP, t, n_chunk)[..., :n_cmp], cover,
                     precision=lax.Precision.HIGHEST)
    cur = (qpos // NSA_SLC_BLOCK)[:, None]
    sl_ids = jnp.arange(n_slc)
    forced = (sl_ids[None, :] == 0) | ((sl_ids[None, :] <= cur) & (sl_ids[None, :] > cur - NSA_N_LOCAL))
    imp = jnp.where(forced, NSA_FORCE, jnp.where(sl_ids[None, :] <= cur, imp, -1.0))
    top_v, top_i = lax.top_k(imp, min(NSA_TOPN, n_slc))
    kslc = jnp.minimum(kpos // NSA_SLC_BLOCK, n_slc - 1)
    keep_s = jnp.any((top_i[..., None] == kslc) & (top_v >= 0)[..., None], axis=-2)
    bias_t5n = t5_bias(2).reshape(N_KV, GROUP, t, lkp)
    bias_slc = jnp.where(keep_s[:, :, None], bias_t5n[None], NEG).reshape(b, N_KV, GROUP * t, lkp)
    o_slc = _decode_attn(q_n, caches["nsa_slc_kv"], l, page_table, pad_new(rows["nsa_slc_kv"]), bias_slc, pp)

    wp = w_len // PAGE_SIZE
    pool_w = caches["nsa_win_kv"].reshape(DEPTH, b * wp, PAGE_SIZE, 2, N_KV, HEAD_DIM)
    pt_w = (jnp.arange(b, dtype=jnp.int32)[:, None] * wp + jnp.arange(wp, dtype=jnp.int32)[None, :])
    o_win = _decode_attn(q_n, pool_w, l, pt_w, pad_new(win_new), tables["bias_w"], wp)

    g = _sigmoid(misc[..., N_HEADS:].reshape(b, t, 3, N_HEADS))[..., None]
    hd = lambda o: _unrows(o, t).reshape(b, t, N_HEADS, HEAD_DIM)
    o_nsa = (g[:, :, 0] * hd(o_cmp) + g[:, :, 1] * hd(o_slc) + g[:, :, 2] * hd(o_win)).reshape(b, t, MIX_W)
    o_all = jnp.stack([_unrows(o_fox, t), _unrows(o_diff, t), _unrows(o_moba, t), o_nsa]).astype(BF).reshape(4, n, MIX_W)
    return o_all, rows


STATE_NAMES = ("fox_kv", "fox_logf", "diff_kv", "moba_kv", "nsa_cmp_kv", "nsa_slc_kv", "nsa_win_kv")


def _run_group(x, mod, P, mixers):
    b, t, d = x.shape
    n = b * t
    rows = {name: [] for name in STATE_NAMES}
    h = _modulate(x, mod, 0)
    for l in range(DEPTH):
        o_all, new_rows = mixers(l, h)
        merged = _gate_merge(h.reshape(n, d), o_all, P["w_gate"], P["b_gate"], P["w_branch"], l)
        y = _mm(merged, P["w_o"], P["zero_bias"], l).reshape(b, t, d)
        x, h2, h2_packed = _ln_mod(x, y, None, mod, P["ln_g"], P["ln_b"], l, 0, l, 3)
        y_slots, gates = _moe(h2, h2_packed, P["w_router"], P["b_router"], P["w_gu"], P["b_gu"], P["w_dn"], P["b_dn"], l)
        x, h, _ = _ln_mod(x, y_slots, gates, mod, P["ln_g"], P["ln_b"], l, 1, l + 1 if l + 1 < DEPTH else None, 0)
        for name in STATE_NAMES:
            rows[name].append(new_rows[name])
    return x, {name: jnp.stack(rows[name]) for name in STATE_NAMES}


def kernel(x_prompt, x_sample, cache_fox_kv, cache_fox_logf, cache_diff_kv, cache_moba_kv, cache_nsa_cmp_kv,
           cache_nsa_slc_kv, cache_nsa_win_kv, page_table, c_prompt, c_sample, w_ada, b_ada, ln_g, ln_b, w_in, b_in,
           diff_lambda, diff_norm_g, rel_bias, w_cmp, w_gate, b_gate, w_branch, w_o, w_router, b_router,
           w_gu, b_gu, w_dn, b_dn):
    bp = x_prompt.shape[0]
    bs = x_sample.shape[0]
    past_len = page_table.shape[1] * PAGE_SIZE

    o_f = MIX_W + KV_W
    o_g = D_IN - 3 * N_HEADS

    def reorder(w):
        pad = jnp.zeros(w.shape[:-1] + (PROJ_W - D_IN,), w.dtype)
        return jnp.concatenate([w[..., :o_f], w[..., o_f + N_HEADS:o_g], w[..., o_f:o_f + N_HEADS], w[..., o_g:], pad], -1)

    P = dict(w_in_r=reorder(w_in), b_in_r=reorder(b_in).reshape(DEPTH, 1, PROJ_W), diff_lambda=diff_lambda,
             diff_norm_g=diff_norm_g, rel_bias=rel_bias, w_cmp=w_cmp, w_gate=w_gate, b_gate=b_gate, w_branch=w_branch,
             w_o=w_o, zero_bias=jnp.zeros((DEPTH, 1, D_MODEL), F32), w_router=w_router, b_router=b_router,
             w_gu=w_gu, b_gu=b_gu, w_dn=w_dn, b_dn=b_dn, ln_g=ln_g, ln_b=ln_b)

    r_pad = -(-(bp + bs) // 8) * 8
    c_all = jnp.pad(jnp.concatenate([c_prompt, c_sample], 0), ((0, r_pad - bp - bs), (0, 0)))
    mod_all = _ada_mod(c_all, w_ada, b_ada).reshape(DEPTH, r_pad, 6, D_MODEL)
    mod_p = mod_all[:, :bp]
    mod_s = mod_all[:, bp:bp + bs]

    n_off = x_prompt.shape[1] // TB
    tabs = _t5_tiles(rel_bias, n_off)
    y_prompt, sp = _run_group(x_prompt, mod_p, P, lambda l, h: _mixers_prompt(l, h, P, tabs))

    caches = dict(fox_kv=cache_fox_kv, fox_logf=cache_fox_logf, diff_kv=cache_diff_kv, moba_kv=cache_moba_kv,
                  nsa_cmp_kv=cache_nsa_cmp_kv, nsa_slc_kv=cache_nsa_slc_kv, nsa_win_kv=cache_nsa_win_kv)
    tables = _sample_tables(rel_bias, x_sample.shape[1], page_table.shape[1], cache_nsa_win_kv.shape[2], past_len)
    y_sample, ss = _run_group(x_sample, mod_s, P,
                              lambda l, h: _mixers_sample(l, h, P, caches, page_table, past_len, tables))
    return (y_prompt, y_sample,
            sp["fox_kv"], sp["fox_logf"], sp["diff_kv"], sp["moba_kv"], sp["nsa_cmp_kv"], sp["nsa_slc_kv"], sp["nsa_win_kv"],
            ss["fox_kv"], ss["fox_logf"], ss["diff_kv"], ss["moba_kv"], ss["nsa_cmp_kv"], ss["nsa_slc_kv"], ss["nsa_win_kv"])
```

```python
import functools
import math

import jax
import jax.numpy as jnp
import numpy as np
from jax import lax
from jax.experimental import pallas as pl
from jax.experimental.pallas import tpu as pltpu

D_MODEL = 4096
DEPTH = 2
PAGE_SIZE = 128
HEAD_DIM = 128
N_HEADS = 8
N_KV = 2
GROUP = N_HEADS // N_KV
MIX_W = N_HEADS * HEAD_DIM
KV_W = 2 * N_KV * HEAD_DIM
DIFF_DH = HEAD_DIM // 2
SCALE = HEAD_DIM ** -0.5
DIFF_SCALE = DIFF_DH ** -0.5
MOBA_BLOCK = 256
MOBA_TOPK = 3
NSA_CMP_LEN = 32
NSA_CMP_STRIDE = 16
NSA_SLC_BLOCK = 64
NSA_TOPN = 16
NSA_N_LOCAL = 2
NSA_WINDOW = 512
NSA_FORCE = 1e9
N_BUCKETS = 32
MAX_DISTANCE = 4096
N_EXPERTS = 32
TOP_K = 4
D_EXPERT = D_MODEL // 4
SWIGLU_ALPHA = 1.702
SWIGLU_LIMIT = 7.0
DN_ALPHA = (2 * DEPTH) ** 0.25
LN_EPS = 1e-5
IN_SPLITS = (MIX_W, KV_W, N_HEADS, MIX_W, KV_W, MIX_W, KV_W, MIX_W, KV_W, KV_W, KV_W, 3 * N_HEADS)
D_IN = sum(IN_SPLITS)

LANE = 128
QF, KVF, QD, KVD, QM, KVM, QN, KVC, KVS, KVW, MISC = 0, 8, 12, 20, 24, 32, 36, 44, 48, 52, 56
PROJ_W = 60 * LANE
NEG = -1e30
NEG_TEST = -1e29
TB = 512
VMEM_LIMIT = 56 * 1024 * 1024
BF = jnp.bfloat16
F32 = jnp.float32


def _cparams(n_axes):
    return pltpu.CompilerParams(dimension_semantics=("arbitrary",) * n_axes, vmem_limit_bytes=VMEM_LIMIT)


def _dot(a, b):
    return jnp.dot(a, b, preferred_element_type=F32)


def _dot_nt(a, b):
    return lax.dot_general(a, b, (((1,), (1,)), ((), ())), preferred_element_type=F32)


def _dot_hi(a, b):
    return jnp.dot(a, b, precision=lax.Precision.HIGHEST, preferred_element_type=F32)


def _sigmoid(x):
    return 1.0 / (1.0 + jnp.exp(-x))


def _cast_rows(src_ref, dst_ref, rows, chunk=256):
    def body(r, c):
        sl = pl.ds(pl.multiple_of(r * chunk, chunk), chunk)
        dst_ref[sl, :] = src_ref[sl, :].astype(BF)
        return c
    lax.fori_loop(0, rows // chunk, body, 0)


def _ada_kernel(c_ref, w_ref, b_ref, o_ref):
    c = c_ref[...]
    a = (c * _sigmoid(c)).astype(BF)
    o_ref[...] = _dot(a, w_ref[...].astype(BF)) + b_ref[...]


def _ada_mod(c_all, w_ada, b_ada):
    r = c_all.shape[0]
    tn = 512
    n6 = 6 * D_MODEL
    return pl.pallas_call(
        _ada_kernel,
        grid=(DEPTH, n6 // tn),
        in_specs=[pl.BlockSpec((r, D_MODEL), lambda l, n: (0, 0)),
                  pl.BlockSpec((None, D_MODEL, tn), lambda l, n: (l, 0, n)),
                  pl.BlockSpec((None, 1, tn), lambda l, n: (l, 0, n))],
        out_specs=pl.BlockSpec((None, r, tn), lambda l, n: (l, 0, n)),
        out_shape=jax.ShapeDtypeStruct((DEPTH, r, n6), F32),
        compiler_params=_cparams(2), name="ada_mod",
    )(c_all, w_ada, b_ada.reshape(DEPTH, 1, n6))


def _modulate_kernel(x_ref, m_ref, o_ref):
    o_ref[...] = (x_ref[...] * (1.0 + m_ref[1:2, :]) + m_ref[0:1, :]).astype(BF)


def _modulate(x, mod, l):
    b, t, d = x.shape
    tm = min(t, 512)
    return pl.pallas_call(
        _modulate_kernel,
        grid=(b, t // tm),
        in_specs=[pl.BlockSpec((None, tm, d), lambda i, j: (i, j, 0)),
                  pl.BlockSpec((None, None, 6, d), lambda i, j: (l, i, 0, 0))],
        out_specs=pl.BlockSpec((None, tm, d), lambda i, j: (i, j, 0)),
        out_shape=jax.ShapeDtypeStruct((b, t, d), BF),
        compiler_params=_cparams(2), name="modulate",
    )(x, mod)


def _mm_kernel(x_ref, w_ref, b_ref, o_ref, wbf_ref):
    @pl.when(pl.program_id(1) == 0)
    def _():
        _cast_rows(w_ref, wbf_ref, w_ref.shape[0])
    o_ref[...] = (_dot(x_ref[...], wbf_ref[...]) + b_ref[...]).astype(o_ref.dtype)


def _mm(x, w, bias, l, out_dtype=F32, tn=512):
    m, k = x.shape
    n = w.shape[-1]
    tm = min(m, 512)
    return pl.pallas_call(
        _mm_kernel,
        grid=(n // tn, m // tm),
        in_specs=[pl.BlockSpec((tm, k), lambda j, i: (i, 0)),
                  pl.BlockSpec((None, k, tn), lambda j, i: (l, 0, j)),
                  pl.BlockSpec((None, 1, tn), lambda j, i: (l, 0, j))],
        out_specs=pl.BlockSpec((tm, tn), lambda j, i: (i, j)),
        out_shape=jax.ShapeDtypeStruct((m, n), out_dtype),
        scratch_shapes=[pltpu.VMEM((k, tn), BF)],
        compiler_params=_cparams(2), name="mm",
    )(x, w, bias)


def _gate_merge_kernel(h_ref, o_ref, wg_ref, bg_ref, wb_ref, out_ref, acc_ref, wgbf_ref, wbbf_ref):
    br = pl.program_id(2)

    @pl.when(br == 0)
    def _():
        acc_ref[...] = jnp.zeros_like(acc_ref)

    _cast_rows(wg_ref, wgbf_ref, wg_ref.shape[0])
    _cast_rows(wb_ref, wbbf_ref, wb_ref.shape[0])
    g = _dot(h_ref[...], wgbf_ref[...]) + bg_ref[...]
    u = _dot(o_ref[...], wbbf_ref[...])
    acc_ref[...] += _sigmoid(g) * u

    @pl.when(br == 3)
    def _():
        out_ref[...] = acc_ref[...].astype(out_ref.dtype)


def _gate_merge(h, o_all, w_gate, b_gate, w_branch, l):
    m = h.shape[0]
    tm = min(m, 1024)
    tn = 256
    return pl.pallas_call(
        _gate_merge_kernel,
        grid=(D_MODEL // tn, m // tm, 4),
        in_specs=[pl.BlockSpec((tm, D_MODEL), lambda j, i, b: (i, 0)),
                  pl.BlockSpec((None, tm, MIX_W), lambda j, i, b: (b, i, 0)),
                  pl.BlockSpec((None, None, D_MODEL, tn), lambda j, i, b: (l, b, 0, j)),
                  pl.BlockSpec((None, None, 1, tn), lambda j, i, b: (l, b, 0, j)),
                  pl.BlockSpec((None, None, MIX_W, tn), lambda j, i, b: (l, b, 0, j))],
        out_specs=pl.BlockSpec((tm, tn), lambda j, i, b: (i, j)),
        out_shape=jax.ShapeDtypeStruct((m, D_MODEL), BF),
        scratch_shapes=[pltpu.VMEM((tm, tn), F32), pltpu.VMEM((D_MODEL, tn), BF), pltpu.VMEM((MIX_W, tn), BF)],
        compiler_params=_cparams(3), name="gate_merge",
    )(h, o_all, w_gate, b_gate.reshape(DEPTH, 4, 1, D_MODEL), w_branch)


def _pack_halves(hb):
    u = lax.bitcast_convert_type(hb.astype(F32), jnp.uint32)
    w = hb.shape[-1] // 2
    return (u[:, :w] >> 16) | (u[:, w:] & jnp.uint32(0xFFFF0000))


def _unpack_halves(xw):
    lo = lax.bitcast_convert_type(xw << 16, F32).astype(BF)
    hi = lax.bitcast_convert_type(xw & jnp.uint32(0xFFFF0000), F32).astype(BF)
    return lo, hi


def _ln_kernel(*refs, n_slots, gate_row, shift_row, emit_h, emit_packed):
    it = iter(refs)
    x_ref, y_ref = next(it), next(it)
    wt_ref = next(it) if n_slots > 1 else None
    ma_ref, mb_ref, g_ref, b_ref = next(it), next(it), next(it), next(it)
    xo_ref = next(it)
    ho_ref = next(it) if emit_h else None
    po_ref = next(it) if emit_packed else None
    if n_slots > 1:
        y = y_ref[0] * wt_ref[:, 0:1]
        for s in range(1, n_slots):
            y = y + y_ref[s] * wt_ref[:, s:s + 1]
    else:
        y = y_ref[...]
    z = DN_ALPHA * x_ref[...] + (1.0 + ma_ref[gate_row:gate_row + 1, :]) * y
    mu = jnp.mean(z, axis=-1, keepdims=True)
    zc = z - mu
    var = jnp.mean(zc * zc, axis=-1, keepdims=True)
    xn = zc * lax.rsqrt(var + LN_EPS) * g_ref[...] + b_ref[...]
    xo_ref[...] = xn
    if emit_h:
        hb = (xn * (1.0 + mb_ref[shift_row + 1:shift_row + 2, :]) + mb_ref[shift_row:shift_row + 1, :]).astype(BF)
        ho_ref[...] = hb
        if emit_packed:
            po_ref[...] = _pack_halves(hb)


def _ln_mod(x, y, wts, mod, ln_g, ln_b, l, which, l_next, shift_row):
    b, t, d = x.shape
    n_slots = 1 if wts is None else wts.shape[2]
    tm = min(t, 256 if n_slots == 1 else 128)
    emit_h = l_next is not None
    emit_packed = emit_h and which == 0
    gate_row = 2 if which == 0 else 5
    ln_i = which
    in_specs = [pl.BlockSpec((None, tm, d), lambda i, j: (i, j, 0))]
    args = [x, y]
    if n_slots > 1:
        in_specs += [pl.BlockSpec((n_slots, None, tm, d), lambda i, j: (0, i, j, 0)),
                     pl.BlockSpec((None, tm, n_slots), lambda i, j: (i, j, 0))]
        args.append(wts)
    else:
        in_specs.append(pl.BlockSpec((None, tm, d), lambda i, j: (i, j, 0)))
    lb = l if l_next is None else l_next
    in_specs += [pl.BlockSpec((None, None, 6, d), lambda i, j: (l, i, 0, 0)),
                 pl.BlockSpec((None, None, 6, d), lambda i, j: (lb, i, 0, 0)),
                 pl.BlockSpec((None, None, 1, d), lambda i, j: (l, ln_i, 0, 0)),
                 pl.BlockSpec((None, None, 1, d), lambda i, j: (l, ln_i, 0, 0))]
    args += [mod, mod, ln_g.reshape(DEPTH, 2, 1, d), ln_b.reshape(DEPTH, 2, 1, d)]
    out_specs = [pl.BlockSpec((None, tm, d), lambda i, j: (i, j, 0))]
    out_shape = [jax.ShapeDtypeStruct((b, t, d), F32)]
    if emit_h:
        out_specs.append(pl.BlockSpec((None, tm, d), lambda i, j: (i, j, 0)))
        out_shape.append(jax.ShapeDtypeStruct((b, t, d), BF))
    if emit_packed:
        out_specs.append(pl.BlockSpec((None, tm, d // 2), lambda i, j: (i, j, 0)))
        out_shape.append(jax.ShapeDtypeStruct((b, t, d // 2), jnp.uint32))
    res = pl.pallas_call(
        functools.partial(_ln_kernel, n_slots=n_slots, gate_row=gate_row, shift_row=shift_row, emit_h=emit_h,
                          emit_packed=emit_packed),
        grid=(b, t // tm), in_specs=in_specs, out_specs=out_specs, out_shape=out_shape,
        compiler_params=_cparams(2), name="ln_mod",
    )(*args)
    return tuple(res) + (None,) * (3 - len(res))


def _gather_rows_kernel(idx_ref, live_ref, src_ref, out_ref, *scratch, tr, unpack):
    if unpack:
        dst_ref, sem = scratch
    else:
        dst_ref, (sem,) = out_ref, scratch

    def row_copy(r, src_row):
        return pltpu.make_async_copy(src_ref.at[pl.ds(src_row, 1)], dst_ref.at[pl.ds(r, 1)], sem)

    @pl.when(live_ref[0, 0] > 0)
    def _():
        def issue(r, c):
            row_copy(r, idx_ref[0, r]).start()
            return c
        lax.fori_loop(0, tr, issue, 0, unroll=8)

        def wait(r, c):
            row_copy(r, 0).wait()
            return c
        lax.fori_loop(0, tr, wait, 0, unroll=8)
        if unpack:
            w = dst_ref.shape[1]
            out_ref[:, :w], out_ref[:, w:] = _unpack_halves(dst_ref[...])

    @pl.when(live_ref[0, 0] == 0)
    def _():
        out_ref[...] = jnp.zeros_like(out_ref)


def _gather_rows(src, idx, live=None, unpack=False):
    n_out = idx.shape[0]
    w = src.shape[1]
    tr = next(c for c in (256, 128, 64, 32, 16, 8) if n_out % c == 0)
    nt = n_out // tr
    if live is None:
        live = jnp.ones((nt,), jnp.int32)
    out_w, out_dtype = (2 * w, BF) if unpack else (w, src.dtype)
    scratch = ([pltpu.VMEM((tr, w), src.dtype)] if unpack else []) + [pltpu.SemaphoreType.DMA(())]
    return pl.pallas_call(
        functools.partial(_gather_rows_kernel, tr=tr, unpack=unpack), grid=(nt,),
        in_specs=[pl.BlockSpec((None, 1, tr), lambda t: (t, 0, 0), memory_space=pltpu.SMEM),
                  pl.BlockSpec((None, 1, 1), lambda t: (t, 0, 0), memory_space=pltpu.SMEM),
                  pl.BlockSpec(memory_space=pl.ANY)],
        out_specs=pl.BlockSpec((tr, out_w), lambda t: (t, 0)),
        out_shape=jax.ShapeDtypeStruct((n_out, out_w), out_dtype),
        scratch_shapes=scratch,
        compiler_params=_cparams(1), name="gather_rows",
    )(idx.reshape(nt, 1, tr), live.reshape(nt, 1, 1), src)


def _topk_lanes(vals, lane, k, floor):
    picks = []
    lane_f = lane.astype(F32)
    for _ in range(k):
        mx = jnp.max(vals, axis=-1, keepdims=True)
        idx = jnp.min(jnp.where(vals == mx, lane_f, 4096.0), axis=-1, keepdims=True)
        picks.append((mx, idx))
        vals = jnp.where(lane_f == idx, floor, vals)
    return picks


def _router_kernel(x_ref, w_ref, b_ref, gate_ref, idx_ref):
    logits = _dot(x_ref[...], w_ref[...].astype(BF)) + b_ref[...]
    lane = lax.broadcasted_iota(jnp.int32, logits.shape, 1)
    vals = jnp.where(lane < N_EXPERTS, logits, NEG)
    picks = _topk_lanes(vals, lane, TOP_K, -3e38)
    v0 = picks[0][0]
    es = [jnp.exp(v - v0) for v, _ in picks]
    z = es[0] + es[1] + es[2] + es[3]
    gates = jnp.zeros(logits.shape, F32)
    idxs = jnp.zeros(logits.shape, F32)
    for k in range(TOP_K):
        gates = jnp.where(lane == k, es[k] / z, gates)
        idxs = jnp.where(lane == k, picks[k][1], idxs)
    gate_ref[...] = gates
    idx_ref[...] = idxs.astype(jnp.int32)


def _router(h2, w_router, b_router, l):
    m = h2.shape[0]
    tm = min(m, 512)
    w = jnp.pad(w_router, ((0, 0), (0, 0), (0, LANE - N_EXPERTS)))
    bb = jnp.pad(b_router, ((0, 0), (0, LANE - N_EXPERTS))).reshape(DEPTH, 1, LANE)
    return pl.pallas_call(
        _router_kernel,
        grid=(m // tm,),
        in_specs=[pl.BlockSpec((tm, D_MODEL), lambda i: (i, 0)),
                  pl.BlockSpec((None, D_MODEL, LANE), lambda i: (l, 0, 0)),
                  pl.BlockSpec((None, 1, LANE), lambda i: (l, 0, 0))],
        out_specs=[pl.BlockSpec((tm, LANE), lambda i: (i, 0)), pl.BlockSpec((tm, LANE), lambda i: (i, 0))],
        out_shape=[jax.ShapeDtypeStruct((m, LANE), F32), jax.ShapeDtypeStruct((m, LANE), jnp.int32)],
        compiler_params=_cparams(1), name="router",
    )(h2, w, bb)


def _new_expert(te_ref, t):
    return (t == 0) | (te_ref[t] != te_ref[jnp.maximum(t - 1, 0)])


def _moe_up_kernel(te_ref, tv_ref, x_ref, wg_ref, wu_ref, bg_ref, bu_ref, o_ref, wgbf_ref, wubf_ref):
    t = pl.program_id(1)
    live = tv_ref[t] > 0

    @pl.when(live & _new_expert(te_ref, t))
    def _():
        _cast_rows(wg_ref, wgbf_ref, wg_ref.shape[0])
        _cast_rows(wu_ref, wubf_ref, wu_ref.shape[0])

    @pl.when(live)
    def _():
        x = x_ref[...]
        g = _dot(x, wgbf_ref[...]) + bg_ref[...]
        u = _dot(x, wubf_ref[...]) + bu_ref[...]
        g = jnp.minimum(g, SWIGLU_LIMIT)
        u = jnp.clip(u, -SWIGLU_LIMIT, SWIGLU_LIMIT)
        o_ref[...] = ((u + 1.0) * g * _sigmoid(SWIGLU_ALPHA * g)).astype(o_ref.dtype)

    @pl.when(jnp.logical_not(live))
    def _():
        o_ref[...] = jnp.zeros_like(o_ref)


def _moe_dn_kernel(te_ref, tv_ref, a_ref, w_ref, b_ref, o_ref, wbf_ref):
    t = pl.program_id(1)
    live = tv_ref[t] > 0

    @pl.when(live & _new_expert(te_ref, t))
    def _():
        _cast_rows(w_ref, wbf_ref, w_ref.shape[0])

    @pl.when(live)
    def _():
        o_ref[...] = _dot(a_ref[...], wbf_ref[...]) + b_ref[...]

    @pl.when(jnp.logical_not(live))
    def _():
        o_ref[...] = jnp.zeros_like(o_ref)


def _moe_experts(x_rows, tile_e, tile_v, w_gu, b_gu, w_dn, b_dn, l, tm):
    r = x_rows.shape[0]
    nt = r // tm
    tn = 256
    nj = D_EXPERT // tn
    b_gu4 = b_gu.reshape(DEPTH, N_EXPERTS, 1, 2 * D_EXPERT)
    act = pl.pallas_call(
        _moe_up_kernel,
        grid_spec=pltpu.PrefetchScalarGridSpec(
            num_scalar_prefetch=2, grid=(nj, nt),
            in_specs=[pl.BlockSpec((tm, D_MODEL), lambda j, t, te, tv: (t * tv[t], 0)),
                      pl.BlockSpec((None, None, D_MODEL, tn), lambda j, t, te, tv: (l, te[t], 0, j)),
                      pl.BlockSpec((None, None, D_MODEL, tn), lambda j, t, te, tv: (l, te[t], 0, nj + j)),
                      pl.BlockSpec((None, None, 1, tn), lambda j, t, te, tv: (l, te[t], 0, j)),
                      pl.BlockSpec((None, None, 1, tn), lambda j, t, te, tv: (l, te[t], 0, nj + j))],
            out_specs=pl.BlockSpec((tm, tn), lambda j, t, te, tv: (t, j)),
            scratch_shapes=[pltpu.VMEM((D_MODEL, tn), BF), pltpu.VMEM((D_MODEL, tn), BF)]),
        out_shape=jax.ShapeDtypeStruct((r, D_EXPERT), BF),
        compiler_params=_cparams(2), name="moe_up",
    )(tile_e, tile_v, x_rows, w_gu, w_gu, b_gu4, b_gu4)
    tn2 = 1024
    return pl.pallas_call(
        _moe_dn_kernel,
        grid_spec=pltpu.PrefetchScalarGridSpec(
            num_scalar_prefetch=2, grid=(D_MODEL // tn2, nt),
            in_specs=[pl.BlockSpec((tm, D_EXPERT), lambda j, t, te, tv: (t * tv[t], 0)),
                      pl.BlockSpec((None, None, D_EXPERT, tn2), lambda j, t, te, tv: (l, te[t], 0, j)),
                      pl.BlockSpec((None, None, 1, tn2), lambda j, t, te, tv: (l, te[t], 0, j))],
            out_specs=pl.BlockSpec((tm, tn2), lambda j, t, te, tv: (t, j)),
            scratch_shapes=[pltpu.VMEM((D_EXPERT, tn2), BF)]),
        out_shape=jax.ShapeDtypeStruct((r, D_MODEL), F32),
        compiler_params=_cparams(2), name="moe_dn",
    )(tile_e, tile_v, act, w_dn, b_dn.reshape(DEPTH, N_EXPERTS, 1, D_MODEL))


def _moe(h2, h2_packed, w_router, b_router, w_gu, b_gu, w_dn, b_dn, l):
    b, t, d = h2.shape
    n_tok = b * t
    xs = h2.reshape(n_tok, d)
    gate_l, idx_l = _router(xs, w_router, b_router, l)
    gates = gate_l[:, :TOP_K]
    e_flat = idx_l[:, :TOP_K].reshape(-1)
    n_asg = n_tok * TOP_K
    tm = 512 if n_asg >= 512 * N_EXPERTS else 16
    order = jnp.argsort(e_flat, stable=True)
    counts = jnp.sum(jax.nn.one_hot(e_flat, N_EXPERTS, dtype=jnp.int32), axis=0)
    padded = (counts + tm - 1) // tm * tm
    pad_end = jnp.cumsum(padded)
    pad_start = pad_end - padded
    start = jnp.cumsum(counts) - counts
    e_sorted = e_flat[order]
    dest_sorted = pad_start[e_sorted] + jnp.arange(n_asg, dtype=jnp.int32) - start[e_sorted]
    n_rows = (n_asg // tm + N_EXPERTS) * tm
    nt = n_rows // tm
    row_tok = jnp.zeros((n_rows,), jnp.int32).at[dest_sorted].set((order // TOP_K).astype(jnp.int32))
    pos = jnp.zeros((n_asg,), jnp.int32).at[order].set(dest_sorted.astype(jnp.int32))
    tile_start = jnp.arange(nt, dtype=jnp.int32) * tm
    tile_e = jnp.minimum(jnp.searchsorted(pad_end, tile_start, side="right"), N_EXPERTS - 1).astype(jnp.int32)
    tile_v = (tile_start < pad_end[-1]).astype(jnp.int32)
    tr = next(c for c in (256, 128, 64, 32, 16, 8) if n_rows % c == 0)
    live = (jnp.arange(n_rows // tr, dtype=jnp.int32) * tr < pad_end[-1]).astype(jnp.int32)
    x_rows = _gather_rows(h2_packed.reshape(n_tok, d // 2), row_tok, live=live, unpack=True)
    y_rows = _moe_experts(x_rows, tile_e, tile_v, w_gu, b_gu, w_dn, b_dn, l, tm)
    y_slots = _gather_rows(y_rows, pos.reshape(n_tok, TOP_K).T.reshape(-1)).reshape(TOP_K, b, t, d)
    return y_slots, gates.reshape(b, t, TOP_K)


def _t5_bucket(dist):
    n = jnp.maximum(dist, 0)
    exact = N_BUCKETS // 2
    nf = jnp.maximum(n, 1).astype(F32)
    large = exact + (jnp.log(nf / exact) / math.log(MAX_DISTANCE / exact) * (N_BUCKETS - exact)).astype(jnp.int32)
    return jnp.where(n < exact, n, jnp.minimum(large, N_BUCKETS - 1))


def _bias_lookup(table, bucket):
    out = jnp.zeros((table.shape[1],) + bucket.shape, F32)
    for b in range(N_BUCKETS):
        out = jnp.where(bucket == b, table[b].reshape((-1,) + (1,) * bucket.ndim), out)
    return out


def _toeplitz_kernel(g_ref, o_ref):
    x = jnp.broadcast_to(g_ref[...], (TB, 2 * TB))
    o_ref[...] = pltpu.roll(x, TB + 1, 1, stride=1, stride_axis=0)[:, :TB]


def _t5_tiles(rel_bias, n_off):
    o = jnp.arange(n_off)[:, None]
    y = jnp.arange(2 * TB)[None, :]
    g = jnp.transpose(rel_bias[_t5_bucket(o * TB + (TB - 1) - y)], (2, 3, 0, 1))
    n_mh = 3 * N_HEADS
    return pl.pallas_call(
        _toeplitz_kernel, grid=(n_mh, n_off),
        in_specs=[pl.BlockSpec((None, None, 1, 2 * TB), lambda m, i: (m, i, 0, 0))],
        out_specs=pl.BlockSpec((None, None, TB, TB), lambda m, i: (m, i, 0, 0)),
        out_shape=jax.ShapeDtypeStruct((n_mh, n_off, TB, TB), F32),
        compiler_params=_cparams(2), name="t5_tiles",
    )(g.reshape(n_mh, n_off, 1, 2 * TB))


def _lam_of(dl_ref, lam_init):
    a = jnp.sum(dl_ref[0:1, :] * dl_ref[1:2, :], axis=-1, keepdims=True)
    b = jnp.sum(dl_ref[2:3, :] * dl_ref[3:4, :], axis=-1, keepdims=True)
    return jnp.exp(a) - jnp.exp(b) + lam_init


def _online_update(s, mask, v, m_ref, l_ref, acc_ref):
    if mask is not None:
        s = jnp.where(mask, s, NEG)
    m_prev = m_ref[...]
    m_new = jnp.maximum(m_prev, jnp.max(s, axis=-1, keepdims=True))
    alpha = jnp.exp(m_prev - m_new)
    p = jnp.exp(s - m_new)
    if mask is not None:
        p = jnp.where(mask, p, 0.0)
    l_ref[...] = alpha * l_ref[...] + jnp.sum(p, axis=-1, keepdims=True)
    acc_ref[...] = alpha * acc_ref[...] + _dot(p.astype(BF), v)
    m_ref[...] = m_new


def _normalized(l_ref, acc_ref):
    l = l_ref[...]
    return acc_ref[...] / jnp.where(l > 0, l, 1.0)


def _flash_kernel(*refs, nq, band, fox, sel, sel_shared, diff, gated, addend, lam_init):
    it = iter(refs)
    q_ref, k_ref, v_ref = next(it), next(it), next(it)
    if fox:
        fq_ref, fk_ref = next(it), next(it)
    else:
        tab_ref = next(it)
    if sel:
        sel_ref, e_ref = next(it), next(it)
    if gated:
        g_ref = next(it)
    if addend:
        add_ref = next(it)
    if diff:
        dl_ref, gn_ref = next(it), next(it)
    o_ref = next(it)
    n_maps = 2 if diff else 1
    m_ref, l_ref, acc_ref = next(it), next(it), next(it)

    i = pl.program_id(2)
    jj = pl.program_id(3)
    if band:
        j = i - 1 + jj
        active = j >= 0
        last = jj == 1
    else:
        j = jj
        active = jj <= i
        last = jj == nq - 1

    @pl.when(jj == 0)
    def _():
        m_ref[...] = jnp.full_like(m_ref, NEG)
        l_ref[...] = jnp.zeros_like(l_ref)
        acc_ref[...] = jnp.zeros_like(acc_ref)

    def tile(positional):
        k = k_ref[...].astype(BF)
        v = v_ref[...].astype(BF)
        pos_mask = None
        if positional:
            row = lax.broadcasted_iota(jnp.int32, (TB, TB), 0)
            col = lax.broadcasted_iota(jnp.int32, (TB, TB), 1)
            dpos = (i - j) * TB + row - col
            pos_mask = dpos >= 0
            if band:
                pos_mask = pos_mask & (dpos <= NSA_WINDOW)
        for g in range(GROUP):
            q = q_ref[:, g * HEAD_DIM:(g + 1) * HEAD_DIM]
            mask = pos_mask
            if sel:
                sm = _dot(sel_ref[0 if sel_shared else g].astype(BF), e_ref[...]) > 0.5
                mask = sm if mask is None else (mask & sm)
            bias = (fq_ref[g] - fk_ref[g]) if fox else tab_ref[g]
            for mi in range(n_maps):
                if diff:
                    lane = lax.broadcasted_iota(jnp.int32, q.shape, 1)
                    half = (lane < DIFF_DH) if mi == 0 else (lane >= DIFF_DH)
                    s = _dot_nt(jnp.where(half, q * DIFF_SCALE, 0.0).astype(BF), k) + bias
                else:
                    s = _dot_nt(q.astype(BF), k) * SCALE + bias
                _online_update(s, mask, v, m_ref.at[mi, g], l_ref.at[mi, g], acc_ref.at[mi, g])

    if band:
        pl.when(active)(lambda: tile(True))
    else:
        pl.when(active & (j == i))(lambda: tile(True))
        pl.when(active & (j != i))(lambda: tile(False))

    @pl.when(last)
    def _():
        for g in range(GROUP):
            o = _normalized(l_ref.at[0, g], acc_ref.at[0, g])
            if diff:
                o = o - _lam_of(dl_ref, lam_init) * _normalized(l_ref.at[1, g], acc_ref.at[1, g])
                o = o * lax.rsqrt(jnp.mean(o * o, axis=-1, keepdims=True) + LN_EPS) * gn_ref[...] * (1.0 - lam_init)
            if gated:
                o = _sigmoid(g_ref[g]) * o
            if addend:
                o = o + add_ref[:, g * HEAD_DIM:(g + 1) * HEAD_DIM]
            o_ref[:, g * HEAD_DIM:(g + 1) * HEAD_DIM] = o.astype(o_ref.dtype)


def _flash_prefill(proj, qcol, kvcol, *, fq=None, fk=None, tab=None, tab_i=0, sel=None, sel_e=None, sel_per_kv=False,
                   band=False, gates=None, gate_i=0, addend=None, diff=None, out_dtype=BF):
    b, t, _ = proj.shape
    nq = t // TB
    nkk = 2 if band else nq
    if band:
        jmap = lambda i, jj: jnp.maximum(i - 1 + jj, 0)
    else:
        jmap = lambda i, jj: jnp.minimum(jj, i)
    gw = GROUP * LANE
    in_specs = [pl.BlockSpec((None, TB, gw), lambda bi, kv, i, jj: (bi, i, qcol // GROUP + kv)),
                pl.BlockSpec((None, TB, LANE), lambda bi, kv, i, jj: (bi, jmap(i, jj), kvcol + kv)),
                pl.BlockSpec((None, TB, LANE), lambda bi, kv, i, jj: (bi, jmap(i, jj), kvcol + N_KV + kv))]
    args = [proj, proj, proj]
    if fq is not None:
        in_specs += [pl.BlockSpec((None, GROUP, TB, 1), lambda bi, kv, i, jj: (bi, kv, i, 0)),
                     pl.BlockSpec((None, GROUP, 1, TB), lambda bi, kv, i, jj: (bi, kv, 0, jmap(i, jj)))]
        args += [fq, fk]
    else:
        in_specs.append(pl.BlockSpec((GROUP, None, TB, TB),
                                     lambda bi, kv, i, jj: (tab_i * N_KV + kv, i - jmap(i, jj), 0, 0)))
        args.append(tab)
    if sel is not None:
        if sel_per_kv:
            in_specs.append(pl.BlockSpec((None, 1, TB, LANE), lambda bi, kv, i, jj: (bi, kv, i, 0)))
        else:
            in_specs.append(pl.BlockSpec((None, GROUP, TB, LANE), lambda bi, kv, i, jj: (bi, kv, i, 0)))
        in_specs.append(pl.BlockSpec((None, LANE, TB), lambda bi, kv, i, jj: (jmap(i, jj), 0, 0)))
        args += [sel, sel_e]
    if gates is not None:
        in_specs.append(pl.BlockSpec((None, None, GROUP, TB, 1), lambda bi, kv, i, jj: (bi, gate_i, kv, i, 0)))
        args.append(gates)
    if addend is not None:
        in_specs.append(pl.BlockSpec((None, TB, gw), lambda bi, kv, i, jj: (bi, i, kv)))
        args.append(addend)
    lam_init = 0.0
    if diff is not None:
        dl, gn, lam_init = diff
        in_specs += [pl.BlockSpec((4, DIFF_DH), lambda bi, kv, i, jj: (0, 0)),
                     pl.BlockSpec((1, HEAD_DIM), lambda bi, kv, i, jj: (0, 0))]
        args += [dl, gn]
    n_maps = 2 if diff is not None else 1
    scratch = [pltpu.VMEM((n_maps, GROUP, TB, 1), F32), pltpu.VMEM((n_maps, GROUP, TB, 1), F32),
               pltpu.VMEM((n_maps, GROUP, TB, HEAD_DIM), F32)]
    kern = functools.partial(_flash_kernel, nq=nq, band=band, fox=fq is not None, sel=sel is not None,
                             sel_shared=sel_per_kv, diff=diff is not None, gated=gates is not None,
                             addend=addend is not None, lam_init=lam_init)
    return pl.pallas_call(
        kern, grid=(b, N_KV, nq, nkk), in_specs=in_specs,
        out_specs=pl.BlockSpec((None, TB, gw), lambda bi, kv, i, jj: (bi, i, kv)),
        out_shape=jax.ShapeDtypeStruct((b, t, MIX_W), out_dtype),
        scratch_shapes=scratch, compiler_params=_cparams(4), name="flash_prefill",
    )(*args)


def _logsig_kernel(x_ref, o_ref):
    x = x_ref[...]
    o_ref[...] = jnp.minimum(x, 0.0) - jnp.log(1.0 + jnp.exp(-jnp.abs(x)))


def _log_sigmoid(x):
    b, h, t = x.shape
    return pl.pallas_call(
        _logsig_kernel, grid=(b,),
        in_specs=[pl.BlockSpec((None, h, t), lambda i: (i, 0, 0))],
        out_specs=pl.BlockSpec((None, h, t), lambda i: (i, 0, 0)),
        out_shape=jax.ShapeDtypeStruct((b, h, t), F32), compiler_params=_cparams(1), name="log_sigmoid",
    )(x)


CS_CHUNK = 512


def _cumsum_kernel(x_ref, o_ref, carry_ref):
    @pl.when(pl.program_id(1) == 0)
    def _():
        carry_ref[...] = jnp.zeros_like(carry_ref)
    r = lax.broadcasted_iota(jnp.int32, (CS_CHUNK, CS_CHUNK), 0)
    c = lax.broadcasted_iota(jnp.int32, (CS_CHUNK, CS_CHUNK), 1)
    tri = (r <= c).astype(F32)
    y = _dot_hi(x_ref[...], tri) + carry_ref[...]
    o_ref[...] = y
    carry_ref[...] = y[:, CS_CHUNK - 1:CS_CHUNK]


def _cumsum(x):
    b, h, n = x.shape
    return pl.pallas_call(
        _cumsum_kernel, grid=(b, n // CS_CHUNK),
        in_specs=[pl.BlockSpec((None, h, CS_CHUNK), lambda i, j: (i, 0, j))],
        out_specs=pl.BlockSpec((None, h, CS_CHUNK), lambda i, j: (i, 0, j)),
        out_shape=jax.ShapeDtypeStruct((b, h, n), F32),
        scratch_shapes=[pltpu.VMEM((h, 1), F32)], compiler_params=_cparams(2), name="cumsum",
    )(x)


def _moba_gate_kernel(q_ref, k_ref, sel_ref, *, t):
    nblk = t // MOBA_BLOCK
    r = lax.broadcasted_iota(jnp.int32, (LANE, t), 0)
    c = lax.broadcasted_iota(jnp.int32, (LANE, t), 1)
    avg = jnp.where(c // MOBA_BLOCK == r, 1.0 / MOBA_BLOCK, 0.0)
    kmean = _dot_hi(avg, k_ref[...])
    gs = _dot_nt(q_ref[...].astype(BF), kmean.astype(BF))
    lane = lax.broadcasted_iota(jnp.int32, gs.shape, 1)
    own = lax.broadcasted_iota(jnp.int32, gs.shape, 0) // MOBA_BLOCK
    vals = jnp.where(lane < own, gs, NEG)
    sel = (lane == own).astype(F32)
    for mx, idx in _topk_lanes(vals, lane, min(MOBA_TOPK, nblk), -3e38):
        sel = jnp.where((lane.astype(F32) == idx) & (mx > NEG_TEST), 1.0, sel)
    sel_ref[...] = sel


def _moba_gate(proj):
    b, t, _ = proj.shape
    return pl.pallas_call(
        functools.partial(_moba_gate_kernel, t=t), grid=(b, N_HEADS),
        in_specs=[pl.BlockSpec((None, t, LANE), lambda bi, h: (bi, 0, QM + h)),
                  pl.BlockSpec((None, t, LANE), lambda bi, h: (bi, 0, KVM + h // GROUP))],
        out_specs=pl.BlockSpec((None, None, t, LANE), lambda bi, h: (bi, h, 0, 0)),
        out_shape=jax.ShapeDtypeStruct((b, N_HEADS, t, LANE), F32),
        compiler_params=_cparams(2), name="moba_gate",
    )(proj, proj)


def _compress_kernel(c_ref, w_ref, o_ref):
    o_ref[...] = _dot(c_ref[...].astype(BF), w_ref[...].astype(BF))


def _nsa_compress(chunks, w_cmp, l):
    b, _, _, n, sd = chunks.shape
    rt = min(n, 512)
    r_len = NSA_CMP_LEN // NSA_CMP_STRIDE
    w = w_cmp.reshape(DEPTH, 2, r_len, sd, HEAD_DIM)
    y = pl.pallas_call(
        _compress_kernel, grid=(b, 2, N_KV, r_len, n // rt),
        in_specs=[pl.BlockSpec((None, None, None, rt, sd), lambda bi, j, k, r, i: (bi, j, k, i, 0)),
                  pl.BlockSpec((None, None, None, sd, HEAD_DIM), lambda bi, j, k, r, i: (l, j, r, 0, 0))],
        out_specs=pl.BlockSpec((None, None, None, None, rt, HEAD_DIM), lambda bi, j, k, r, i: (bi, j, k, r, i, 0)),
        out_shape=jax.ShapeDtypeStruct((b, 2, N_KV, r_len, n, HEAD_DIM), F32),
        compiler_params=_cparams(5), name="nsa_compress",
    )(chunks, w)
    return _combine_halves(y)


def _combine_halves(y):
    return y[:, :, :, 0] + jnp.pad(y[:, :, :, 1, 1:], ((0, 0), (0, 0), (0, 0), (0, 1), (0, 0)))


def _compress_paged_kernel(*refs, pp):
    _pt_ref = refs[0]
    page_refs = refs[1:1 + pp]
    w_ref, o_ref = refs[1 + pp], refs[2 + pp]
    cpp = PAGE_SIZE // NSA_CMP_STRIDE
    chunk_rows = NSA_CMP_STRIDE * 2 * N_KV
    r_len = NSA_CMP_LEN // NSA_CMP_STRIDE
    for j in range(2):
        for kv in range(N_KV):
            acc = [jnp.zeros((pp * cpp, HEAD_DIM), F32) for _ in range(r_len)]
            for s in range(NSA_CMP_STRIDE):
                start = s * 2 * N_KV + j * N_KV + kv
                xs = jnp.concatenate([ref[pl.ds(start, cpp, stride=chunk_rows), :] for ref in page_refs], axis=0)
                xs = xs.astype(BF)
                for r in range(r_len):
                    acc[r] = acc[r] + _dot(xs, w_ref[j, r * NSA_CMP_STRIDE + s].astype(BF))
            for r in range(r_len):
                o_ref[j, kv, r] = acc[r]


def _nsa_compress_paged(pool, l, page_table, w_cmp, pp):
    b, n_pages = page_table.shape
    cpp = PAGE_SIZE // NSA_CMP_STRIDE
    r_len = NSA_CMP_LEN // NSA_CMP_STRIDE
    page_rows = PAGE_SIZE * 2 * N_KV
    pool2 = pool.reshape(pool.shape[0], pool.shape[1], page_rows, HEAD_DIM)
    page_specs = [pl.BlockSpec((None, None, page_rows, HEAD_DIM), (lambda bi, s, pt, i=i: (l, pt[bi, s * pp + i], 0, 0)))
                  for i in range(pp)]
    y = pl.pallas_call(
        functools.partial(_compress_paged_kernel, pp=pp),
        grid_spec=pltpu.PrefetchScalarGridSpec(
            num_scalar_prefetch=1, grid=(b, n_pages // pp),
            in_specs=page_specs + [pl.BlockSpec((None, 2, NSA_CMP_LEN, HEAD_DIM, HEAD_DIM), lambda bi, s, pt: (l, 0, 0, 0, 0))],
            out_specs=pl.BlockSpec((None, 2, N_KV, r_len, pp * cpp, HEAD_DIM), lambda bi, s, pt: (bi, 0, 0, 0, s, 0))),
        out_shape=jax.ShapeDtypeStruct((b, 2, N_KV, r_len, n_pages * cpp, HEAD_DIM), F32),
        compiler_params=_cparams(2), name="nsa_compress_paged",
    )(page_table, *([pool2] * pp), w_cmp)
    return _combine_halves(y)


def _nsa_cmp_kernel(q_ref, kc_ref, vc_ref, tab_ref, cover_ref, g_ref, o_ref, sel_ref, *, n_cmp, n_slc):
    i = pl.program_id(2)
    shape = (TB, LANE)
    qpos = i * TB + lax.broadcasted_iota(jnp.int32, shape, 0)
    lane = lax.broadcasted_iota(jnp.int32, shape, 1)
    maskc = (lane * NSA_CMP_STRIDE + (NSA_CMP_LEN - 1) <= qpos) & (lane < n_cmp)
    kc = kc_ref[...].astype(BF)
    vc = vc_ref[...].astype(BF)
    imp = jnp.zeros(shape, F32)
    for g in range(GROUP):
        q = q_ref[:, g * HEAD_DIM:(g + 1) * HEAD_DIM].astype(BF)
        s = jnp.where(maskc, _dot_nt(q, kc) * SCALE + tab_ref[g], NEG)
        mx = jnp.max(s, axis=-1, keepdims=True)
        p = jnp.where(maskc, jnp.exp(s - mx), 0.0)
        z = jnp.sum(p, axis=-1, keepdims=True)
        pc = (p / jnp.where(z > 0, z, 1.0)).astype(BF)
        o_ref[:, g * HEAD_DIM:(g + 1) * HEAD_DIM] = _sigmoid(g_ref[g]) * _dot(pc, vc)
        imp = imp + _dot(pc, cover_ref[...].astype(BF))
    cur = qpos // NSA_SLC_BLOCK
    forced = (lane == 0) | ((lane <= cur) & (lane > cur - NSA_N_LOCAL))
    vals = jnp.where(forced, NSA_FORCE, jnp.where(lane <= cur, imp, -1.0))
    vals = jnp.where(lane < n_slc, vals, NEG)
    sel = jnp.zeros(shape, F32)
    for mx, idx in _topk_lanes(vals, lane, min(NSA_TOPN, n_slc), -3e38):
        sel = jnp.where((lane.astype(F32) == idx) & (mx >= 0.0), 1.0, sel)
    sel_ref[...] = sel


def _nsa_cmp_prefill(proj, kvc, tabc, cover, gates, n_cmp, n_slc):
    b, t, _ = proj.shape
    nq = t // TB
    return pl.pallas_call(
        functools.partial(_nsa_cmp_kernel, n_cmp=n_cmp, n_slc=n_slc), grid=(b, N_KV, nq),
        in_specs=[pl.BlockSpec((None, TB, GROUP * LANE), lambda bi, k, i: (bi, i, QN // GROUP + k)),
                  pl.BlockSpec((None, None, None, LANE, HEAD_DIM), lambda bi, k, i: (bi, 0, k, 0, 0)),
                  pl.BlockSpec((None, None, None, LANE, HEAD_DIM), lambda bi, k, i: (bi, 1, k, 0, 0)),
                  pl.BlockSpec((GROUP, TB, LANE), lambda bi, k, i: (k, i, 0)),
                  pl.BlockSpec((LANE, LANE), lambda bi, k, i: (0, 0)),
                  pl.BlockSpec((None, None, GROUP, TB, 1), lambda bi, k, i: (bi, 0, k, i, 0))],
        out_specs=[pl.BlockSpec((None, TB, GROUP * LANE), lambda bi, k, i: (bi, i, k)),
                   pl.BlockSpec((None, None, TB, LANE), lambda bi, k, i: (bi, k, i, 0))],
        out_shape=[jax.ShapeDtypeStruct((b, t, MIX_W), F32), jax.ShapeDtypeStruct((b, N_KV, t, LANE), F32)],
        compiler_params=_cparams(3), name="nsa_cmp_prefill",
    )(proj, kvc, kvc, tabc, cover, gates)


def _cover_matrix(n_cmp, n_slc):
    c_start = np.arange(n_cmp) * NSA_CMP_STRIDE
    s_ids = np.arange(n_slc)
    return ((c_start[:, None] < (s_ids[None, :] + 1) * NSA_SLC_BLOCK)
            & (c_start[:, None] + NSA_CMP_LEN > s_ids[None, :] * NSA_SLC_BLOCK)).astype(np.float32)


def _sel_expand(n_tiles, block):
    j = np.arange(n_tiles)[:, None, None]
    m = np.arange(LANE)[None, :, None]
    c = np.arange(TB)[None, None, :]
    return jnp.asarray(m == (j * TB + c) // block, BF)


def _mixers_prompt(l, h, P, tabs):
    b, t, d = h.shape
    n = b * t
    proj = _mm(h.reshape(n, d), P["w_in_r"], P["b_in_r"], l).reshape(b, t, PROJ_W)
    misc = proj[:, :, MISC * LANE:MISC * LANE + N_HEADS + 3 * N_HEADS]
    logf = _log_sigmoid(jnp.swapaxes(misc[..., :N_HEADS], 1, 2))
    gates = jnp.transpose(misc[..., N_HEADS:].reshape(b, t, 3, N_HEADS), (0, 2, 3, 1))[..., None]
    kvs = lambda c: proj[:, :, c * LANE:(c + 4) * LANE].reshape(b, t, 2, N_KV, HEAD_DIM)
    rows = dict(fox_kv=kvs(KVF), fox_logf=jnp.swapaxes(logf, 1, 2), diff_kv=kvs(KVD), moba_kv=kvs(KVM),
                nsa_cmp_kv=kvs(KVC), nsa_slc_kv=kvs(KVS))
    win = kvs(KVW)
    rows["nsa_win_kv"] = win[:, -min(NSA_WINDOW, t):]

    f_cum = _cumsum(logf)
    o_fox = _flash_prefill(proj, QF, KVF, fq=f_cum[..., None], fk=f_cum[:, :, None, :])

    lam_init = 0.8 - 0.6 * math.exp(-0.3 * l)
    o_diff = _flash_prefill(proj, QD, KVD, tab=tabs, tab_i=0,
                            diff=(P["diff_lambda"][l], P["diff_norm_g"][l].reshape(1, HEAD_DIM), lam_init))

    nk = t // TB
    o_moba = _flash_prefill(proj, QM, KVM, tab=tabs, tab_i=1, sel=_moba_gate(proj), sel_e=_sel_expand(nk, MOBA_BLOCK))

    n_chunk = t // NSA_CMP_STRIDE
    n_cmp = n_chunk - NSA_CMP_LEN // NSA_CMP_STRIDE + 1
    n_slc = -(-t // NSA_SLC_BLOCK)
    chunks = jnp.transpose(rows["nsa_cmp_kv"], (0, 2, 3, 1, 4)).reshape(b, 2, N_KV, n_chunk, NSA_CMP_STRIDE * HEAD_DIM)
    kvc = _nsa_compress(chunks, P["w_cmp"], l)
    cmp_end = jnp.arange(LANE) * NSA_CMP_STRIDE + (NSA_CMP_LEN - 1)
    tabc = _bias_lookup(P["rel_bias"][:, 2], _t5_bucket(jnp.arange(t)[:, None] - cmp_end[None, :]))
    cover = jnp.asarray(np.pad(_cover_matrix(n_cmp, n_slc), ((0, LANE - n_cmp), (0, LANE - n_slc))))
    o1, sel_n = _nsa_cmp_prefill(proj, kvc, tabc, cover, gates, n_cmp, n_slc)
    o2 = _flash_prefill(proj, QN, KVS, tab=tabs, tab_i=2, sel=sel_n, sel_e=_sel_expand(nk, NSA_SLC_BLOCK), sel_per_kv=True,
                        gates=gates, gate_i=1, addend=o1, out_dtype=F32)
    o_nsa = _flash_prefill(proj, QN, KVW, tab=tabs, tab_i=2, band=True, gates=gates, gate_i=2, addend=o2)
    o_all = jnp.stack([o_fox, o_diff, o_moba, o_nsa]).reshape(4, n, MIX_W)
    return o_all, rows


def _decode_kernel(*refs, pp, ns, diff, lam_init):
    it = iter(refs)
    _pt_ref = next(it)
    q_ref = next(it)
    page_refs = [next(it) for _ in range(pp)]
    new_ref, bias_ref = next(it), next(it)
    if diff:
        dl_ref, gn_ref = next(it), next(it)
    o_ref = next(it)
    m_ref, l_ref, acc_ref = next(it), next(it), next(it)
    n_maps = 2 if diff else 1
    s_id = pl.program_id(1)

    @pl.when(s_id == 0)
    def _():
        m_ref[...] = jnp.full_like(m_ref, NEG)
        l_ref[...] = jnp.zeros_like(l_ref)
        acc_ref[...] = jnp.zeros_like(acc_ref)

    def process(page_list):
        n_keys = len(page_list) * PAGE_SIZE
        for kv in range(N_KV):
            rows_of = lambda ref, j: ref[pl.ds(j * N_KV + kv, PAGE_SIZE, stride=2 * N_KV), :]
            k = jnp.concatenate([rows_of(ref, 0) for ref in page_list], axis=0).astype(BF)
            v = jnp.concatenate([rows_of(ref, 1) for ref in page_list], axis=0).astype(BF)
            bias = bias_ref[kv, :, :n_keys]
            mask = bias > NEG_TEST
            q = q_ref[kv]
            for mi in range(n_maps):
                if diff:
                    lane = lax.broadcasted_iota(jnp.int32, q.shape, 1)
                    half = (lane < DIFF_DH) if mi == 0 else (lane >= DIFF_DH)
                    s = _dot_nt(jnp.where(half, q * DIFF_SCALE, 0.0).astype(BF), k) + bias
                else:
                    s = _dot_nt(q.astype(BF), k) * SCALE + bias
                _online_update(s, mask, v, m_ref.at[mi, kv], l_ref.at[mi, kv], acc_ref.at[mi, kv])

    @pl.when(s_id < ns)
    def _():
        process(page_refs)

    @pl.when(s_id == ns)
    def _():
        process([new_ref])
        for kv in range(N_KV):
            o = _normalized(l_ref.at[0, kv], acc_ref.at[0, kv])
            if diff:
                o = o - _lam_of(dl_ref, lam_init) * _normalized(l_ref.at[1, kv], acc_ref.at[1, kv])
                o = o * lax.rsqrt(jnp.mean(o * o, axis=-1, keepdims=True) + LN_EPS) * gn_ref[...] * (1.0 - lam_init)
            o_ref[kv] = o


def _decode_attn(q, pool, l, page_table, new_kv, bias, pp, diff=None):
    b, _, r, _ = q.shape
    n_pages = page_table.shape[1]
    ns = n_pages // pp
    bb = bias.shape[0]
    page_rows = PAGE_SIZE * 2 * N_KV
    pool2 = pool.reshape(pool.shape[0], pool.shape[1], page_rows, HEAD_DIM)
    blk = (None, None, page_rows, HEAD_DIM)
    page_specs = [pl.BlockSpec(blk, (lambda bi, s, pt, i=i: (l, pt[bi, jnp.minimum(s, ns - 1) * pp + i], 0, 0)))
                  for i in range(pp)]
    in_specs = ([pl.BlockSpec((None, N_KV, r, HEAD_DIM), lambda bi, s, pt: (bi, 0, 0, 0))] + page_specs +
                [pl.BlockSpec((None, page_rows, HEAD_DIM), lambda bi, s, pt: (bi, 0, 0)),
                 pl.BlockSpec((None, N_KV, r, pp * PAGE_SIZE), lambda bi, s, pt: (bi if bb > 1 else 0, 0, 0, s))])
    args = [q] + [pool2] * pp + [new_kv.reshape(b, page_rows, HEAD_DIM), bias]
    lam_init = 0.0
    if diff is not None:
        dl, gn, lam_init = diff
        in_specs += [pl.BlockSpec((4, DIFF_DH), lambda bi, s, pt: (0, 0)),
                     pl.BlockSpec((1, HEAD_DIM), lambda bi, s, pt: (0, 0))]
        args += [dl, gn]
    n_maps = 2 if diff is not None else 1
    return pl.pallas_call(
        functools.partial(_decode_kernel, pp=pp, ns=ns, diff=diff is not None, lam_init=lam_init),
        grid_spec=pltpu.PrefetchScalarGridSpec(
            num_scalar_prefetch=1, grid=(b, ns + 1), in_specs=in_specs,
            out_specs=pl.BlockSpec((None, N_KV, r, HEAD_DIM), lambda bi, s, pt: (bi, 0, 0, 0)),
            scratch_shapes=[pltpu.VMEM((n_maps, N_KV, r, 1), F32), pltpu.VMEM((n_maps, N_KV, r, 1), F32),
                            pltpu.VMEM((n_maps, N_KV, r, HEAD_DIM), F32)]),
        out_shape=jax.ShapeDtypeStruct((b, N_KV, r, HEAD_DIM), F32),
        compiler_params=_cparams(2), name="decode_attn",
    )(page_table, *args)


def _page_sum_kernel(*refs, pp):
    _pt_ref = refs[0]
    page_refs = refs[1:1 + pp]
    o_ref = refs[1 + pp]
    for i, ref in enumerate(page_refs):
        for kv in range(N_KV):
            o_ref[i, kv:kv + 1, :] = jnp.sum(ref[:, kv, :], axis=0, keepdims=True)


def _page_key_sums(pool, l, page_table, pp):
    b, n_pages = page_table.shape
    blk = (None, None, PAGE_SIZE, None, N_KV, HEAD_DIM)
    page_specs = [pl.BlockSpec(blk, (lambda bi, s, pt, i=i: (l, pt[bi, s * pp + i], 0, 0, 0, 0))) for i in range(pp)]
    return pl.pallas_call(
        functools.partial(_page_sum_kernel, pp=pp),
        grid_spec=pltpu.PrefetchScalarGridSpec(
            num_scalar_prefetch=1, grid=(b, n_pages // pp), in_specs=page_specs,
            out_specs=pl.BlockSpec((None, pp, N_KV, HEAD_DIM), lambda bi, s, pt: (bi, s, 0, 0))),
        out_shape=jax.ShapeDtypeStruct((b, n_pages, N_KV, HEAD_DIM), F32),
        compiler_params=_cparams(2), name="page_key_sums",
    )(page_table, *([pool] * pp))


def _cmp_decode_kernel(q_ref, kvc_ref, bias_ref, o_ref, p_ref):
    for kv in range(N_KV):
        q = q_ref[kv].astype(BF)
        bias = bias_ref[kv]
        mask = bias > NEG_TEST
        s = jnp.where(mask, _dot_nt(q, kvc_ref[0, kv].astype(BF)) * SCALE + bias, NEG)
        mx = jnp.max(s, axis=-1, keepdims=True)
        p = jnp.where(mask, jnp.exp(s - mx), 0.0)
        z = jnp.sum(p, axis=-1, keepdims=True)
        pc = p / jnp.where(z > 0, z, 1.0)
        p_ref[kv] = pc
        o_ref[kv] = _dot(pc.astype(BF), kvc_ref[1, kv].astype(BF))


def _cmp_decode(q, kvc, bias):
    b, _, r, _ = q.shape
    n = kvc.shape[3]
    return pl.pallas_call(
        _cmp_decode_kernel, grid=(b,),
        in_specs=[pl.BlockSpec((None, N_KV, r, HEAD_DIM), lambda bi: (bi, 0, 0, 0)),
                  pl.BlockSpec((None, 2, N_KV, n, HEAD_DIM), lambda bi: (bi, 0, 0, 0, 0)),
                  pl.BlockSpec((N_KV, r, n), lambda bi: (0, 0, 0))],
        out_specs=[pl.BlockSpec((None, N_KV, r, HEAD_DIM), lambda bi: (bi, 0, 0, 0)),
                   pl.BlockSpec((None, N_KV, r, n), lambda bi: (bi, 0, 0, 0))],
        out_shape=[jax.ShapeDtypeStruct((b, N_KV, r, HEAD_DIM), F32), jax.ShapeDtypeStruct((b, N_KV, r, n), F32)],
        compiler_params=_cparams(1), name="cmp_decode",
    )(q, kvc, bias)


def _rows_of(x, t):
    b = x.shape[0]
    return jnp.transpose(x.reshape(b, t, N_KV, GROUP, HEAD_DIM), (0, 2, 3, 1, 4)).reshape(b, N_KV, GROUP * t, HEAD_DIM)


def _unrows(o, t):
    b = o.shape[0]
    return jnp.transpose(o.reshape(b, N_KV, GROUP, t, HEAD_DIM), (0, 3, 1, 2, 4)).reshape(b, t, MIX_W)


def _head_rows(a, t):
    b = a.shape[0]
    return a.reshape(b, N_KV, GROUP * t, a.shape[-1])


DEC_PP = 16


def _sample_tables(rel_bias, t, n_pages, w_len, past_len):
    lkp = (n_pages // DEC_PP + 1) * DEC_PP * PAGE_SIZE
    qpos = past_len + jnp.arange(t)
    kpos = jnp.arange(lkp)
    kvalid = (kpos[None, :] <= qpos[:, None]) & (kpos[None, :] < past_len + t)
    t5 = _bias_lookup(rel_bias.reshape(N_BUCKETS, 3 * N_HEADS), _t5_bucket(qpos[:, None] - kpos[None, :]))
    t5 = jnp.where(kvalid[None, None], t5.reshape(3, N_HEADS, t, lkp), NEG)
    n_chunk = (past_len + t) // NSA_CMP_STRIDE
    n_cmp = n_chunk - NSA_CMP_LEN // NSA_CMP_STRIDE + 1
    cmp_end = jnp.arange(n_chunk) * NSA_CMP_STRIDE + (NSA_CMP_LEN - 1)
    cvalid = (cmp_end[None, :] <= qpos[:, None]) & (jnp.arange(n_chunk)[None, :] < n_cmp)
    bias_c = _bias_lookup(rel_bias[:, 2], _t5_bucket(qpos[:, None] - cmp_end[None, :]))
    bias_c = jnp.where(cvalid[None], bias_c, NEG).reshape(N_KV, GROUP * t, n_chunk)
    wp = w_len // PAGE_SIZE
    lkw = 2 * wp * PAGE_SIZE
    kidx = jnp.arange(lkw)
    dw = (w_len + jnp.arange(t))[:, None] - kidx[None, :]
    wvalid = (dw >= 0) & (dw <= NSA_WINDOW) & (kidx[None, :] < w_len + t)
    bias_w = _bias_lookup(rel_bias[:, 2], _t5_bucket(dw))
    bias_w = jnp.where(wvalid[None], bias_w, NEG).reshape(1, N_KV, GROUP * t, lkw)
    return dict(kvalid=kvalid, t5=t5, bias_c=bias_c, bias_w=bias_w)


def _mixers_sample(l, h, P, caches, page_table, past_len, tables):
    b, t, d = h.shape
    n = b * t
    n_pages = page_table.shape[1]
    pp = DEC_PP
    ns = n_pages // pp
    lkp = (ns + 1) * pp * PAGE_SIZE
    proj = _mm(h.reshape(n, d), P["w_in_r"], P["b_in_r"], l).reshape(b, t, PROJ_W)
    seg = lambda c, w: proj[:, :, c * LANE:(c + w) * LANE]
    misc = proj[:, :, MISC * LANE:MISC * LANE + 4 * N_HEADS]
    kvs = lambda c: seg(c, 4).reshape(b, t, 2, N_KV, HEAD_DIM)
    pad_new = lambda kv: jnp.pad(kv, ((0, 0), (0, PAGE_SIZE - t), (0, 0), (0, 0), (0, 0)))
    logf_new = _log_sigmoid(jnp.pad(jnp.swapaxes(misc[..., :N_HEADS], 1, 2), ((0, 0), (0, 0), (0, LANE - t))))[..., :t]
    rows = dict(fox_kv=kvs(KVF), fox_logf=jnp.swapaxes(logf_new, 1, 2), diff_kv=kvs(KVD), moba_kv=kvs(KVM),
                nsa_cmp_kv=kvs(KVC), nsa_slc_kv=kvs(KVS))
    win_new = kvs(KVW)
    win_past = caches["nsa_win_kv"][l]
    w_len = win_past.shape[1]
    rows["nsa_win_kv"] = jnp.concatenate([win_past, win_new], axis=1)[:, -min(NSA_WINDOW, w_len + t):]

    qpos = past_len + jnp.arange(t)
    kpos = jnp.arange(lkp)
    kvalid = tables["kvalid"]
    t5_bias = lambda ti: tables["t5"][ti]

    logf_past = caches["fox_logf"][l][page_table].reshape(b, past_len, N_HEADS)
    lf = jnp.concatenate([jnp.swapaxes(logf_past, 1, 2), logf_new], axis=2)
    lf = jnp.pad(lf, ((0, 0), (0, 0), (0, lkp - past_len - t)))
    f_cum = _cumsum(lf)
    f_q = f_cum[:, :, past_len:past_len + t]
    bias_fox = jnp.where(kvalid[None, None], f_q[..., None] - f_cum[:, :, None, :], NEG)
    o_fox = _decode_attn(_rows_of(seg(QF, 8), t), caches["fox_kv"], l, page_table, pad_new(rows["fox_kv"]),
                         _head_rows(bias_fox, t), pp)

    lam_init = 0.8 - 0.6 * math.exp(-0.3 * l)
    o_diff = _decode_attn(_rows_of(seg(QD, 8), t), caches["diff_kv"], l, page_table, pad_new(rows["diff_kv"]),
                          _head_rows(t5_bias(0)[None], t), pp,
                          diff=(P["diff_lambda"][l], P["diff_norm_g"][l].reshape(1, HEAD_DIM), lam_init))

    lk = past_len + t
    nblk = -(-lk // MOBA_BLOCK)
    ppb = MOBA_BLOCK // PAGE_SIZE
    psum = _page_key_sums(caches["moba_kv"], l, page_table, pp)
    kmean = psum.reshape(b, n_pages // ppb, ppb, N_KV, HEAD_DIM).sum(2) / MOBA_BLOCK
    q_m = seg(QM, 8).reshape(b, t, N_KV, GROUP, HEAD_DIM)
    gs = jnp.einsum("btkgd,bmkd->bkgtm", q_m, kmean)
    own = qpos // MOBA_BLOCK
    blk_ids = jnp.arange(n_pages // ppb)
    gs = jnp.where(blk_ids[None, :] < own[:, None], gs, -jnp.inf)
    top_v, top_i = lax.top_k(gs, min(MOBA_TOPK, nblk))
    kblk = kpos // MOBA_BLOCK
    in_chosen = jnp.any((top_i[..., None] == kblk) & jnp.isfinite(top_v)[..., None], axis=-2)
    keep = in_chosen | (kblk[None, :] == own[:, None])
    bias_moba = jnp.where(keep, t5_bias(1).reshape(N_KV, GROUP, t, lkp)[None], NEG)
    o_moba = _decode_attn(_rows_of(seg(QM, 8), t), caches["moba_kv"], l, page_table, pad_new(rows["moba_kv"]),
                          bias_moba.reshape(b, N_KV, GROUP * t, lkp), pp)

    n_chunk = lk // NSA_CMP_STRIDE
    r_len = NSA_CMP_LEN // NSA_CMP_STRIDE
    n_cmp = n_chunk - r_len + 1
    n_slc = -(-lk // NSA_SLC_BLOCK)
    assert t < NSA_CMP_STRIDE and past_len % NSA_CMP_STRIDE == 0
    kvc = _nsa_compress_paged(caches["nsa_cmp_kv"], l, page_table, P["w_cmp"], pp)
    q_n = _rows_of(seg(QN, 8), t)
    o_cmp, pc = _cmp_decode(q_n, kvc, tables["bias_c"])
    cover = jnp.asarray(_cover_matrix(n_cmp, n_slc))
    imp = jnp.einsum("bkgtm,mj->bktj", pc.reshape(b, N_KV, GROUP, t, n_chunk)[..., :n_cmp], cover)
    cur = (qpos // NSA_SLC_BLOCK)[:, None]
    sl_ids = jnp.arange(n_slc)
    forced = (sl_ids[None, :] == 0) | ((sl_ids[None, :] <= cur) & (sl_ids[None, :] > cur - NSA_N_LOCAL))
    imp = jnp.where(forced, NSA_FORCE, jnp.where(sl_ids[None, :] <= cur, imp, -1.0))
    top_v, top_i = lax.top_k(imp, min(NSA_TOPN, n_slc))
    kslc = jnp.minimum(kpos // NSA_SLC_BLOCK, n_slc - 1)
    keep_s = jnp.any((top_i[..., None] == kslc) & (top_v >= 0)[..., None], axis=-2)
    bias_t5n = t5_bias(2).reshape(N_KV, GROUP, t, lkp)
    bias_slc = jnp.where(keep_s[:, :, None], bias_t5n[None], NEG).reshape(b, N_KV, GROUP * t, lkp)
    o_slc = _decode_attn(q_n, caches["nsa_slc_kv"], l, page_table, pad_new(rows["nsa_slc_kv"]), bias_slc, pp)

    wp = w_len // PAGE_SIZE
    pool_w = caches["nsa_win_kv"].reshape(DEPTH, b * wp, PAGE_SIZE, 2, N_KV, HEAD_DIM)
    pt_w = (jnp.arange(b, dtype=jnp.int32)[:, None] * wp + jnp.arange(wp, dtype=jnp.int32)[None, :])
    o_win = _decode_attn(q_n, pool_w, l, pt_w, pad_new(win_new), tables["bias_w"], wp)

    g = _sigmoid(misc[..., N_HEADS:].reshape(b, t, 3, N_HEADS))[..., None]
    hd = lambda o: _unrows(o, t).reshape(b, t, N_HEADS, HEAD_DIM)
    o_nsa = (g[:, :, 0] * hd(o_cmp) + g[:, :, 1] * hd(o_slc) + g[:, :, 2] * hd(o_win)).reshape(b, t, MIX_W)
    o_all = jnp.stack([_unrows(o_fox, t), _unrows(o_diff, t), _unrows(o_moba, t), o_nsa]).astype(BF).reshape(4, n, MIX_W)
    return o_all, rows


STATE_NAMES = ("fox_kv", "fox_logf", "diff_kv", "moba_kv", "nsa_cmp_kv", "nsa_slc_kv", "nsa_win_kv")


def _run_group(x, mod, P, mixers):
    b, t, d = x.shape
    n = b * t
    rows = {name: [] for name in STATE_NAMES}
    h = _modulate(x, mod, 0)
    for l in range(DEPTH):
        o_all, new_rows = mixers(l, h)
        merged = _gate_merge(h.reshape(n, d), o_all, P["w_gate"], P["b_gate"], P["w_branch"], l)
        y = _mm(merged, P["w_o"], P["zero_bias"], l).reshape(b, t, d)
        x, h2, h2_packed = _ln_mod(x, y, None, mod, P["ln_g"], P["ln_b"], l, 0, l, 3)
        y_slots, gates = _moe(h2, h2_packed, P["w_router"], P["b_router"], P["w_gu"], P["b_gu"], P["w_dn"], P["b_dn"], l)
        x, h, _ = _ln_mod(x, y_slots, gates, mod, P["ln_g"], P["ln_b"], l, 1, l + 1 if l + 1 < DEPTH else None, 0)
        for name in STATE_NAMES:
            rows[name].append(new_rows[name])
    return x, {name: jnp.stack(rows[name]) for name in STATE_NAMES}


def kernel(x_prompt, x_sample, cache_fox_kv, cache_fox_logf, cache_diff_kv, cache_moba_kv, cache_nsa_cmp_kv,
           cache_nsa_slc_kv, cache_nsa_win_kv, page_table, c_prompt, c_sample, w_ada, b_ada, ln_g, ln_b, w_in, b_in,
           diff_lambda, diff_norm_g, rel_bias, w_cmp, w_gate, b_gate, w_branch, w_o, w_router, b_router,
           w_gu, b_gu, w_dn, b_dn):
    bp = x_prompt.shape[0]
    bs = x_sample.shape[0]
    past_len = page_table.shape[1] * PAGE_SIZE

    o_f = MIX_W + KV_W
    o_g = D_IN - 3 * N_HEADS

    def reorder(w):
        pad = jnp.zeros(w.shape[:-1] + (PROJ_W - D_IN,), w.dtype)
        return jnp.concatenate([w[..., :o_f], w[..., o_f + N_HEADS:o_g], w[..., o_f:o_f + N_HEADS], w[..., o_g:], pad], -1)

    P = dict(w_in_r=reorder(w_in), b_in_r=reorder(b_in).reshape(DEPTH, 1, PROJ_W), diff_lambda=diff_lambda,
             diff_norm_g=diff_norm_g, rel_bias=rel_bias, w_cmp=w_cmp, w_gate=w_gate, b_gate=b_gate, w_branch=w_branch,
             w_o=w_o, zero_bias=jnp.zeros((DEPTH, 1, D_MODEL), F32), w_router=w_router, b_router=b_router,
             w_gu=w_gu, b_gu=b_gu, w_dn=w_dn, b_dn=b_dn, ln_g=ln_g, ln_b=ln_b)

    r_pad = -(-(bp + bs) // 8) * 8
    c_all = jnp.pad(jnp.concatenate([c_prompt, c_sample], 0), ((0, r_pad - bp - bs), (0, 0)))
    mod_all = _ada_mod(c_all, w_ada, b_ada).reshape(DEPTH, r_pad, 6, D_MODEL)
    mod_p = mod_all[:, :bp]
    mod_s = mod_all[:, bp:bp + bs]

    n_off = x_prompt.shape[1] // TB
    tabs = _t5_tiles(rel_bias, n_off)
    y_prompt, sp = _run_group(x_prompt, mod_p, P, lambda l, h: _mixers_prompt(l, h, P, tabs))

    caches = dict(fox_kv=cache_fox_kv, fox_logf=cache_fox_logf, diff_kv=cache_diff_kv, moba_kv=cache_moba_kv,
                  nsa_cmp_kv=cache_nsa_cmp_kv, nsa_slc_kv=cache_nsa_slc_kv, nsa_win_kv=cache_nsa_win_kv)
    tables = _sample_tables(rel_bias, x_sample.shape[1], page_table.shape[1], cache_nsa_win_kv.shape[2], past_len)
    y_sample, ss = _run_group(x_sample, mod_s, P,
                              lambda l, h: _mixers_sample(l, h, P, caches, page_table, past_len, tables))
    return (y_prompt, y_sample,
            sp["fox_kv"], sp["fox_logf"], sp["diff_kv"], sp["moba_kv"], sp["nsa_cmp_kv"], sp["nsa_slc_kv"], sp["nsa_win_kv"],
            ss["fox_kv"], ss["fox_logf"], ss["diff_kv"], ss["moba_kv"], ss["nsa_cmp_kv"], ss["nsa_slc_kv"], ss["nsa_win_kv"])
```

```python
import functools
import math

import jax
import jax.numpy as jnp
import numpy as np
from jax import lax
from jax.experimental import pallas as pl
from jax.experimental.pallas import tpu as pltpu

D_MODEL = 4096
DEPTH = 2
PAGE_SIZE = 128
HEAD_DIM = 128
N_HEADS = 8
N_KV = 2
GROUP = N_HEADS // N_KV
MIX_W = N_HEADS * HEAD_DIM
KV_W = 2 * N_KV * HEAD_DIM
DIFF_DH = HEAD_DIM // 2
SCALE = HEAD_DIM ** -0.5
DIFF_SCALE = DIFF_DH ** -0.5
MOBA_BLOCK = 256
MOBA_TOPK = 3
NSA_CMP_LEN = 32
NSA_CMP_STRIDE = 16
NSA_SLC_BLOCK = 64
NSA_TOPN = 16
NSA_N_LOCAL = 2
NSA_WINDOW = 512
NSA_FORCE = 1e9
N_BUCKETS = 32
MAX_DISTANCE = 4096
N_EXPERTS = 32
TOP_K = 4
D_EXPERT = D_MODEL // 4
SWIGLU_ALPHA = 1.702
SWIGLU_LIMIT = 7.0
DN_ALPHA = (2 * DEPTH) ** 0.25
LN_EPS = 1e-5
IN_SPLITS = (MIX_W, KV_W, N_HEADS, MIX_W, KV_W, MIX_W, KV_W, MIX_W, KV_W, KV_W, KV_W, 3 * N_HEADS)
D_IN = sum(IN_SPLITS)

LANE = 128
QF, KVF, QD, KVD, QM, KVM, QN, KVC, KVS, KVW, MISC = 0, 8, 12, 20, 24, 32, 36, 44, 48, 52, 56
PROJ_W = 60 * LANE
NEG = -1e30
NEG_TEST = -1e29
TB = 512
VMEM_LIMIT = 56 * 1024 * 1024
BF = jnp.bfloat16
F32 = jnp.float32


def _cparams(n_axes):
    return pltpu.CompilerParams(dimension_semantics=("arbitrary",) * n_axes, vmem_limit_bytes=VMEM_LIMIT)


def _dot(a, b):
    return jnp.dot(a, b, preferred_element_type=F32)


def _dot_nt(a, b):
    return lax.dot_general(a, b, (((1,), (1,)), ((), ())), preferred_element_type=F32)


def _dot_hi(a, b):
    return jnp.dot(a, b, precision=lax.Precision.HIGHEST, preferred_element_type=F32)


def _sigmoid(x):
    return 1.0 / (1.0 + jnp.exp(-x))


def _cast_rows(src_ref, dst_ref, rows, chunk=256):
    def body(r, c):
        sl = pl.ds(pl.multiple_of(r * chunk, chunk), chunk)
        dst_ref[sl, :] = src_ref[sl, :].astype(BF)
        return c
    lax.fori_loop(0, rows // chunk, body, 0)


def _ada_kernel(c_ref, w_ref, b_ref, o_ref):
    c = c_ref[...]
    a = (c * _sigmoid(c)).astype(BF)
    o_ref[...] = _dot(a, w_ref[...].astype(BF)) + b_ref[...]


def _ada_mod(c_all, w_ada, b_ada):
    r = c_all.shape[0]
    tn = 512
    n6 = 6 * D_MODEL
    return pl.pallas_call(
        _ada_kernel,
        grid=(DEPTH, n6 // tn),
        in_specs=[pl.BlockSpec((r, D_MODEL), lambda l, n: (0, 0)),
                  pl.BlockSpec((None, D_MODEL, tn), lambda l, n: (l, 0, n)),
                  pl.BlockSpec((None, 1, tn), lambda l, n: (l, 0, n))],
        out_specs=pl.BlockSpec((None, r, tn), lambda l, n: (l, 0, n)),
        out_shape=jax.ShapeDtypeStruct((DEPTH, r, n6), F32),
        compiler_params=_cparams(2), name="ada_mod",
    )(c_all, w_ada, b_ada.reshape(DEPTH, 1, n6))


def _modulate_kernel(x_ref, m_ref, o_ref):
    o_ref[...] = (x_ref[...] * (1.0 + m_ref[1:2, :]) + m_ref[0:1, :]).astype(BF)


def _modulate(x, mod, l):
    b, t, d = x.shape
    tm = min(t, 512)
    return pl.pallas_call(
        _modulate_kernel,
        grid=(b, t // tm),
        in_specs=[pl.BlockSpec((None, tm, d), lambda i, j: (i, j, 0)),
                  pl.BlockSpec((None, None, 6, d), lambda i, j: (l, i, 0, 0))],
        out_specs=pl.BlockSpec((None, tm, d), lambda i, j: (i, j, 0)),
        out_shape=jax.ShapeDtypeStruct((b, t, d), BF),
        compiler_params=_cparams(2), name="modulate",
    )(x, mod)


def _mm_kernel(x_ref, w_ref, b_ref, o_ref, wbf_ref):
    @pl.when(pl.program_id(1) == 0)
    def _():
        _cast_rows(w_ref, wbf_ref, w_ref.shape[0])
    o_ref[...] = (_dot(x_ref[...], wbf_ref[...]) + b_ref[...]).astype(o_ref.dtype)


def _mm(x, w, bias, l, out_dtype=F32, tn=512):
    m, k = x.shape
    n = w.shape[-1]
    tm = min(m, 512)
    return pl.pallas_call(
        _mm_kernel,
        grid=(n // tn, m // tm),
        in_specs=[pl.BlockSpec((tm, k), lambda j, i: (i, 0)),
                  pl.BlockSpec((None, k, tn), lambda j, i: (l, 0, j)),
                  pl.BlockSpec((None, 1, tn), lambda j, i: (l, 0, j))],
        out_specs=pl.BlockSpec((tm, tn), lambda j, i: (i, j)),
        out_shape=jax.ShapeDtypeStruct((m, n), out_dtype),
        scratch_shapes=[pltpu.VMEM((k, tn), BF)],
        compiler_params=_cparams(2), name="mm",
    )(x, w, bias)


def _gate_merge_kernel(h_ref, o_ref, wg_ref, bg_ref, wb_ref, out_ref, acc_ref, wgbf_ref, wbbf_ref):
    br = pl.program_id(2)

    @pl.when(br == 0)
    def _():
        acc_ref[...] = jnp.zeros_like(acc_ref)

    _cast_rows(wg_ref, wgbf_ref, wg_ref.shape[0])
    _cast_rows(wb_ref, wbbf_ref, wb_ref.shape[0])
    g = _dot(h_ref[...], wgbf_ref[...]) + bg_ref[...]
    u = _dot(o_ref[...], wbbf_ref[...])
    acc_ref[...] += _sigmoid(g) * u

    @pl.when(br == 3)
    def _():
        out_ref[...] = acc_ref[...].astype(out_ref.dtype)


def _gate_merge(h, o_all, w_gate, b_gate, w_branch, l):
    m = h.shape[0]
    tm = min(m, 1024)
    tn = 512
    return pl.pallas_call(
        _gate_merge_kernel,
        grid=(D_MODEL // tn, m // tm, 4),
        in_specs=[pl.BlockSpec((tm, D_MODEL), lambda j, i, b: (i, 0)),
                  pl.BlockSpec((None, tm, MIX_W), lambda j, i, b: (b, i, 0)),
                  pl.BlockSpec((None, None, D_MODEL, tn), lambda j, i, b: (l, b, 0, j)),
                  pl.BlockSpec((None, None, 1, tn), lambda j, i, b: (l, b, 0, j)),
                  pl.BlockSpec((None, None, MIX_W, tn), lambda j, i, b: (l, b, 0, j))],
        out_specs=pl.BlockSpec((tm, tn), lambda j, i, b: (i, j)),
        out_shape=jax.ShapeDtypeStruct((m, D_MODEL), BF),
        scratch_shapes=[pltpu.VMEM((tm, tn), F32), pltpu.VMEM((D_MODEL, tn), BF), pltpu.VMEM((MIX_W, tn), BF)],
        compiler_params=_cparams(3), name="gate_merge",
    )(h, o_all, w_gate, b_gate.reshape(DEPTH, 4, 1, D_MODEL), w_branch)


def _pack_halves(hb):
    u = lax.bitcast_convert_type(hb.astype(F32), jnp.uint32)
    w = hb.shape[-1] // 2
    return (u[:, :w] >> 16) | (u[:, w:] & jnp.uint32(0xFFFF0000))


def _unpack_halves(xw):
    lo = lax.bitcast_convert_type(xw << 16, F32).astype(BF)
    hi = lax.bitcast_convert_type(xw & jnp.uint32(0xFFFF0000), F32).astype(BF)
    return lo, hi


def _ln_kernel(*refs, n_slots, gate_row, shift_row, emit_h, emit_packed):
    it = iter(refs)
    x_ref, y_ref = next(it), next(it)
    wt_ref = next(it) if n_slots > 1 else None
    ma_ref, mb_ref, g_ref, b_ref = next(it), next(it), next(it), next(it)
    xo_ref = next(it)
    ho_ref = next(it) if emit_h else None
    po_ref = next(it) if emit_packed else None
    if n_slots > 1:
        y = y_ref[0] * wt_ref[:, 0:1]
        for s in range(1, n_slots):
            y = y + y_ref[s] * wt_ref[:, s:s + 1]
    else:
        y = y_ref[...]
    z = DN_ALPHA * x_ref[...] + (1.0 + ma_ref[gate_row:gate_row + 1, :]) * y
    mu = jnp.mean(z, axis=-1, keepdims=True)
    zc = z - mu
    var = jnp.mean(zc * zc, axis=-1, keepdims=True)
    xn = zc * lax.rsqrt(var + LN_EPS) * g_ref[...] + b_ref[...]
    xo_ref[...] = xn
    if emit_h:
        hb = (xn * (1.0 + mb_ref[shift_row + 1:shift_row + 2, :]) + mb_ref[shift_row:shift_row + 1, :]).astype(BF)
        ho_ref[...] = hb
        if emit_packed:
            po_ref[...] = _pack_halves(hb)


def _ln_mod(x, y, wts, mod, ln_g, ln_b, l, which, l_next, shift_row):
    b, t, d = x.shape
    n_slots = 1 if wts is None else wts.shape[2]
    tm = min(t, 256 if n_slots == 1 else 128)
    emit_h = l_next is not None
    emit_packed = emit_h and which == 0
    gate_row = 2 if which == 0 else 5
    ln_i = which
    in_specs = [pl.BlockSpec((None, tm, d), lambda i, j: (i, j, 0))]
    args = [x, y]
    if n_slots > 1:
        in_specs += [pl.BlockSpec((n_slots, None, tm, d), lambda i, j: (0, i, j, 0)),
                     pl.BlockSpec((None, tm, n_slots), lambda i, j: (i, j, 0))]
        args.append(wts)
    else:
        in_specs.append(pl.BlockSpec((None, tm, d), lambda i, j: (i, j, 0)))
    lb = l if l_next is None else l_next
    in_specs += [pl.BlockSpec((None, None, 6, d), lambda i, j: (l, i, 0, 0)),
                 pl.BlockSpec((None, None, 6, d), lambda i, j: (lb, i, 0, 0)),
                 pl.BlockSpec((None, None, 1, d), lambda i, j: (l, ln_i, 0, 0)),
                 pl.BlockSpec((None, None, 1, d), lambda i, j: (l, ln_i, 0, 0))]
    args += [mod, mod, ln_g.reshape(DEPTH, 2, 1, d), ln_b.reshape(DEPTH, 2, 1, d)]
    out_specs = [pl.BlockSpec((None, tm, d), lambda i, j: (i, j, 0))]
    out_shape = [jax.ShapeDtypeStruct((b, t, d), F32)]
    if emit_h:
        out_specs.append(pl.BlockSpec((None, tm, d), lambda i, j: (i, j, 0)))
        out_shape.append(jax.ShapeDtypeStruct((b, t, d), BF))
    if emit_packed:
        out_specs.append(pl.BlockSpec((None, tm, d // 2), lambda i, j: (i, j, 0)))
        out_shape.append(jax.ShapeDtypeStruct((b, t, d // 2), jnp.uint32))
    res = pl.pallas_call(
        functools.partial(_ln_kernel, n_slots=n_slots, gate_row=gate_row, shift_row=shift_row, emit_h=emit_h,
                          emit_packed=emit_packed),
        grid=(b, t // tm), in_specs=in_specs, out_specs=out_specs, out_shape=out_shape,
        compiler_params=_cparams(2), name="ln_mod",
    )(*args)
    return tuple(res) + (None,) * (3 - len(res))


def _gather_rows_kernel(idx_ref, live_ref, src_ref, out_ref, *scratch, tr, unpack):
    if unpack:
        dst_ref, sem = scratch
    else:
        dst_ref, (sem,) = out_ref, scratch

    def row_copy(r, src_row):
        return pltpu.make_async_copy(src_ref.at[pl.ds(src_row, 1)], dst_ref.at[pl.ds(r, 1)], sem)

    @pl.when(live_ref[0, 0] > 0)
    def _():
        def issue(r8, c):
            for u in range(8):
                r = r8 * 8 + u
                row_copy(r, idx_ref[0, r]).start(priority=u % 2)
            return c
        lax.fori_loop(0, tr // 8, issue, 0)

        def wait(r, c):
            row_copy(r, 0).wait()
            return c
        lax.fori_loop(0, tr, wait, 0, unroll=8)
        if unpack:
            w = dst_ref.shape[1]
            out_ref[:, :w], out_ref[:, w:] = _unpack_halves(dst_ref[...])

    @pl.when(live_ref[0, 0] == 0)
    def _():
        out_ref[...] = jnp.zeros_like(out_ref)


def _gather_rows(src, idx, live=None, unpack=False):
    n_out = idx.shape[0]
    w = src.shape[1]
    tr = next(c for c in (256, 128, 64, 32, 16, 8) if n_out % c == 0)
    nt = n_out // tr
    if live is None:
        live = jnp.ones((nt,), jnp.int32)
    out_w, out_dtype = (2 * w, BF) if unpack else (w, src.dtype)
    scratch = ([pltpu.VMEM((tr, w), src.dtype)] if unpack else []) + [pltpu.SemaphoreType.DMA(())]
    return pl.pallas_call(
        functools.partial(_gather_rows_kernel, tr=tr, unpack=unpack), grid=(nt,),
        in_specs=[pl.BlockSpec((None, 1, tr), lambda t: (t, 0, 0), memory_space=pltpu.SMEM),
                  pl.BlockSpec((None, 1, 1), lambda t: (t, 0, 0), memory_space=pltpu.SMEM),
                  pl.BlockSpec(memory_space=pl.ANY)],
        out_specs=pl.BlockSpec((tr, out_w), lambda t: (t, 0)),
        out_shape=jax.ShapeDtypeStruct((n_out, out_w), out_dtype),
        scratch_shapes=scratch,
        compiler_params=_cparams(1), name="gather_rows",
    )(idx.reshape(nt, 1, tr), live.reshape(nt, 1, 1), src)


def _topk_lanes(vals, lane, k, floor):
    picks = []
    lane_f = lane.astype(F32)
    for _ in range(k):
        mx = jnp.max(vals, axis=-1, keepdims=True)
        idx = jnp.min(jnp.where(vals == mx, lane_f, 4096.0), axis=-1, keepdims=True)
        picks.append((mx, idx))
        vals = jnp.where(lane_f == idx, floor, vals)
    return picks


def _router_kernel(x_ref, w_ref, b_ref, gate_ref, idx_ref):
    logits = _dot(x_ref[...], w_ref[...].astype(BF)) + b_ref[...]
    lane = lax.broadcasted_iota(jnp.int32, logits.shape, 1)
    vals = jnp.where(lane < N_EXPERTS, logits, NEG)
    picks = _topk_lanes(vals, lane, TOP_K, -3e38)
    v0 = picks[0][0]
    es = [jnp.exp(v - v0) for v, _ in picks]
    z = es[0] + es[1] + es[2] + es[3]
    gates = jnp.zeros(logits.shape, F32)
    idxs = jnp.zeros(logits.shape, F32)
    for k in range(TOP_K):
        gates = jnp.where(lane == k, es[k] / z, gates)
        idxs = jnp.where(lane == k, picks[k][1], idxs)
    gate_ref[...] = gates
    idx_ref[...] = idxs.astype(jnp.int32)


def _router(h2, w_router, b_router, l):
    m = h2.shape[0]
    tm = min(m, 512)
    w = jnp.pad(w_router, ((0, 0), (0, 0), (0, LANE - N_EXPERTS)))
    bb = jnp.pad(b_router, ((0, 0), (0, LANE - N_EXPERTS))).reshape(DEPTH, 1, LANE)
    return pl.pallas_call(
        _router_kernel,
        grid=(m // tm,),
        in_specs=[pl.BlockSpec((tm, D_MODEL), lambda i: (i, 0)),
                  pl.BlockSpec((None, D_MODEL, LANE), lambda i: (l, 0, 0)),
                  pl.BlockSpec((None, 1, LANE), lambda i: (l, 0, 0))],
        out_specs=[pl.BlockSpec((tm, LANE), lambda i: (i, 0)), pl.BlockSpec((tm, LANE), lambda i: (i, 0))],
        out_shape=[jax.ShapeDtypeStruct((m, LANE), F32), jax.ShapeDtypeStruct((m, LANE), jnp.int32)],
        compiler_params=_cparams(1), name="router",
    )(h2, w, bb)


def _new_expert(te_ref, t):
    return (t == 0) | (te_ref[t] != te_ref[jnp.maximum(t - 1, 0)])


def _moe_up_kernel(te_ref, tv_ref, x_ref, wg_ref, wu_ref, bg_ref, bu_ref, o_ref, wgbf_ref, wubf_ref):
    t = pl.program_id(1)
    live = tv_ref[t] > 0

    @pl.when(live & _new_expert(te_ref, t))
    def _():
        _cast_rows(wg_ref, wgbf_ref, wg_ref.shape[0])
        _cast_rows(wu_ref, wubf_ref, wu_ref.shape[0])

    @pl.when(live)
    def _():
        x = x_ref[...]
        g = _dot(x, wgbf_ref[...]) + bg_ref[...]
        u = _dot(x, wubf_ref[...]) + bu_ref[...]
        g = jnp.minimum(g, SWIGLU_LIMIT)
        u = jnp.clip(u, -SWIGLU_LIMIT, SWIGLU_LIMIT)
        o_ref[...] = ((u + 1.0) * g * _sigmoid(SWIGLU_ALPHA * g)).astype(o_ref.dtype)

    @pl.when(jnp.logical_not(live))
    def _():
        o_ref[...] = jnp.zeros_like(o_ref)


def _moe_dn_kernel(te_ref, tv_ref, a_ref, w_ref, b_ref, o_ref, wbf_ref):
    t = pl.program_id(1)
    live = tv_ref[t] > 0

    @pl.when(live & _new_expert(te_ref, t))
    def _():
        _cast_rows(w_ref, wbf_ref, w_ref.shape[0])

    @pl.when(live)
    def _():
        o_ref[...] = _dot(a_ref[...], wbf_ref[...]) + b_ref[...]

    @pl.when(jnp.logical_not(live))
    def _():
        o_ref[...] = jnp.zeros_like(o_ref)


def _moe_experts(x_rows, tile_e, tile_v, w_gu, b_gu, w_dn, b_dn, l, tm):
    r = x_rows.shape[0]
    nt = r // tm
    tn = 256
    nj = D_EXPERT // tn
    b_gu4 = b_gu.reshape(DEPTH, N_EXPERTS, 1, 2 * D_EXPERT)
    act = pl.pallas_call(
        _moe_up_kernel,
        grid_spec=pltpu.PrefetchScalarGridSpec(
            num_scalar_prefetch=2, grid=(nj, nt),
            in_specs=[pl.BlockSpec((tm, D_MODEL), lambda j, t, te, tv: (t * tv[t], 0)),
                      pl.BlockSpec((None, None, D_MODEL, tn), lambda j, t, te, tv: (l, te[t], 0, j)),
                      pl.BlockSpec((None, None, D_MODEL, tn), lambda j, t, te, tv: (l, te[t], 0, nj + j)),
                      pl.BlockSpec((None, None, 1, tn), lambda j, t, te, tv: (l, te[t], 0, j)),
                      pl.BlockSpec((None, None, 1, tn), lambda j, t, te, tv: (l, te[t], 0, nj + j))],
            out_specs=pl.BlockSpec((tm, tn), lambda j, t, te, tv: (t, j)),
            scratch_shapes=[pltpu.VMEM((D_MODEL, tn), BF), pltpu.VMEM((D_MODEL, tn), BF)]),
        out_shape=jax.ShapeDtypeStruct((r, D_EXPERT), BF),
        compiler_params=_cparams(2), name="moe_up",
    )(tile_e, tile_v, x_rows, w_gu, w_gu, b_gu4, b_gu4)
    tn2 = 1024
    return pl.pallas_call(
        _moe_dn_kernel,
        grid_spec=pltpu.PrefetchScalarGridSpec(
            num_scalar_prefetch=2, grid=(D_MODEL // tn2, nt),
            in_specs=[pl.BlockSpec((tm, D_EXPERT), lambda j, t, te, tv: (t * tv[t], 0)),
                      pl.BlockSpec((None, None, D_EXPERT, tn2), lambda j, t, te, tv: (l, te[t], 0, j)),
                      pl.BlockSpec((None, None, 1, tn2), lambda j, t, te, tv: (l, te[t], 0, j))],
            out_specs=pl.BlockSpec((tm, tn2), lambda j, t, te, tv: (t, j)),
            scratch_shapes=[pltpu.VMEM((D_EXPERT, tn2), BF)]),
        out_shape=jax.ShapeDtypeStruct((r, D_MODEL), F32),
        compiler_params=_cparams(2), name="moe_dn",
    )(tile_e, tile_v, act, w_dn, b_dn.reshape(DEPTH, N_EXPERTS, 1, D_MODEL))


def _moe(h2, h2_packed, w_router, b_router, w_gu, b_gu, w_dn, b_dn, l):
    b, t, d = h2.shape
    n_tok = b * t
    xs = h2.reshape(n_tok, d)
    gate_l, idx_l = _router(xs, w_router, b_router, l)
    gates = gate_l[:, :TOP_K]
    e_flat = idx_l[:, :TOP_K].reshape(-1)
    n_asg = n_tok * TOP_K
    tm = 512 if n_asg >= 512 * N_EXPERTS else 16
    order = jnp.argsort(e_flat, stable=True)
    counts = jnp.sum(jax.nn.one_hot(e_flat, N_EXPERTS, dtype=jnp.int32), axis=0)
    padded = (counts + tm - 1) // tm * tm
    pad_end = jnp.cumsum(padded)
    pad_start = pad_end - padded
    start = jnp.cumsum(counts) - counts
    e_sorted = e_flat[order]
    dest_sorted = pad_start[e_sorted] + jnp.arange(n_asg, dtype=jnp.int32) - start[e_sorted]
    n_rows = (n_asg // tm + N_EXPERTS) * tm
    nt = n_rows // tm
    row_tok = jnp.zeros((n_rows,), jnp.int32).at[dest_sorted].set((order // TOP_K).astype(jnp.int32))
    pos = jnp.zeros((n_asg,), jnp.int32).at[order].set(dest_sorted.astype(jnp.int32))
    tile_start = jnp.arange(nt, dtype=jnp.int32) * tm
    tile_e = jnp.minimum(jnp.searchsorted(pad_end, tile_start, side="right"), N_EXPERTS - 1).astype(jnp.int32)
    tile_v = (tile_start < pad_end[-1]).astype(jnp.int32)
    tr = next(c for c in (256, 128, 64, 32, 16, 8) if n_rows % c == 0)
    live = (jnp.arange(n_rows // tr, dtype=jnp.int32) * tr < pad_end[-1]).astype(jnp.int32)
    x_rows = _gather_rows(h2_packed.reshape(n_tok, d // 2), row_tok, live=live, unpack=True)
    y_rows = _moe_experts(x_rows, tile_e, tile_v, w_gu, b_gu, w_dn, b_dn, l, tm)
    y_slots = _gather_rows(y_rows, pos.reshape(n_tok, TOP_K).T.reshape(-1)).reshape(TOP_K, b, t, d)
    return y_slots, gates.reshape(b, t, TOP_K)


def _t5_bucket(dist):
    n = jnp.maximum(dist, 0)
    exact = N_BUCKETS // 2
    nf = jnp.maximum(n, 1).astype(F32)
    large = exact + (jnp.log(nf / exact) / math.log(MAX_DISTANCE / exact) * (N_BUCKETS - exact)).astype(jnp.int32)
    return jnp.where(n < exact, n, jnp.minimum(large, N_BUCKETS - 1))


def _bias_lookup(table, bucket):
    out = jnp.zeros((table.shape[1],) + bucket.shape, F32)
    for b in range(N_BUCKETS):
        out = jnp.where(bucket == b, table[b].reshape((-1,) + (1,) * bucket.ndim), out)
    return out


def _toeplitz_kernel(g_ref, o_ref):
    x = jnp.broadcast_to(g_ref[...], (TB, 2 * TB))
    o_ref[...] = pltpu.roll(x, TB + 1, 1, stride=1, stride_axis=0)[:, :TB]


def _t5_tiles(rel_bias, n_off):
    o = jnp.arange(n_off)[:, None]
    y = jnp.arange(2 * TB)[None, :]
    g = jnp.transpose(rel_bias[_t5_bucket(o * TB + (TB - 1) - y)], (2, 3, 0, 1))
    n_mh = 3 * N_HEADS
    return pl.pallas_call(
        _toeplitz_kernel, grid=(n_mh, n_off),
        in_specs=[pl.BlockSpec((None, None, 1, 2 * TB), lambda m, i: (m, i, 0, 0))],
        out_specs=pl.BlockSpec((None, None, TB, TB), lambda m, i: (m, i, 0, 0)),
        out_shape=jax.ShapeDtypeStruct((n_mh, n_off, TB, TB), F32),
        compiler_params=_cparams(2), name="t5_tiles",
    )(g.reshape(n_mh, n_off, 1, 2 * TB))


def _lam_of(dl_ref, lam_init):
    a = jnp.sum(dl_ref[0:1, :] * dl_ref[1:2, :], axis=-1, keepdims=True)
    b = jnp.sum(dl_ref[2:3, :] * dl_ref[3:4, :], axis=-1, keepdims=True)
    return jnp.exp(a) - jnp.exp(b) + lam_init


def _online_update(s, mask, v, m_ref, l_ref, acc_ref):
    if mask is not None:
        s = jnp.where(mask, s, NEG)
    m_prev = m_ref[...]
    m_new = jnp.maximum(m_prev, jnp.max(s, axis=-1, keepdims=True))
    alpha = jnp.exp(m_prev - m_new)
    p = jnp.exp(s - m_new)
    if mask is not None:
        p = jnp.where(mask, p, 0.0)
    l_ref[...] = alpha * l_ref[...] + jnp.sum(p, axis=-1, keepdims=True)
    acc_ref[...] = alpha * acc_ref[...] + _dot(p.astype(BF), v)
    m_ref[...] = m_new


def _normalized(l_ref, acc_ref):
    l = l_ref[...]
    return acc_ref[...] / jnp.where(l > 0, l, 1.0)


def _flash_kernel(*refs, nq, band, fox, sel, sel_shared, diff, gated, addend, lam_init):
    it = iter(refs)
    q_ref, k_ref, v_ref = next(it), next(it), next(it)
    if fox:
        fq_ref, fk_ref = next(it), next(it)
    else:
        tab_ref = next(it)
    if sel:
        sel_ref, e_ref = next(it), next(it)
    if gated:
        g_ref = next(it)
    if addend:
        add_ref = next(it)
    if diff:
        dl_ref, gn_ref = next(it), next(it)
    o_ref = next(it)
    n_maps = 2 if diff else 1
    m_ref, l_ref, acc_ref = next(it), next(it), next(it)

    i = pl.program_id(2)
    jj = pl.program_id(3)
    if band:
        j = i - 1 + jj
        active = j >= 0
        last = jj == 1
    else:
        j = jj
        active = jj <= i
        last = jj == nq - 1

    @pl.when(jj == 0)
    def _():
        m_ref[...] = jnp.full_like(m_ref, NEG)
        l_ref[...] = jnp.zeros_like(l_ref)
        acc_ref[...] = jnp.zeros_like(acc_ref)

    def tile(positional):
        k = k_ref[...].astype(BF)
        v = v_ref[...].astype(BF)
        pos_mask = None
        if positional:
            row = lax.broadcasted_iota(jnp.int32, (TB, TB), 0)
            col = lax.broadcasted_iota(jnp.int32, (TB, TB), 1)
            dpos = (i - j) * TB + row - col
            pos_mask = dpos >= 0
            if band:
                pos_mask = pos_mask & (dpos <= NSA_WINDOW)
        for g in range(GROUP):
            q = q_ref[:, g * HEAD_DIM:(g + 1) * HEAD_DIM]
            mask = pos_mask
            if sel:
                sm = _dot(sel_ref[0 if sel_shared else g].astype(BF), e_ref[...]) > 0.5
                mask = sm if mask is None else (mask & sm)
            bias = (fq_ref[g] - fk_ref[g]) if fox else tab_ref[g]
            for mi in range(n_maps):
                if diff:
                    lane = lax.broadcasted_iota(jnp.int32, q.shape, 1)
                    half = (lane < DIFF_DH) if mi == 0 else (lane >= DIFF_DH)
                    s = _dot_nt(jnp.where(half, q * DIFF_SCALE, 0.0).astype(BF), k) + bias
                else:
                    s = _dot_nt(q.astype(BF), k) * SCALE + bias
                _online_update(s, mask, v, m_ref.at[mi, g], l_ref.at[mi, g], acc_ref.at[mi, g])

    if band:
        pl.when(active)(lambda: tile(True))
    else:
        pl.when(active & (j == i))(lambda: tile(True))
        pl.when(active & (j != i))(lambda: tile(False))

    @pl.when(last)
    def _():
        for g in range(GROUP):
            o = _normalized(l_ref.at[0, g], acc_ref.at[0, g])
            if diff:
                o = o - _lam_of(dl_ref, lam_init) * _normalized(l_ref.at[1, g], acc_ref.at[1, g])
                o = o * lax.rsqrt(jnp.mean(o * o, axis=-1, keepdims=True) + LN_EPS) * gn_ref[...] * (1.0 - lam_init)
            if gated:
                o = _sigmoid(g_ref[g]) * o
            if addend:
                o = o + add_ref[:, g * HEAD_DIM:(g + 1) * HEAD_DIM]
            o_ref[:, g * HEAD_DIM:(g + 1) * HEAD_DIM] = o.astype(o_ref.dtype)


def _flash_prefill(proj, qcol, kvcol, *, fq=None, fk=None, tab=None, tab_i=0, sel=None, sel_e=None, sel_per_kv=False,
                   band=False, gates=None, gate_i=0, addend=None, diff=None, out_dtype=BF):
    b, t, _ = proj.shape
    nq = t // TB
    nkk = 2 if band else nq
    if band:
        jmap = lambda i, jj: jnp.maximum(i - 1 + jj, 0)
    else:
        jmap = lambda i, jj: jnp.minimum(jj, i)
    gw = GROUP * LANE
    in_specs = [pl.BlockSpec((None, TB, gw), lambda bi, kv, i, jj: (bi, i, qcol // GROUP + kv)),
                pl.BlockSpec((None, TB, LANE), lambda bi, kv, i, jj: (bi, jmap(i, jj), kvcol + kv)),
                pl.BlockSpec((None, TB, LANE), lambda bi, kv, i, jj: (bi, jmap(i, jj), kvcol + N_KV + kv))]
    args = [proj, proj, proj]
    if fq is not None:
        in_specs += [pl.BlockSpec((None, GROUP, TB, 1), lambda bi, kv, i, jj: (bi, kv, i, 0)),
                     pl.BlockSpec((None, GROUP, 1, TB), lambda bi, kv, i, jj: (bi, kv, 0, jmap(i, jj)))]
        args += [fq, fk]
    else:
        in_specs.append(pl.BlockSpec((GROUP, None, TB, TB),
                                     lambda bi, kv, i, jj: (tab_i * N_KV + kv, i - jmap(i, jj), 0, 0)))
        args.append(tab)
    if sel is not None:
        if sel_per_kv:
            in_specs.append(pl.BlockSpec((None, 1, TB, LANE), lambda bi, kv, i, jj: (bi, kv, i, 0)))
        else:
            in_specs.append(pl.BlockSpec((None, GROUP, TB, LANE), lambda bi, kv, i, jj: (bi, kv, i, 0)))
        in_specs.append(pl.BlockSpec((None, LANE, TB), lambda bi, kv, i, jj: (jmap(i, jj), 0, 0)))
        args += [sel, sel_e]
    if gates is not None:
        in_specs.append(pl.BlockSpec((None, None, GROUP, TB, 1), lambda bi, kv, i, jj: (bi, gate_i, kv, i, 0)))
        args.append(gates)
    if addend is not None:
        in_specs.append(pl.BlockSpec((None, TB, gw), lambda bi, kv, i, jj: (bi, i, kv)))
        args.append(addend)
    lam_init = 0.0
    if diff is not None:
        dl, gn, lam_init = diff
        in_specs += [pl.BlockSpec((4, DIFF_DH), lambda bi, kv, i, jj: (0, 0)),
                     pl.BlockSpec((1, HEAD_DIM), lambda bi, kv, i, jj: (0, 0))]
        args += [dl, gn]
    n_maps = 2 if diff is not None else 1
    scratch = [pltpu.VMEM((n_maps, GROUP, TB, 1), F32), pltpu.VMEM((n_maps, GROUP, TB, 1), F32),
               pltpu.VMEM((n_maps, GROUP, TB, HEAD_DIM), F32)]
    kern = functools.partial(_flash_kernel, nq=nq, band=band, fox=fq is not None, sel=sel is not None,
                             sel_shared=sel_per_kv, diff=diff is not None, gated=gates is not None,
                             addend=addend is not None, lam_init=lam_init)
    return pl.pallas_call(
        kern, grid=(b, N_KV, nq, nkk), in_specs=in_specs,
        out_specs=pl.BlockSpec((None, TB, gw), lambda bi, kv, i, jj: (bi, i, kv)),
        out_shape=jax.ShapeDtypeStruct((b, t, MIX_W), out_dtype),
        scratch_shapes=scratch, compiler_params=_cparams(4), name="flash_prefill",
    )(*args)


def _logsig_kernel(x_ref, o_ref):
    x = x_ref[...]
    o_ref[...] = jnp.minimum(x, 0.0) - jnp.log(1.0 + jnp.exp(-jnp.abs(x)))


def _log_sigmoid(x):
    b, h, t = x.shape
    return pl.pallas_call(
        _logsig_kernel, grid=(b,),
        in_specs=[pl.BlockSpec((None, h, t), lambda i: (i, 0, 0))],
        out_specs=pl.BlockSpec((None, h, t), lambda i: (i, 0, 0)),
        out_shape=jax.ShapeDtypeStruct((b, h, t), F32), compiler_params=_cparams(1), name="log_sigmoid",
    )(x)


CS_CHUNK = 512


def _cumsum_kernel(x_ref, o_ref, carry_ref):
    @pl.when(pl.program_id(1) == 0)
    def _():
        carry_ref[...] = jnp.zeros_like(carry_ref)
    r = lax.broadcasted_iota(jnp.int32, (CS_CHUNK, CS_CHUNK), 0)
    c = lax.broadcasted_iota(jnp.int32, (CS_CHUNK, CS_CHUNK), 1)
    tri = (r <= c).astype(F32)
    y = _dot_hi(x_ref[...], tri) + carry_ref[...]
    o_ref[...] = y
    carry_ref[...] = y[:, CS_CHUNK - 1:CS_CHUNK]


def _cumsum(x):
    b, h, n = x.shape
    return pl.pallas_call(
        _cumsum_kernel, grid=(b, n // CS_CHUNK),
        in_specs=[pl.BlockSpec((None, h, CS_CHUNK), lambda i, j: (i, 0, j))],
        out_specs=pl.BlockSpec((None, h, CS_CHUNK), lambda i, j: (i, 0, j)),
        out_shape=jax.ShapeDtypeStruct((b, h, n), F32),
        scratch_shapes=[pltpu.VMEM((h, 1), F32)], compiler_params=_cparams(2), name="cumsum",
    )(x)


def _moba_gate_kernel(q_ref, k_ref, sel_ref, *, t):
    nblk = t // MOBA_BLOCK
    r = lax.broadcasted_iota(jnp.int32, (LANE, t), 0)
    c = lax.broadcasted_iota(jnp.int32, (LANE, t), 1)
    avg = jnp.where(c // MOBA_BLOCK == r, 1.0 / MOBA_BLOCK, 0.0)
    kmean = _dot_hi(avg, k_ref[...])
    gs = _dot_nt(q_ref[...].astype(BF), kmean.astype(BF))
    lane = lax.broadcasted_iota(jnp.int32, gs.shape, 1)
    own = lax.broadcasted_iota(jnp.int32, gs.shape, 0) // MOBA_BLOCK
    vals = jnp.where(lane < own, gs, NEG)
    sel = (lane == own).astype(F32)
    for mx, idx in _topk_lanes(vals, lane, min(MOBA_TOPK, nblk), -3e38):
        sel = jnp.where((lane.astype(F32) == idx) & (mx > NEG_TEST), 1.0, sel)
    sel_ref[...] = sel


def _moba_gate(proj):
    b, t, _ = proj.shape
    return pl.pallas_call(
        functools.partial(_moba_gate_kernel, t=t), grid=(b, N_HEADS),
        in_specs=[pl.BlockSpec((None, t, LANE), lambda bi, h: (bi, 0, QM + h)),
                  pl.BlockSpec((None, t, LANE), lambda bi, h: (bi, 0, KVM + h // GROUP))],
        out_specs=pl.BlockSpec((None, None, t, LANE), lambda bi, h: (bi, h, 0, 0)),
        out_shape=jax.ShapeDtypeStruct((b, N_HEADS, t, LANE), F32),
        compiler_params=_cparams(2), name="moba_gate",
    )(proj, proj)


def _compress_kernel(c_ref, w_ref, o_ref):
    o_ref[...] = _dot(c_ref[...].astype(BF), w_ref[...].astype(BF))


def _nsa_compress(chunks, w_cmp, l):
    b, _, _, n, sd = chunks.shape
    rt = min(n, 512)
    r_len = NSA_CMP_LEN // NSA_CMP_STRIDE
    w = w_cmp.reshape(DEPTH, 2, r_len, sd, HEAD_DIM)
    y = pl.pallas_call(
        _compress_kernel, grid=(b, 2, N_KV, r_len, n // rt),
        in_specs=[pl.BlockSpec((None, None, None, rt, sd), lambda bi, j, k, r, i: (bi, j, k, i, 0)),
                  pl.BlockSpec((None, None, None, sd, HEAD_DIM), lambda bi, j, k, r, i: (l, j, r, 0, 0))],
        out_specs=pl.BlockSpec((None, None, None, None, rt, HEAD_DIM), lambda bi, j, k, r, i: (bi, j, k, r, i, 0)),
        out_shape=jax.ShapeDtypeStruct((b, 2, N_KV, r_len, n, HEAD_DIM), F32),
        compiler_params=_cparams(5), name="nsa_compress",
    )(chunks, w)
    return _combine_halves(y)


def _combine_halves(y):
    return y[:, :, :, 0] + jnp.pad(y[:, :, :, 1, 1:], ((0, 0), (0, 0), (0, 0), (0, 1), (0, 0)))


def _compress_paged_kernel(*refs, pp):
    _pt_ref = refs[0]
    page_refs = refs[1:1 + pp]
    w_ref, o_ref = refs[1 + pp], refs[2 + pp]
    cpp = PAGE_SIZE // NSA_CMP_STRIDE
    chunk_rows = NSA_CMP_STRIDE * 2 * N_KV
    r_len = NSA_CMP_LEN // NSA_CMP_STRIDE
    for j in range(2):
        for kv in range(N_KV):
            acc = [jnp.zeros((pp * cpp, HEAD_DIM), F32) for _ in range(r_len)]
            for s in range(NSA_CMP_STRIDE):
                start = s * 2 * N_KV + j * N_KV + kv
                xs = jnp.concatenate([ref[pl.ds(start, cpp, stride=chunk_rows), :] for ref in page_refs], axis=0)
                xs = xs.astype(BF)
                for r in range(r_len):
                    acc[r] = acc[r] + _dot(xs, w_ref[j, r * NSA_CMP_STRIDE + s].astype(BF))
            for r in range(r_len):
                o_ref[j, kv, r] = acc[r]


def _nsa_compress_paged(pool, l, page_table, w_cmp, pp):
    b, n_pages = page_table.shape
    cpp = PAGE_SIZE // NSA_CMP_STRIDE
    r_len = NSA_CMP_LEN // NSA_CMP_STRIDE
    page_rows = PAGE_SIZE * 2 * N_KV
    pool2 = pool.reshape(pool.shape[0], pool.shape[1], page_rows, HEAD_DIM)
    page_specs = [pl.BlockSpec((None, None, page_rows, HEAD_DIM), (lambda bi, s, pt, i=i: (l, pt[bi, s * pp + i], 0, 0)))
                  for i in range(pp)]
    y = pl.pallas_call(
        functools.partial(_compress_paged_kernel, pp=pp),
        grid_spec=pltpu.PrefetchScalarGridSpec(
            num_scalar_prefetch=1, grid=(b, n_pages // pp),
            in_specs=page_specs + [pl.BlockSpec((None, 2, NSA_CMP_LEN, HEAD_DIM, HEAD_DIM), lambda bi, s, pt: (l, 0, 0, 0, 0))],
            out_specs=pl.BlockSpec((None, 2, N_KV, r_len, pp * cpp, HEAD_DIM), lambda bi, s, pt: (bi, 0, 0, 0, s, 0))),
        out_shape=jax.ShapeDtypeStruct((b, 2, N_KV, r_len, n_pages * cpp, HEAD_DIM), F32),
        compiler_params=_cparams(2), name="nsa_compress_paged",
    )(page_table, *([pool2] * pp), w_cmp)
    return _combine_halves(y)


def _nsa_cmp_kernel(q_ref, kc_ref, vc_ref, tab_ref, cover_ref, g_ref, o_ref, sel_ref, *, n_cmp, n_slc):
    i = pl.program_id(2)
    shape = (TB, LANE)
    qpos = i * TB + lax.broadcasted_iota(jnp.int32, shape, 0)
    lane = lax.broadcasted_iota(jnp.int32, shape, 1)
    maskc = (lane * NSA_CMP_STRIDE + (NSA_CMP_LEN - 1) <= qpos) & (lane < n_cmp)
    kc = kc_ref[...].astype(BF)
    vc = vc_ref[...].astype(BF)
    imp = jnp.zeros(shape, F32)
    for g in range(GROUP):
        q = q_ref[:, g * HEAD_DIM:(g + 1) * HEAD_DIM].astype(BF)
        s = jnp.where(maskc, _dot_nt(q, kc) * SCALE + tab_ref[g], NEG)
        mx = jnp.max(s, axis=-1, keepdims=True)
        p = jnp.where(maskc, jnp.exp(s - mx), 0.0)
        z = jnp.sum(p, axis=-1, keepdims=True)
        pc = (p / jnp.where(z > 0, z, 1.0)).astype(BF)
        o_ref[:, g * HEAD_DIM:(g + 1) * HEAD_DIM] = _sigmoid(g_ref[g]) * _dot(pc, vc)
        imp = imp + _dot(pc, cover_ref[...].astype(BF))
    cur = qpos // NSA_SLC_BLOCK
    forced = (lane == 0) | ((lane <= cur) & (lane > cur - NSA_N_LOCAL))
    vals = jnp.where(forced, NSA_FORCE, jnp.where(lane <= cur, imp, -1.0))
    vals = jnp.where(lane < n_slc, vals, NEG)
    sel = jnp.zeros(shape, F32)
    for mx, idx in _topk_lanes(vals, lane, min(NSA_TOPN, n_slc), -3e38):
        sel = jnp.where((lane.astype(F32) == idx) & (mx >= 0.0), 1.0, sel)
    sel_ref[...] = sel


def _nsa_cmp_prefill(proj, kvc, tabc, cover, gates, n_cmp, n_slc):
    b, t, _ = proj.shape
    nq = t // TB
    return pl.pallas_call(
        functools.partial(_nsa_cmp_kernel, n_cmp=n_cmp, n_slc=n_slc), grid=(b, N_KV, nq),
        in_specs=[pl.BlockSpec((None, TB, GROUP * LANE), lambda bi, k, i: (bi, i, QN // GROUP + k)),
                  pl.BlockSpec((None, None, None, LANE, HEAD_DIM), lambda bi, k, i: (bi, 0, k, 0, 0)),
                  pl.BlockSpec((None, None, None, LANE, HEAD_DIM), lambda bi, k, i: (bi, 1, k, 0, 0)),
                  pl.BlockSpec((GROUP, TB, LANE), lambda bi, k, i: (k, i, 0)),
                  pl.BlockSpec((LANE, LANE), lambda bi, k, i: (0, 0)),
                  pl.BlockSpec((None, None, GROUP, TB, 1), lambda bi, k, i: (bi, 0, k, i, 0))],
        out_specs=[pl.BlockSpec((None, TB, GROUP * LANE), lambda bi, k, i: (bi, i, k)),
                   pl.BlockSpec((None, None, TB, LANE), lambda bi, k, i: (bi, k, i, 0))],
        out_shape=[jax.ShapeDtypeStruct((b, t, MIX_W), F32), jax.ShapeDtypeStruct((b, N_KV, t, LANE), F32)],
        compiler_params=_cparams(3), name="nsa_cmp_prefill",
    )(proj, kvc, kvc, tabc, cover, gates)


def _cover_matrix(n_cmp, n_slc):
    c_start = np.arange(n_cmp) * NSA_CMP_STRIDE
    s_ids = np.arange(n_slc)
    return ((c_start[:, None] < (s_ids[None, :] + 1) * NSA_SLC_BLOCK)
            & (c_start[:, None] + NSA_CMP_LEN > s_ids[None, :] * NSA_SLC_BLOCK)).astype(np.float32)


def _sel_expand(n_tiles, block):
    j = np.arange(n_tiles)[:, None, None]
    m = np.arange(LANE)[None, :, None]
    c = np.arange(TB)[None, None, :]
    return jnp.asarray(m == (j * TB + c) // block, BF)


def _mixers_prompt(l, h, P, tabs):
    b, t, d = h.shape
    n = b * t
    proj = _mm(h.reshape(n, d), P["w_in_r"], P["b_in_r"], l).reshape(b, t, PROJ_W)
    misc = proj[:, :, MISC * LANE:MISC * LANE + N_HEADS + 3 * N_HEADS]
    logf = _log_sigmoid(jnp.swapaxes(misc[..., :N_HEADS], 1, 2))
    gates = jnp.transpose(misc[..., N_HEADS:].reshape(b, t, 3, N_HEADS), (0, 2, 3, 1))[..., None]
    kvs = lambda c: proj[:, :, c * LANE:(c + 4) * LANE].reshape(b, t, 2, N_KV, HEAD_DIM)
    rows = dict(fox_kv=kvs(KVF), fox_logf=jnp.swapaxes(logf, 1, 2), diff_kv=kvs(KVD), moba_kv=kvs(KVM),
                nsa_cmp_kv=kvs(KVC), nsa_slc_kv=kvs(KVS))
    win = kvs(KVW)
    rows["nsa_win_kv"] = win[:, -min(NSA_WINDOW, t):]

    f_cum = _cumsum(logf)
    o_fox = _flash_prefill(proj, QF, KVF, fq=f_cum[..., None], fk=f_cum[:, :, None, :])

    lam_init = 0.8 - 0.6 * math.exp(-0.3 * l)
    o_diff = _flash_prefill(proj, QD, KVD, tab=tabs, tab_i=0,
                            diff=(P["diff_lambda"][l], P["diff_norm_g"][l].reshape(1, HEAD_DIM), lam_init))

    nk = t // TB
    o_moba = _flash_prefill(proj, QM, KVM, tab=tabs, tab_i=1, sel=_moba_gate(proj), sel_e=_sel_expand(nk, MOBA_BLOCK))

    n_chunk = t // NSA_CMP_STRIDE
    n_cmp = n_chunk - NSA_CMP_LEN // NSA_CMP_STRIDE + 1
    n_slc = -(-t // NSA_SLC_BLOCK)
    chunks = jnp.transpose(rows["nsa_cmp_kv"], (0, 2, 3, 1, 4)).reshape(b, 2, N_KV, n_chunk, NSA_CMP_STRIDE * HEAD_DIM)
    kvc = _nsa_compress(chunks, P["w_cmp"], l)
    cmp_end = jnp.arange(LANE) * NSA_CMP_STRIDE + (NSA_CMP_LEN - 1)
    tabc = _bias_lookup(P["rel_bias"][:, 2], _t5_bucket(jnp.arange(t)[:, None] - cmp_end[None, :]))
    cover = jnp.asarray(np.pad(_cover_matrix(n_cmp, n_slc), ((0, LANE - n_cmp), (0, LANE - n_slc))))
    o1, sel_n = _nsa_cmp_prefill(proj, kvc, tabc, cover, gates, n_cmp, n_slc)
    o2 = _flash_prefill(proj, QN, KVS, tab=tabs, tab_i=2, sel=sel_n, sel_e=_sel_expand(nk, NSA_SLC_BLOCK), sel_per_kv=True,
                        gates=gates, gate_i=1, addend=o1, out_dtype=F32)
    o_nsa = _flash_prefill(proj, QN, KVW, tab=tabs, tab_i=2, band=True, gates=gates, gate_i=2, addend=o2)
    o_all = jnp.stack([o_fox, o_diff, o_moba, o_nsa]).reshape(4, n, MIX_W)
    return o_all, rows


def _decode_kernel(*refs, pp, ns, diff, lam_init):
    it = iter(refs)
    _pt_ref = next(it)
    q_ref = next(it)
    page_refs = [next(it) for _ in range(pp)]
    new_ref, bias_ref = next(it), next(it)
    if diff:
        dl_ref, gn_ref = next(it), next(it)
    o_ref = next(it)
    m_ref, l_ref, acc_ref = next(it), next(it), next(it)
    n_maps = 2 if diff else 1
    s_id = pl.program_id(1)

    @pl.when(s_id == 0)
    def _():
        m_ref[...] = jnp.full_like(m_ref, NEG)
        l_ref[...] = jnp.zeros_like(l_ref)
        acc_ref[...] = jnp.zeros_like(acc_ref)

    def process(page_list):
        n_keys = len(page_list) * PAGE_SIZE
        for kv in range(N_KV):
            rows_of = lambda ref, j: ref[pl.ds(j * N_KV + kv, PAGE_SIZE, stride=2 * N_KV), :]
            k = jnp.concatenate([rows_of(ref, 0) for ref in page_list], axis=0).astype(BF)
            v = jnp.concatenate([rows_of(ref, 1) for ref in page_list], axis=0).astype(BF)
            bias = bias_ref[kv, :, :n_keys]
            mask = bias > NEG_TEST
            q = q_ref[kv]
            for mi in range(n_maps):
                if diff:
                    lane = lax.broadcasted_iota(jnp.int32, q.shape, 1)
                    half = (lane < DIFF_DH) if mi == 0 else (lane >= DIFF_DH)
                    s = _dot_nt(jnp.where(half, q * DIFF_SCALE, 0.0).astype(BF), k) + bias
                else:
                    s = _dot_nt(q.astype(BF), k) * SCALE + bias
                _online_update(s, mask, v, m_ref.at[mi, kv], l_ref.at[mi, kv], acc_ref.at[mi, kv])

    @pl.when(s_id < ns)
    def _():
        process(page_refs)

    @pl.when(s_id == ns)
    def _():
        process([new_ref])
        for kv in range(N_KV):
            o = _normalized(l_ref.at[0, kv], acc_ref.at[0, kv])
            if diff:
                o = o - _lam_of(dl_ref, lam_init) * _normalized(l_ref.at[1, kv], acc_ref.at[1, kv])
                o = o * lax.rsqrt(jnp.mean(o * o, axis=-1, keepdims=True) + LN_EPS) * gn_ref[...] * (1.0 - lam_init)
            o_ref[kv] = o


def _decode_attn(q, pool, l, page_table, new_kv, bias, pp, diff=None):
    b, _, r, _ = q.shape
    n_pages = page_table.shape[1]
    ns = n_pages // pp
    bb = bias.shape[0]
    page_rows = PAGE_SIZE * 2 * N_KV
    pool2 = pool.reshape(pool.shape[0], pool.shape[1], page_rows, HEAD_DIM)
    blk = (None, None, page_rows, HEAD_DIM)
    page_specs = [pl.BlockSpec(blk, (lambda bi, s, pt, i=i: (l, pt[bi, jnp.minimum(s, ns - 1) * pp + i], 0, 0)))
                  for i in range(pp)]
    in_specs = ([pl.BlockSpec((None, N_KV, r, HEAD_DIM), lambda bi, s, pt: (bi, 0, 0, 0))] + page_specs +
                [pl.BlockSpec((None, page_rows, HEAD_DIM), lambda bi, s, pt: (bi, 0, 0)),
                 pl.BlockSpec((None, N_KV, r, pp * PAGE_SIZE), lambda bi, s, pt: (bi if bb > 1 else 0, 0, 0, s))])
    args = [q] + [pool2] * pp + [new_kv.reshape(b, page_rows, HEAD_DIM), bias]
    lam_init = 0.0
    if diff is not None:
        dl, gn, lam_init = diff
        in_specs += [pl.BlockSpec((4, DIFF_DH), lambda bi, s, pt: (0, 0)),
                     pl.BlockSpec((1, HEAD_DIM), lambda bi, s, pt: (0, 0))]
        args += [dl, gn]
    n_maps = 2 if diff is not None else 1
    return pl.pallas_call(
        functools.partial(_decode_kernel, pp=pp, ns=ns, diff=diff is not None, lam_init=lam_init),
        grid_spec=pltpu.PrefetchScalarGridSpec(
            num_scalar_prefetch=1, grid=(b, ns + 1), in_specs=in_specs,
            out_specs=pl.BlockSpec((None, N_KV, r, HEAD_DIM), lambda bi, s, pt: (bi, 0, 0, 0)),
            scratch_shapes=[pltpu.VMEM((n_maps, N_KV, r, 1), F32), pltpu.VMEM((n_maps, N_KV, r, 1), F32),
                            pltpu.VMEM((n_maps, N_KV, r, HEAD_DIM), F32)]),
        out_shape=jax.ShapeDtypeStruct((b, N_KV, r, HEAD_DIM), F32),
        compiler_params=_cparams(2), name="decode_attn",
    )(page_table, *args)


def _page_sum_kernel(*refs, pp):
    _pt_ref = refs[0]
    page_refs = refs[1:1 + pp]
    o_ref = refs[1 + pp]
    for i, ref in enumerate(page_refs):
        for kv in range(N_KV):
            o_ref[i, kv:kv + 1, :] = jnp.sum(ref[:, kv, :], axis=0, keepdims=True)


def _page_key_sums(pool, l, page_table, pp):
    b, n_pages = page_table.shape
    blk = (None, None, PAGE_SIZE, None, N_KV, HEAD_DIM)
    page_specs = [pl.BlockSpec(blk, (lambda bi, s, pt, i=i: (l, pt[bi, s * pp + i], 0, 0, 0, 0))) for i in range(pp)]
    return pl.pallas_call(
        functools.partial(_page_sum_kernel, pp=pp),
        grid_spec=pltpu.PrefetchScalarGridSpec(
            num_scalar_prefetch=1, grid=(b, n_pages // pp), in_specs=page_specs,
            out_specs=pl.BlockSpec((None, pp, N_KV, HEAD_DIM), lambda bi, s, pt: (bi, s, 0, 0))),
        out_shape=jax.ShapeDtypeStruct((b, n_pages, N_KV, HEAD_DIM), F32),
        compiler_params=_cparams(2), name="page_key_sums",
    )(page_table, *([pool] * pp))


def _cmp_decode_kernel(q_ref, kvc_ref, bias_ref, o_ref, p_ref):
    for kv in range(N_KV):
        q = q_ref[kv].astype(BF)
        bias = bias_ref[kv]
        mask = bias > NEG_TEST
        s = jnp.where(mask, _dot_nt(q, kvc_ref[0, kv].astype(BF)) * SCALE + bias, NEG)
        mx = jnp.max(s, axis=-1, keepdims=True)
        p = jnp.where(mask, jnp.exp(s - mx), 0.0)
        z = jnp.sum(p, axis=-1, keepdims=True)
        pc = p / jnp.where(z > 0, z, 1.0)
        p_ref[kv] = pc
        o_ref[kv] = _dot(pc.astype(BF), kvc_ref[1, kv].astype(BF))


def _cmp_decode(q, kvc, bias):
    b, _, r, _ = q.shape
    n = kvc.shape[3]
    return pl.pallas_call(
        _cmp_decode_kernel, grid=(b,),
        in_specs=[pl.BlockSpec((None, N_KV, r, HEAD_DIM), lambda bi: (bi, 0, 0, 0)),
                  pl.BlockSpec((None, 2, N_KV, n, HEAD_DIM), lambda bi: (bi, 0, 0, 0, 0)),
                  pl.BlockSpec((N_KV, r, n), lambda bi: (0, 0, 0))],
        out_specs=[pl.BlockSpec((None, N_KV, r, HEAD_DIM), lambda bi: (bi, 0, 0, 0)),
                   pl.BlockSpec((None, N_KV, r, n), lambda bi: (bi, 0, 0, 0))],
        out_shape=[jax.ShapeDtypeStruct((b, N_KV, r, HEAD_DIM), F32), jax.ShapeDtypeStruct((b, N_KV, r, n), F32)],
        compiler_params=_cparams(1), name="cmp_decode",
    )(q, kvc, bias)


def _rows_of(x, t):
    b = x.shape[0]
    return jnp.transpose(x.reshape(b, t, N_KV, GROUP, HEAD_DIM), (0, 2, 3, 1, 4)).reshape(b, N_KV, GROUP * t, HEAD_DIM)


def _unrows(o, t):
    b = o.shape[0]
    return jnp.transpose(o.reshape(b, N_KV, GROUP, t, HEAD_DIM), (0, 3, 1, 2, 4)).reshape(b, t, MIX_W)


def _head_rows(a, t):
    b = a.shape[0]
    return a.reshape(b, N_KV, GROUP * t, a.shape[-1])


DEC_PP = 16


def _sample_tables(rel_bias, t, n_pages, w_len, past_len):
    lkp = (n_pages // DEC_PP + 1) * DEC_PP * PAGE_SIZE
    qpos = past_len + jnp.arange(t)
    kpos = jnp.arange(lkp)
    kvalid = (kpos[None, :] <= qpos[:, None]) & (kpos[None, :] < past_len + t)
    t5 = _bias_lookup(rel_bias.reshape(N_BUCKETS, 3 * N_HEADS), _t5_bucket(qpos[:, None] - kpos[None, :]))
    t5 = jnp.where(kvalid[None, None], t5.reshape(3, N_HEADS, t, lkp), NEG)
    n_chunk = (past_len + t) // NSA_CMP_STRIDE
    n_cmp = n_chunk - NSA_CMP_LEN // NSA_CMP_STRIDE + 1
    cmp_end = jnp.arange(n_chunk) * NSA_CMP_STRIDE + (NSA_CMP_LEN - 1)
    cvalid = (cmp_end[None, :] <= qpos[:, None]) & (jnp.arange(n_chunk)[None, :] < n_cmp)
    bias_c = _bias_lookup(rel_bias[:, 2], _t5_bucket(qpos[:, None] - cmp_end[None, :]))
    bias_c = jnp.where(cvalid[None], bias_c, NEG).reshape(N_KV, GROUP * t, n_chunk)
    wp = w_len // PAGE_SIZE
    lkw = 2 * wp * PAGE_SIZE
    kidx = jnp.arange(lkw)
    dw = (w_len + jnp.arange(t))[:, None] - kidx[None, :]
    wvalid = (dw >= 0) & (dw <= NSA_WINDOW) & (kidx[None, :] < w_len + t)
    bias_w = _bias_lookup(rel_bias[:, 2], _t5_bucket(dw))
    bias_w = jnp.where(wvalid[None], bias_w, NEG).reshape(1, N_KV, GROUP * t, lkw)
    return dict(kvalid=kvalid, t5=t5, bias_c=bias_c, bias_w=bias_w)


def _mixers_sample(l, h, P, caches, page_table, past_len, tables):
    b, t, d = h.shape
    n = b * t
    n_pages = page_table.shape[1]
    pp = DEC_PP
    ns = n_pages // pp
    lkp = (ns + 1) * pp * PAGE_SIZE
    proj = _mm(h.reshape(n, d), P["w_in_r"], P["b_in_r"], l).reshape(b, t, PROJ_W)
    seg = lambda c, w: proj[:, :, c * LANE:(c + w) * LANE]
    misc = proj[:, :, MISC * LANE:MISC * LANE + 4 * N_HEADS]
    kvs = lambda c: seg(c, 4).reshape(b, t, 2, N_KV, HEAD_DIM)
    pad_new = lambda kv: jnp.pad(kv, ((0, 0), (0, PAGE_SIZE - t), (0, 0), (0, 0), (0, 0)))
    logf_new = _log_sigmoid(jnp.pad(jnp.swapaxes(misc[..., :N_HEADS], 1, 2), ((0, 0), (0, 0), (0, LANE - t))))[..., :t]
    rows = dict(fox_kv=kvs(KVF), fox_logf=jnp.swapaxes(logf_new, 1, 2), diff_kv=kvs(KVD), moba_kv=kvs(KVM),
                nsa_cmp_kv=kvs(KVC), nsa_slc_kv=kvs(KVS))
    win_new = kvs(KVW)
    win_past = caches["nsa_win_kv"][l]
    w_len = win_past.shape[1]
    rows["nsa_win_kv"] = jnp.concatenate([win_past, win_new], axis=1)[:, -min(NSA_WINDOW, w_len + t):]

    qpos = past_len + jnp.arange(t)
    kpos = jnp.arange(lkp)
    kvalid = tables["kvalid"]
    t5_bias = lambda ti: tables["t5"][ti]

    logf_past = caches["fox_logf"][l][page_table].reshape(b, past_len, N_HEADS)
    lf = jnp.concatenate([jnp.swapaxes(logf_past, 1, 2), logf_new], axis=2)
    lf = jnp.pad(lf, ((0, 0), (0, 0), (0, lkp - past_len - t)))
    f_cum = _cumsum(lf)
    f_q = f_cum[:, :, past_len:past_len + t]
    bias_fox = jnp.where(kvalid[None, None], f_q[..., None] - f_cum[:, :, None, :], NEG)
    o_fox = _decode_attn(_rows_of(seg(QF, 8), t), caches["fox_kv"], l, page_table, pad_new(rows["fox_kv"]),
                         _head_rows(bias_fox, t), pp)

    lam_init = 0.8 - 0.6 * math.exp(-0.3 * l)
    o_diff = _decode_attn(_rows_of(seg(QD, 8), t), caches["diff_kv"], l, page_table, pad_new(rows["diff_kv"]),
                          _head_rows(t5_bias(0)[None], t), pp,
                          diff=(P["diff_lambda"][l], P["diff_norm_g"][l].reshape(1, HEAD_DIM), lam_init))

    lk = past_len + t
    nblk = -(-lk // MOBA_BLOCK)
    ppb = MOBA_BLOCK // PAGE_SIZE
    psum = _page_key_sums(caches["moba_kv"], l, page_table, pp)
    kmean = psum.reshape(b, n_pages // ppb, ppb, N_KV, HEAD_DIM).sum(2) / MOBA_BLOCK
    q_m = seg(QM, 8).reshape(b, t, N_KV, GROUP, HEAD_DIM)
    gs = jnp.einsum("btkgd,bmkd->bkgtm", q_m, kmean)
    own = qpos // MOBA_BLOCK
    blk_ids = jnp.arange(n_pages // ppb)
    gs = jnp.where(blk_ids[None, :] < own[:, None], gs, -jnp.inf)
    top_v, top_i = lax.top_k(gs, min(MOBA_TOPK, nblk))
    kblk = kpos // MOBA_BLOCK
    in_chosen = jnp.any((top_i[..., None] == kblk) & jnp.isfinite(top_v)[..., None], axis=-2)
    keep = in_chosen | (kblk[None, :] == own[:, None])
    bias_moba = jnp.where(keep, t5_bias(1).reshape(N_KV, GROUP, t, lkp)[None], NEG)
    o_moba = _decode_attn(_rows_of(seg(QM, 8), t), caches["moba_kv"], l, page_table, pad_new(rows["moba_kv"]),
                          bias_moba.reshape(b, N_KV, GROUP * t, lkp), pp)

    n_chunk = lk // NSA_CMP_STRIDE
    r_len = NSA_CMP_LEN // NSA_CMP_STRIDE
    n_cmp = n_chunk - r_len + 1
    n_slc = -(-lk // NSA_SLC_BLOCK)
    assert t < NSA_CMP_STRIDE and past_len % NSA_CMP_STRIDE == 0
    kvc = _nsa_compress_paged(caches["nsa_cmp_kv"], l, page_table, P["w_cmp"], pp)
    q_n = _rows_of(seg(QN, 8), t)
    o_cmp, pc = _cmp_decode(q_n, kvc, tables["bias_c"])
    cover = jnp.asarray(_cover_matrix(n_cmp, n_slc))
    imp = jnp.einsum("bkgtm,mj->bktj", pc.reshape(b, N_KV, GROUP, t, n_chunk)[..., :n_cmp], cover)
    cur = (qpos // NSA_SLC_BLOCK)[:, None]
    sl_ids = jnp.arange(n_slc)
    forced = (sl_ids[None, :] == 0) | ((sl_ids[None, :] <= cur) & (sl_ids[None, :] > cur - NSA_N_LOCAL))
    imp = jnp.where(forced, NSA_FORCE, jnp.where(sl_ids[None, :] <= cur, imp, -1.0))
    top_v, top_i = lax.top_k(imp, min(NSA_TOPN, n_slc))
    kslc = jnp.minimum(kpos // NSA_SLC_BLOCK, n_slc - 1)
    keep_s = jnp.any((top_i[..., None] == kslc) & (top_v >= 0)[..., None], axis=-2)
    bias_t5n = t5_bias(2).reshape(N_KV, GROUP, t, lkp)
    bias_slc = jnp.where(keep_s[:, :, None], bias_t5n[None], NEG).reshape(b, N_KV, GROUP * t, lkp)
    o_slc = _decode_attn(q_n, caches["nsa_slc_kv"], l, page_table, pad_new(rows["nsa_slc_kv"]), bias_slc, pp)

    wp = w_len // PAGE_SIZE
    pool_w = caches["nsa_win_kv"].reshape(DEPTH, b * wp, PAGE_SIZE, 2, N_KV, HEAD_DIM)
    pt_w = (jnp.arange(b, dtype=jnp.int32)[:, None] * wp + jnp.arange(wp, dtype=jnp.int32)[None, :])
    o_win = _decode_attn(q_n, pool_w, l, pt_w, pad_new(win_new), tables["bias_w"], wp)

    g = _sigmoid(misc[..., N_HEADS:].reshape(b, t, 3, N_HEADS))[..., None]
    hd = lambda o: _unrows(o, t).reshape(b, t, N_HEADS, HEAD_DIM)
    o_nsa = (g[:, :, 0] * hd(o_cmp) + g[:, :, 1] * hd(o_slc) + g[:, :, 2] * hd(o_win)).reshape(b, t, MIX_W)
    o_all = jnp.stack([_unrows(o_fox, t), _unrows(o_diff, t), _unrows(o_moba, t), o_nsa]).astype(BF).reshape(4, n, MIX_W)
    return o_all, rows


STATE_NAMES = ("fox_kv", "fox_logf", "diff_kv", "moba_kv", "nsa_cmp_kv", "nsa_slc_kv", "nsa_win_kv")


def _run_group(x, mod, P, mixers):
    b, t, d = x.shape
    n = b * t
    rows = {name: [] for name in STATE_NAMES}
    h = _modulate(x, mod, 0)
    for l in range(DEPTH):
        o_all, new_rows = mixers(l, h)
        merged = _gate_merge(h.reshape(n, d), o_all, P["w_gate"], P["b_gate"], P["w_branch"], l)
        y = _mm(merged, P["w_o"], P["zero_bias"], l).reshape(b, t, d)
        x, h2, h2_packed = _ln_mod(x, y, None, mod, P["ln_g"], P["ln_b"], l, 0, l, 3)
        y_slots, gates = _moe(h2, h2_packed, P["w_router"], P["b_router"], P["w_gu"], P["b_gu"], P["w_dn"], P["b_dn"], l)
        x, h, _ = _ln_mod(x, y_slots, gates, mod, P["ln_g"], P["ln_b"], l, 1, l + 1 if l + 1 < DEPTH else None, 0)
        for name in STATE_NAMES:
            rows[name].append(new_rows[name])
    return x, {name: jnp.stack(rows[name]) for name in STATE_NAMES}


def kernel(x_prompt, x_sample, cache_fox_kv, cache_fox_logf, cache_diff_kv, cache_moba_kv, cache_nsa_cmp_kv,
           cache_nsa_slc_kv, cache_nsa_win_kv, page_table, c_prompt, c_sample, w_ada, b_ada, ln_g, ln_b, w_in, b_in,
           diff_lambda, diff_norm_g, rel_bias, w_cmp, w_gate, b_gate, w_branch, w_o, w_router, b_router,
           w_gu, b_gu, w_dn, b_dn):
    bp = x_prompt.shape[0]
    bs = x_sample.shape[0]
    past_len = page_table.shape[1] * PAGE_SIZE

    o_f = MIX_W + KV_W
    o_g = D_IN - 3 * N_HEADS

    def reorder(w):
        pad = jnp.zeros(w.shape[:-1] + (PROJ_W - D_IN,), w.dtype)
        return jnp.concatenate([w[..., :o_f], w[..., o_f + N_HEADS:o_g], w[..., o_f:o_f + N_HEADS], w[..., o_g:], pad], -1)

    P = dict(w_in_r=reorder(w_in), b_in_r=reorder(b_in).reshape(DEPTH, 1, PROJ_W), diff_lambda=diff_lambda,
             diff_norm_g=diff_norm_g, rel_bias=rel_bias, w_cmp=w_cmp, w_gate=w_gate, b_gate=b_gate, w_branch=w_branch,
             w_o=w_o, zero_bias=jnp.zeros((DEPTH, 1, D_MODEL), F32), w_router=w_router, b_router=b_router,
             w_gu=w_gu, b_gu=b_gu, w_dn=w_dn, b_dn=b_dn, ln_g=ln_g, ln_b=ln_b)

    r_pad = -(-(bp + bs) // 8) * 8
    c_all = jnp.pad(jnp.concatenate([c_prompt, c_sample], 0), ((0, r_pad - bp - bs), (0, 0)))
    mod_all = _ada_mod(c_all, w_ada, b_ada).reshape(DEPTH, r_pad, 6, D_MODEL)
    mod_p = mod_all[:, :bp]
    mod_s = mod_all[:, bp:bp + bs]

    n_off = x_prompt.shape[1] // TB
    tabs = _t5_tiles(rel_bias, n_off)
    y_prompt, sp = _run_group(x_prompt, mod_p, P, lambda l, h: _mixers_prompt(l, h, P, tabs))

    caches = dict(fox_kv=cache_fox_kv, fox_logf=cache_fox_logf, diff_kv=cache_diff_kv, moba_kv=cache_moba_kv,
                  nsa_cmp_kv=cache_nsa_cmp_kv, nsa_slc_kv=cache_nsa_slc_kv, nsa_win_kv=cache_nsa_win_kv)
    tables = _sample_tables(rel_bias, x_sample.shape[1], page_table.shape[1], cache_nsa_win_kv.shape[2], past_len)
    y_sample, ss = _run_group(x_sample, mod_s, P,
                              lambda l, h: _mixers_sample(l, h, P, caches, page_table, past_len, tables))
    return (y_prompt, y_sample,
            sp["fox_kv"], sp["fox_logf"], sp["diff_kv"], sp["moba_kv"], sp["nsa_cmp_kv"], sp["nsa_slc_kv"], sp["nsa_win_kv"],
            ss["fox_kv"], ss["fox_logf"], ss["diff_kv"], ss["moba_kv"], ss["nsa_cmp_kv"], ss["nsa_slc_kv"], ss["nsa_win_kv"])
```

```python
import functools
import math

import jax
import jax.numpy as jnp
import numpy as np
from jax import lax
from jax.experimental import pallas as pl
from jax.experimental.pallas import tpu as pltpu

D_MODEL = 4096
DEPTH = 2
PAGE_SIZE = 128
HEAD_DIM = 128
N_HEADS = 8
N_KV = 2
GROUP = N_HEADS // N_KV
MIX_W = N_HEADS * HEAD_DIM
KV_W = 2 * N_KV * HEAD_DIM
DIFF_DH = HEAD_DIM // 2
SCALE = HEAD_DIM ** -0.5
DIFF_SCALE = DIFF_DH ** -0.5
MOBA_BLOCK = 256
MOBA_TOPK = 3
NSA_CMP_LEN = 32
NSA_CMP_STRIDE = 16
NSA_SLC_BLOCK = 64
NSA_TOPN = 16
NSA_N_LOCAL = 2
NSA_WINDOW = 512
NSA_FORCE = 1e9
N_BUCKETS = 32
MAX_DISTANCE = 4096
N_EXPERTS = 32
TOP_K = 4
D_EXPERT = D_MODEL // 4
SWIGLU_ALPHA = 1.702
SWIGLU_LIMIT = 7.0
DN_ALPHA = (2 * DEPTH) ** 0.25
LN_EPS = 1e-5
IN_SPLITS = (MIX_W, KV_W, N_HEADS, MIX_W, KV_W, MIX_W, KV_W, MIX_W, KV_W, KV_W, KV_W, 3 * N_HEADS)
D_IN = sum(IN_SPLITS)

LANE = 128
QF, KVF, QD, KVD, QM, KVM, QN, KVC, KVS, KVW, MISC = 0, 8, 12, 20, 24, 32, 36, 44, 48, 52, 56
PROJ_W = 60 * LANE
NEG = -1e30
NEG_TEST = -1e29
TB = 512
VMEM_LIMIT = 56 * 1024 * 1024
BF = jnp.bfloat16
F32 = jnp.float32


def _cparams(n_axes):
    return pltpu.CompilerParams(dimension_semantics=("arbitrary",) * n_axes, vmem_limit_bytes=VMEM_LIMIT)


def _dot(a, b):
    return jnp.dot(a, b, preferred_element_type=F32)


def _dot_nt(a, b):
    return lax.dot_general(a, b, (((1,), (1,)), ((), ())), preferred_element_type=F32)


def _dot_hi(a, b):
    return jnp.dot(a, b, precision=lax.Precision.HIGHEST, preferred_element_type=F32)


def _sigmoid(x):
    return 1.0 / (1.0 + jnp.exp(-x))


def _cast_rows(src_ref, dst_ref, rows, chunk=256):
    def body(r, c):
        sl = pl.ds(pl.multiple_of(r * chunk, chunk), chunk)
        dst_ref[sl, :] = src_ref[sl, :].astype(BF)
        return c
    lax.fori_loop(0, rows // chunk, body, 0)


def _ada_kernel(c_ref, w_ref, b_ref, o_ref):
    c = c_ref[...]
    a = (c * _sigmoid(c)).astype(BF)
    o_ref[...] = _dot(a, w_ref[...].astype(BF)) + b_ref[...]


def _ada_mod(c_all, w_ada, b_ada):
    r = c_all.shape[0]
    tn = 512
    n6 = 6 * D_MODEL
    return pl.pallas_call(
        _ada_kernel,
        grid=(DEPTH, n6 // tn),
        in_specs=[pl.BlockSpec((r, D_MODEL), lambda l, n: (0, 0)),
                  pl.BlockSpec((None, D_MODEL, tn), lambda l, n: (l, 0, n)),
                  pl.BlockSpec((None, 1, tn), lambda l, n: (l, 0, n))],
        out_specs=pl.BlockSpec((None, r, tn), lambda l, n: (l, 0, n)),
        out_shape=jax.ShapeDtypeStruct((DEPTH, r, n6), F32),
        compiler_params=_cparams(2), name="ada_mod",
    )(c_all, w_ada, b_ada.reshape(DEPTH, 1, n6))


def _modulate_kernel(x_ref, m_ref, o_ref):
    o_ref[...] = (x_ref[...] * (1.0 + m_ref[1:2, :]) + m_ref[0:1, :]).astype(BF)


def _modulate(x, mod, l):
    b, t, d = x.shape
    tm = min(t, 512)
    return pl.pallas_call(
        _modulate_kernel,
        grid=(b, t // tm),
        in_specs=[pl.BlockSpec((None, tm, d), lambda i, j: (i, j, 0)),
                  pl.BlockSpec((None, None, 6, d), lambda i, j: (l, i, 0, 0))],
        out_specs=pl.BlockSpec((None, tm, d), lambda i, j: (i, j, 0)),
        out_shape=jax.ShapeDtypeStruct((b, t, d), BF),
        compiler_params=_cparams(2), name="modulate",
    )(x, mod)


def _mm_kernel(x_ref, w_ref, b_ref, o_ref, wbf_ref):
    @pl.when(pl.program_id(1) == 0)
    def _():
        _cast_rows(w_ref, wbf_ref, w_ref.shape[0])
    o_ref[...] = (_dot(x_ref[...], wbf_ref[...]) + b_ref[...]).astype(o_ref.dtype)


def _mm(x, w, bias, l, out_dtype=F32, tn=512):
    m, k = x.shape
    n = w.shape[-1]
    tm = min(m, 1024)
    return pl.pallas_call(
        _mm_kernel,
        grid=(n // tn, m // tm),
        in_specs=[pl.BlockSpec((tm, k), lambda j, i: (i, 0)),
                  pl.BlockSpec((None, k, tn), lambda j, i: (l, 0, j)),
                  pl.BlockSpec((None, 1, tn), lambda j, i: (l, 0, j))],
        out_specs=pl.BlockSpec((tm, tn), lambda j, i: (i, j)),
        out_shape=jax.ShapeDtypeStruct((m, n), out_dtype),
        scratch_shapes=[pltpu.VMEM((k, tn), BF)],
        compiler_params=_cparams(2), name="mm",
    )(x, w, bias)


def _gate_merge_kernel(h_ref, o_ref, wg_ref, bg_ref, wb_ref, out_ref, acc_ref, wgbf_ref, wbbf_ref):
    br = pl.program_id(2)

    @pl.when(br == 0)
    def _():
        acc_ref[...] = jnp.zeros_like(acc_ref)

    _cast_rows(wg_ref, wgbf_ref, wg_ref.shape[0])
    _cast_rows(wb_ref, wbbf_ref, wb_ref.shape[0])
    g = _dot(h_ref[...], wgbf_ref[...]) + bg_ref[...]
    u = _dot(o_ref[...], wbbf_ref[...])
    acc_ref[...] += _sigmoid(g) * u

    @pl.when(br == 3)
    def _():
        out_ref[...] = acc_ref[...].astype(out_ref.dtype)


def _gate_merge(h, o_all, w_gate, b_gate, w_branch, l):
    m = h.shape[0]
    tm = min(m, 1024)
    tn = 512
    return pl.pallas_call(
        _gate_merge_kernel,
        grid=(D_MODEL // tn, m // tm, 4),
        in_specs=[pl.BlockSpec((tm, D_MODEL), lambda j, i, b: (i, 0)),
                  pl.BlockSpec((None, tm, MIX_W), lambda j, i, b: (b, i, 0)),
                  pl.BlockSpec((None, None, D_MODEL, tn), lambda j, i, b: (l, b, 0, j)),
                  pl.BlockSpec((None, None, 1, tn), lambda j, i, b: (l, b, 0, j)),
                  pl.BlockSpec((None, None, MIX_W, tn), lambda j, i, b: (l, b, 0, j))],
        out_specs=pl.BlockSpec((tm, tn), lambda j, i, b: (i, j)),
        out_shape=jax.ShapeDtypeStruct((m, D_MODEL), BF),
        scratch_shapes=[pltpu.VMEM((tm, tn), F32), pltpu.VMEM((D_MODEL, tn), BF), pltpu.VMEM((MIX_W, tn), BF)],
        compiler_params=_cparams(3), name="gate_merge",
    )(h, o_all, w_gate, b_gate.reshape(DEPTH, 4, 1, D_MODEL), w_branch)


def _pack_halves(hb):
    u = lax.bitcast_convert_type(hb.astype(F32), jnp.uint32)
    w = hb.shape[-1] // 2
    return (u[:, :w] >> 16) | (u[:, w:] & jnp.uint32(0xFFFF0000))


def _unpack_halves(xw):
    lo = lax.bitcast_convert_type(xw << 16, F32).astype(BF)
    hi = lax.bitcast_convert_type(xw & jnp.uint32(0xFFFF0000), F32).astype(BF)
    return lo, hi


def _ln_kernel(*refs, n_slots, gate_row, shift_row, emit_h, emit_packed):
    it = iter(refs)
    x_ref, y_ref = next(it), next(it)
    wt_ref = next(it) if n_slots > 1 else None
    ma_ref, mb_ref, g_ref, b_ref = next(it), next(it), next(it), next(it)
    xo_ref = next(it)
    ho_ref = next(it) if emit_h else None
    po_ref = next(it) if emit_packed else None
    if n_slots > 1:
        y = y_ref[0] * wt_ref[:, 0:1]
        for s in range(1, n_slots):
            y = y + y_ref[s] * wt_ref[:, s:s + 1]
    else:
        y = y_ref[...]
    z = DN_ALPHA * x_ref[...] + (1.0 + ma_ref[gate_row:gate_row + 1, :]) * y
    mu = jnp.mean(z, axis=-1, keepdims=True)
    zc = z - mu
    var = jnp.mean(zc * zc, axis=-1, keepdims=True)
    xn = zc * lax.rsqrt(var + LN_EPS) * g_ref[...] + b_ref[...]
    xo_ref[...] = xn
    if emit_h:
        hb = (xn * (1.0 + mb_ref[shift_row + 1:shift_row + 2, :]) + mb_ref[shift_row:shift_row + 1, :]).astype(BF)
        ho_ref[...] = hb
        if emit_packed:
            po_ref[...] = _pack_halves(hb)


def _ln_mod(x, y, wts, mod, ln_g, ln_b, l, which, l_next, shift_row):
    b, t, d = x.shape
    n_slots = 1 if wts is None else wts.shape[2]
    tm = min(t, 256 if n_slots == 1 else 128)
    emit_h = l_next is not None
    emit_packed = emit_h and which == 0
    gate_row = 2 if which == 0 else 5
    ln_i = which
    in_specs = [pl.BlockSpec((None, tm, d), lambda i, j: (i, j, 0))]
    args = [x, y]
    if n_slots > 1:
        in_specs += [pl.BlockSpec((n_slots, None, tm, d), lambda i, j: (0, i, j, 0)),
                     pl.BlockSpec((None, tm, n_slots), lambda i, j: (i, j, 0))]
        args.append(wts)
    else:
        in_specs.append(pl.BlockSpec((None, tm, d), lambda i, j: (i, j, 0)))
    lb = l if l_next is None else l_next
    in_specs += [pl.BlockSpec((None, None, 6, d), lambda i, j: (l, i, 0, 0)),
                 pl.BlockSpec((None, None, 6, d), lambda i, j: (lb, i, 0, 0)),
                 pl.BlockSpec((None, None, 1, d), lambda i, j: (l, ln_i, 0, 0)),
                 pl.BlockSpec((None, None, 1, d), lambda i, j: (l, ln_i, 0, 0))]
    args += [mod, mod, ln_g.reshape(DEPTH, 2, 1, d), ln_b.reshape(DEPTH, 2, 1, d)]
    out_specs = [pl.BlockSpec((None, tm, d), lambda i, j: (i, j, 0))]
    out_shape = [jax.ShapeDtypeStruct((b, t, d), F32)]
    if emit_h:
        out_specs.append(pl.BlockSpec((None, tm, d), lambda i, j: (i, j, 0)))
        out_shape.append(jax.ShapeDtypeStruct((b, t, d), BF))
    if emit_packed:
        out_specs.append(pl.BlockSpec((None, tm, d // 2), lambda i, j: (i, j, 0)))
        out_shape.append(jax.ShapeDtypeStruct((b, t, d // 2), jnp.uint32))
    res = pl.pallas_call(
        functools.partial(_ln_kernel, n_slots=n_slots, gate_row=gate_row, shift_row=shift_row, emit_h=emit_h,
                          emit_packed=emit_packed),
        grid=(b, t // tm), in_specs=in_specs, out_specs=out_specs, out_shape=out_shape,
        compiler_params=_cparams(2), name="ln_mod",
    )(*args)
    return tuple(res) + (None,) * (3 - len(res))


def _gather_rows_kernel(idx_ref, live_ref, src_ref, out_ref, *scratch, tr, unpack):
    if unpack:
        dst_ref, sem = scratch
    else:
        dst_ref, (sem,) = out_ref, scratch

    def row_copy(r, src_row):
        return pltpu.make_async_copy(src_ref.at[pl.ds(src_row, 1)], dst_ref.at[pl.ds(r, 1)], sem)

    @pl.when(live_ref[0, 0] > 0)
    def _():
        def issue(r8, c):
            for u in range(8):
                r = r8 * 8 + u
                row_copy(r, idx_ref[0, r]).start(priority=u % 2)
            return c
        lax.fori_loop(0, tr // 8, issue, 0)

        def wait(r, c):
            row_copy(r, 0).wait()
            return c
        lax.fori_loop(0, tr, wait, 0, unroll=8)
        if unpack:
            w = dst_ref.shape[1]
            out_ref[:, :w], out_ref[:, w:] = _unpack_halves(dst_ref[...])

    @pl.when(live_ref[0, 0] == 0)
    def _():
        out_ref[...] = jnp.zeros_like(out_ref)


def _gather_rows(src, idx, live=None, unpack=False):
    n_out = idx.shape[0]
    w = src.shape[1]
    tr = next(c for c in (256, 128, 64, 32, 16, 8) if n_out % c == 0)
    nt = n_out // tr
    if live is None:
        live = jnp.ones((nt,), jnp.int32)
    out_w, out_dtype = (2 * w, BF) if unpack else (w, src.dtype)
    scratch = ([pltpu.VMEM((tr, w), src.dtype)] if unpack else []) + [pltpu.SemaphoreType.DMA(())]
    return pl.pallas_call(
        functools.partial(_gather_rows_kernel, tr=tr, unpack=unpack), grid=(nt,),
        in_specs=[pl.BlockSpec((None, 1, tr), lambda t: (t, 0, 0), memory_space=pltpu.SMEM),
                  pl.BlockSpec((None, 1, 1), lambda t: (t, 0, 0), memory_space=pltpu.SMEM),
                  pl.BlockSpec(memory_space=pl.ANY)],
        out_specs=pl.BlockSpec((tr, out_w), lambda t: (t, 0)),
        out_shape=jax.ShapeDtypeStruct((n_out, out_w), out_dtype),
        scratch_shapes=scratch,
        compiler_params=_cparams(1), name="gather_rows",
    )(idx.reshape(nt, 1, tr), live.reshape(nt, 1, 1), src)


def _topk_lanes(vals, lane, k, floor):
    picks = []
    lane_f = lane.astype(F32)
    for _ in range(k):
        mx = jnp.max(vals, axis=-1, keepdims=True)
        idx = jnp.min(jnp.where(vals == mx, lane_f, 4096.0), axis=-1, keepdims=True)
        picks.append((mx, idx))
        vals = jnp.where(lane_f == idx, floor, vals)
    return picks


def _router_kernel(x_ref, w_ref, b_ref, gate_ref, idx_ref):
    logits = _dot(x_ref[...], w_ref[...].astype(BF)) + b_ref[...]
    lane = lax.broadcasted_iota(jnp.int32, logits.shape, 1)
    vals = jnp.where(lane < N_EXPERTS, logits, NEG)
    picks = _topk_lanes(vals, lane, TOP_K, -3e38)
    v0 = picks[0][0]
    es = [jnp.exp(v - v0) for v, _ in picks]
    z = es[0] + es[1] + es[2] + es[3]
    gates = jnp.zeros(logits.shape, F32)
    idxs = jnp.zeros(logits.shape, F32)
    for k in range(TOP_K):
        gates = jnp.where(lane == k, es[k] / z, gates)
        idxs = jnp.where(lane == k, picks[k][1], idxs)
    gate_ref[...] = gates
    idx_ref[...] = idxs.astype(jnp.int32)


def _router(h2, w_router, b_router, l):
    m = h2.shape[0]
    tm = min(m, 512)
    w = jnp.pad(w_router, ((0, 0), (0, 0), (0, LANE - N_EXPERTS)))
    bb = jnp.pad(b_router, ((0, 0), (0, LANE - N_EXPERTS))).reshape(DEPTH, 1, LANE)
    return pl.pallas_call(
        _router_kernel,
        grid=(m // tm,),
        in_specs=[pl.BlockSpec((tm, D_MODEL), lambda i: (i, 0)),
                  pl.BlockSpec((None, D_MODEL, LANE), lambda i: (l, 0, 0)),
                  pl.BlockSpec((None, 1, LANE), lambda i: (l, 0, 0))],
        out_specs=[pl.BlockSpec((tm, LANE), lambda i: (i, 0)), pl.BlockSpec((tm, LANE), lambda i: (i, 0))],
        out_shape=[jax.ShapeDtypeStruct((m, LANE), F32), jax.ShapeDtypeStruct((m, LANE), jnp.int32)],
        compiler_params=_cparams(1), name="router",
    )(h2, w, bb)


def _new_expert(te_ref, t):
    return (t == 0) | (te_ref[t] != te_ref[jnp.maximum(t - 1, 0)])


def _moe_up_kernel(te_ref, tv_ref, x_ref, wg_ref, wu_ref, bg_ref, bu_ref, o_ref, wgbf_ref, wubf_ref):
    t = pl.program_id(1)
    live = tv_ref[t] > 0

    @pl.when(live & _new_expert(te_ref, t))
    def _():
        _cast_rows(wg_ref, wgbf_ref, wg_ref.shape[0])
        _cast_rows(wu_ref, wubf_ref, wu_ref.shape[0])

    @pl.when(live)
    def _():
        x = x_ref[...]
        g = _dot(x, wgbf_ref[...]) + bg_ref[...]
        u = _dot(x, wubf_ref[...]) + bu_ref[...]
        g = jnp.minimum(g, SWIGLU_LIMIT)
        u = jnp.clip(u, -SWIGLU_LIMIT, SWIGLU_LIMIT)
        o_ref[...] = ((u + 1.0) * g * _sigmoid(SWIGLU_ALPHA * g)).astype(o_ref.dtype)

    @pl.when(jnp.logical_not(live))
    def _():
        o_ref[...] = jnp.zeros_like(o_ref)


def _moe_dn_kernel(te_ref, tv_ref, a_ref, w_ref, b_ref, o_ref, wbf_ref):
    t = pl.program_id(1)
    live = tv_ref[t] > 0

    @pl.when(live & _new_expert(te_ref, t))
    def _():
        _cast_rows(w_ref, wbf_ref, w_ref.shape[0])

    @pl.when(live)
    def _():
        o_ref[...] = _dot(a_ref[...], wbf_ref[...]) + b_ref[...]

    @pl.when(jnp.logical_not(live))
    def _():
        o_ref[...] = jnp.zeros_like(o_ref)


def _moe_experts(x_rows, tile_e, tile_v, w_gu, b_gu, w_dn, b_dn, l, tm):
    r = x_rows.shape[0]
    nt = r // tm
    tn = 512
    nj = D_EXPERT // tn
    b_gu4 = b_gu.reshape(DEPTH, N_EXPERTS, 1, 2 * D_EXPERT)
    act = pl.pallas_call(
        _moe_up_kernel,
        grid_spec=pltpu.PrefetchScalarGridSpec(
            num_scalar_prefetch=2, grid=(nj, nt),
            in_specs=[pl.BlockSpec((tm, D_MODEL), lambda j, t, te, tv: (t * tv[t], 0)),
                      pl.BlockSpec((None, None, D_MODEL, tn), lambda j, t, te, tv: (l, te[t], 0, j)),
                      pl.BlockSpec((None, None, D_MODEL, tn), lambda j, t, te, tv: (l, te[t], 0, nj + j)),
                      pl.BlockSpec((None, None, 1, tn), lambda j, t, te, tv: (l, te[t], 0, j)),
                      pl.BlockSpec((None, None, 1, tn), lambda j, t, te, tv: (l, te[t], 0, nj + j))],
            out_specs=pl.BlockSpec((tm, tn), lambda j, t, te, tv: (t, j)),
            scratch_shapes=[pltpu.VMEM((D_MODEL, tn), BF), pltpu.VMEM((D_MODEL, tn), BF)]),
        out_shape=jax.ShapeDtypeStruct((r, D_EXPERT), BF),
        compiler_params=_cparams(2), name="moe_up",
    )(tile_e, tile_v, x_rows, w_gu, w_gu, b_gu4, b_gu4)
    tn2 = 2048
    return pl.pallas_call(
        _moe_dn_kernel,
        grid_spec=pltpu.PrefetchScalarGridSpec(
            num_scalar_prefetch=2, grid=(D_MODEL // tn2, nt),
            in_specs=[pl.BlockSpec((tm, D_EXPERT), lambda j, t, te, tv: (t * tv[t], 0)),
                      pl.BlockSpec((None, None, D_EXPERT, tn2), lambda j, t, te, tv: (l, te[t], 0, j)),
                      pl.BlockSpec((None, None, 1, tn2), lambda j, t, te, tv: (l, te[t], 0, j))],
            out_specs=pl.BlockSpec((tm, tn2), lambda j, t, te, tv: (t, j)),
            scratch_shapes=[pltpu.VMEM((D_EXPERT, tn2), BF)]),
        out_shape=jax.ShapeDtypeStruct((r, D_MODEL), F32),
        compiler_params=_cparams(2), name="moe_dn",
    )(tile_e, tile_v, act, w_dn, b_dn.reshape(DEPTH, N_EXPERTS, 1, D_MODEL))


def _moe(h2, h2_packed, w_router, b_router, w_gu, b_gu, w_dn, b_dn, l):
    b, t, d = h2.shape
    n_tok = b * t
    xs = h2.reshape(n_tok, d)
    gate_l, idx_l = _router(xs, w_router, b_router, l)
    gates = gate_l[:, :TOP_K]
    e_flat = idx_l[:, :TOP_K].reshape(-1)
    n_asg = n_tok * TOP_K
    tm = 512 if n_asg >= 512 * N_EXPERTS else 16
    order = jnp.argsort(e_flat, stable=True)
    counts = jnp.sum(jax.nn.one_hot(e_flat, N_EXPERTS, dtype=jnp.int32), axis=0)
    padded = (counts + tm - 1) // tm * tm
    pad_end = jnp.cumsum(padded)
    pad_start = pad_end - padded
    start = jnp.cumsum(counts) - counts
    e_sorted = e_flat[order]
    dest_sorted = pad_start[e_sorted] + jnp.arange(n_asg, dtype=jnp.int32) - start[e_sorted]
    n_rows = (n_asg // tm + N_EXPERTS) * tm
    nt = n_rows // tm
    row_tok = jnp.zeros((n_rows,), jnp.int32).at[dest_sorted].set((order // TOP_K).astype(jnp.int32))
    pos = jnp.zeros((n_asg,), jnp.int32).at[order].set(dest_sorted.astype(jnp.int32))
    tile_start = jnp.arange(nt, dtype=jnp.int32) * tm
    tile_e = jnp.minimum(jnp.searchsorted(pad_end, tile_start, side="right"), N_EXPERTS - 1).astype(jnp.int32)
    tile_v = (tile_start < pad_end[-1]).astype(jnp.int32)
    tr = next(c for c in (256, 128, 64, 32, 16, 8) if n_rows % c == 0)
    live = (jnp.arange(n_rows // tr, dtype=jnp.int32) * tr < pad_end[-1]).astype(jnp.int32)
    x_rows = _gather_rows(h2_packed.reshape(n_tok, d // 2), row_tok, live=live, unpack=True)
    y_rows = _moe_experts(x_rows, tile_e, tile_v, w_gu, b_gu, w_dn, b_dn, l, tm)
    y_slots = _gather_rows(y_rows, pos.reshape(n_tok, TOP_K).T.reshape(-1)).reshape(TOP_K, b, t, d)
    return y_slots, gates.reshape(b, t, TOP_K)


def _t5_bucket(dist):
    n = jnp.maximum(dist, 0)
    exact = N_BUCKETS // 2
    nf = jnp.maximum(n, 1).astype(F32)
    large = exact + (jnp.log(nf / exact) / math.log(MAX_DISTANCE / exact) * (N_BUCKETS - exact)).astype(jnp.int32)
    return jnp.where(n < exact, n, jnp.minimum(large, N_BUCKETS - 1))


def _bias_lookup(table, bucket):
    out = jnp.zeros((table.shape[1],) + bucket.shape, F32)
    for b in range(N_BUCKETS):
        out = jnp.where(bucket == b, table[b].reshape((-1,) + (1,) * bucket.ndim), out)
    return out


def _toeplitz_kernel(g_ref, o_ref):
    x = jnp.broadcast_to(g_ref[...], (TB, 2 * TB))
    o_ref[...] = pltpu.roll(x, TB + 1, 1, stride=1, stride_axis=0)[:, :TB]


def _t5_tiles(rel_bias, n_off):
    o = jnp.arange(n_off)[:, None]
    y = jnp.arange(2 * TB)[None, :]
    g = jnp.transpose(rel_bias[_t5_bucket(o * TB + (TB - 1) - y)], (2, 3, 0, 1))
    n_mh = 3 * N_HEADS
    return pl.pallas_call(
        _toeplitz_kernel, grid=(n_mh, n_off),
        in_specs=[pl.BlockSpec((None, None, 1, 2 * TB), lambda m, i: (m, i, 0, 0))],
        out_specs=pl.BlockSpec((None, None, TB, TB), lambda m, i: (m, i, 0, 0)),
        out_shape=jax.ShapeDtypeStruct((n_mh, n_off, TB, TB), F32),
        compiler_params=_cparams(2), name="t5_tiles",
    )(g.reshape(n_mh, n_off, 1, 2 * TB))


def _lam_of(dl_ref, lam_init):
    a = jnp.sum(dl_ref[0:1, :] * dl_ref[1:2, :], axis=-1, keepdims=True)
    b = jnp.sum(dl_ref[2:3, :] * dl_ref[3:4, :], axis=-1, keepdims=True)
    return jnp.exp(a) - jnp.exp(b) + lam_init


def _online_update(s, mask, v, m_ref, l_ref, acc_ref):
    if mask is not None:
        s = jnp.where(mask, s, NEG)
    m_prev = m_ref[...]
    m_new = jnp.maximum(m_prev, jnp.max(s, axis=-1, keepdims=True))
    alpha = jnp.exp(m_prev - m_new)
    p = jnp.exp(s - m_new)
    if mask is not None:
        p = jnp.where(mask, p, 0.0)
    l_ref[...] = alpha * l_ref[...] + jnp.sum(p, axis=-1, keepdims=True)
    acc_ref[...] = alpha * acc_ref[...] + _dot(p.astype(BF), v)
    m_ref[...] = m_new


def _normalized(l_ref, acc_ref):
    l = l_ref[...]
    return acc_ref[...] / jnp.where(l > 0, l, 1.0)


def _flash_kernel(*refs, nq, band, fox, sel, sel_shared, diff, gated, addend, lam_init):
    it = iter(refs)
    q_ref, k_ref, v_ref = next(it), next(it), next(it)
    if fox:
        fq_ref, fk_ref = next(it), next(it)
    else:
        tab_ref = next(it)
    if sel:
        sel_ref, e_ref = next(it), next(it)
    if gated:
        g_ref = next(it)
    if addend:
        add_ref = next(it)
    if diff:
        dl_ref, gn_ref = next(it), next(it)
    o_ref = next(it)
    n_maps = 2 if diff else 1
    m_ref, l_ref, acc_ref = next(it), next(it), next(it)

    i = pl.program_id(2)
    jj = pl.program_id(3)
    if band:
        j = i - 1 + jj
        active = j >= 0
        last = jj == 1
    else:
        j = jj
        active = jj <= i
        last = jj == nq - 1

    @pl.when(jj == 0)
    def _():
        m_ref[...] = jnp.full_like(m_ref, NEG)
        l_ref[...] = jnp.zeros_like(l_ref)
        acc_ref[...] = jnp.zeros_like(acc_ref)

    def tile(positional):
        k = k_ref[...].astype(BF)
        v = v_ref[...].astype(BF)
        pos_mask = None
        if positional:
            row = lax.broadcasted_iota(jnp.int32, (TB, TB), 0)
            col = lax.broadcasted_iota(jnp.int32, (TB, TB), 1)
            dpos = (i - j) * TB + row - col
            pos_mask = dpos >= 0
            if band:
                pos_mask = pos_mask & (dpos <= NSA_WINDOW)
        for g in range(GROUP):
            q = q_ref[:, g * HEAD_DIM:(g + 1) * HEAD_DIM]
            mask = pos_mask
            if sel:
                sm = _dot(sel_ref[0 if sel_shared else g].astype(BF), e_ref[...]) > 0.5
                mask = sm if mask is None else (mask & sm)
            bias = (fq_ref[g] - fk_ref[g]) if fox else tab_ref[g]
            for mi in range(n_maps):
                if diff:
                    lane = lax.broadcasted_iota(jnp.int32, q.shape, 1)
                    half = (lane < DIFF_DH) if mi == 0 else (lane >= DIFF_DH)
                    s = _dot_nt(jnp.where(half, q * DIFF_SCALE, 0.0).astype(BF), k) + bias
                else:
                    s = _dot_nt(q.astype(BF), k) * SCALE + bias
                _online_update(s, mask, v, m_ref.at[mi, g], l_ref.at[mi, g], acc_ref.at[mi, g])

    if band:
        pl.when(active)(lambda: tile(True))
    else:
        pl.when(active & (j == i))(lambda: tile(True))
        pl.when(active & (j != i))(lambda: tile(False))

    @pl.when(last)
    def _():
        for g in range(GROUP):
            o = _normalized(l_ref.at[0, g], acc_ref.at[0, g])
            if diff:
                o = o - _lam_of(dl_ref, lam_init) * _normalized(l_ref.at[1, g], acc_ref.at[1, g])
                o = o * lax.rsqrt(jnp.mean(o * o, axis=-1, keepdims=True) + LN_EPS) * gn_ref[...] * (1.0 - lam_init)
            if gated:
                o = _sigmoid(g_ref[g]) * o
            if addend:
                o = o + add_ref[:, g * HEAD_DIM:(g + 1) * HEAD_DIM]
            o_ref[:, g * HEAD_DIM:(g + 1) * HEAD_DIM] = o.astype(o_ref.dtype)


def _flash_prefill(proj, qcol, kvcol, *, fq=None, fk=None, tab=None, tab_i=0, sel=None, sel_e=None, sel_per_kv=False,
                   band=False, gates=None, gate_i=0, addend=None, diff=None, out_dtype=BF):
    b, t, _ = proj.shape
    nq = t // TB
    nkk = 2 if band else nq
    if band:
        jmap = lambda i, jj: jnp.maximum(i - 1 + jj, 0)
    else:
        jmap = lambda i, jj: jnp.minimum(jj, i)
    gw = GROUP * LANE
    in_specs = [pl.BlockSpec((None, TB, gw), lambda bi, kv, i, jj: (bi, i, qcol // GROUP + kv)),
                pl.BlockSpec((None, TB, LANE), lambda bi, kv, i, jj: (bi, jmap(i, jj), kvcol + kv)),
                pl.BlockSpec((None, TB, LANE), lambda bi, kv, i, jj: (bi, jmap(i, jj), kvcol + N_KV + kv))]
    args = [proj, proj, proj]
    if fq is not None:
        in_specs += [pl.BlockSpec((None, GROUP, TB, 1), lambda bi, kv, i, jj: (bi, kv, i, 0)),
                     pl.BlockSpec((None, GROUP, 1, TB), lambda bi, kv, i, jj: (bi, kv, 0, jmap(i, jj)))]
        args += [fq, fk]
    else:
        in_specs.append(pl.BlockSpec((GROUP, None, TB, TB),
                                     lambda bi, kv, i, jj: (tab_i * N_KV + kv, i - jmap(i, jj), 0, 0)))
        args.append(tab)
    if sel is not None:
        if sel_per_kv:
            in_specs.append(pl.BlockSpec((None, 1, TB, LANE), lambda bi, kv, i, jj: (bi, kv, i, 0)))
        else:
            in_specs.append(pl.BlockSpec((None, GROUP, TB, LANE), lambda bi, kv, i, jj: (bi, kv, i, 0)))
        in_specs.append(pl.BlockSpec((None, LANE, TB), lambda bi, kv, i, jj: (jmap(i, jj), 0, 0)))
        args += [sel, sel_e]
    if gates is not None:
        in_specs.append(pl.BlockSpec((None, None, GROUP, TB, 1), lambda bi, kv, i, jj: (bi, gate_i, kv, i, 0)))
        args.append(gates)
    if addend is not None:
        in_specs.append(pl.BlockSpec((None, TB, gw), lambda bi, kv, i, jj: (bi, i, kv)))
        args.append(addend)
    lam_init = 0.0
    if diff is not None:
        dl, gn, lam_init = diff
        in_specs += [pl.BlockSpec((4, DIFF_DH), lambda bi, kv, i, jj: (0, 0)),
                     pl.BlockSpec((1, HEAD_DIM), lambda bi, kv, i, jj: (0, 0))]
        args += [dl, gn]
    n_maps = 2 if diff is not None else 1
    scratch = [pltpu.VMEM((n_maps, GROUP, TB, 1), F32), pltpu.VMEM((n_maps, GROUP, TB, 1), F32),
               pltpu.VMEM((n_maps, GROUP, TB, HEAD_DIM), F32)]
    kern = functools.partial(_flash_kernel, nq=nq, band=band, fox=fq is not None, sel=sel is not None,
                             sel_shared=sel_per_kv, diff=diff is not None, gated=gates is not None,
                             addend=addend is not None, lam_init=lam_init)
    return pl.pallas_call(
        kern, grid=(b, N_KV, nq, nkk), in_specs=in_specs,
        out_specs=pl.BlockSpec((None, TB, gw), lambda bi, kv, i, jj: (bi, i, kv)),
        out_shape=jax.ShapeDtypeStruct((b, t, MIX_W), out_dtype),
        scratch_shapes=scratch, compiler_params=_cparams(4), name="flash_prefill",
    )(*args)


def _logsig_kernel(x_ref, o_ref):
    x = x_ref[...]
    o_ref[...] = jnp.minimum(x, 0.0) - jnp.log(1.0 + jnp.exp(-jnp.abs(x)))


def _log_sigmoid(x):
    b, h, t = x.shape
    return pl.pallas_call(
        _logsig_kernel, grid=(b,),
        in_specs=[pl.BlockSpec((None, h, t), lambda i: (i, 0, 0))],
        out_specs=pl.BlockSpec((None, h, t), lambda i: (i, 0, 0)),
        out_shape=jax.ShapeDtypeStruct((b, h, t), F32), compiler_params=_cparams(1), name="log_sigmoid",
    )(x)


CS_CHUNK = 512


def _cumsum_kernel(x_ref, o_ref, carry_ref):
    @pl.when(pl.program_id(1) == 0)
    def _():
        carry_ref[...] = jnp.zeros_like(carry_ref)
    r = lax.broadcasted_iota(jnp.int32, (CS_CHUNK, CS_CHUNK), 0)
    c = lax.broadcasted_iota(jnp.int32, (CS_CHUNK, CS_CHUNK), 1)
    tri = (r <= c).astype(F32)
    y = _dot_hi(x_ref[...], tri) + carry_ref[...]
    o_ref[...] = y
    carry_ref[...] = y[:, CS_CHUNK - 1:CS_CHUNK]


def _cumsum(x):
    b, h, n = x.shape
    return pl.pallas_call(
        _cumsum_kernel, grid=(b, n // CS_CHUNK),
        in_specs=[pl.BlockSpec((None, h, CS_CHUNK), lambda i, j: (i, 0, j))],
        out_specs=pl.BlockSpec((None, h, CS_CHUNK), lambda i, j: (i, 0, j)),
        out_shape=jax.ShapeDtypeStruct((b, h, n), F32),
        scratch_shapes=[pltpu.VMEM((h, 1), F32)], compiler_params=_cparams(2), name="cumsum",
    )(x)


def _moba_gate_kernel(q_ref, k_ref, sel_ref, *, t):
    nblk = t // MOBA_BLOCK
    r = lax.broadcasted_iota(jnp.int32, (LANE, t), 0)
    c = lax.broadcasted_iota(jnp.int32, (LANE, t), 1)
    avg = jnp.where(c // MOBA_BLOCK == r, 1.0 / MOBA_BLOCK, 0.0)
    kmean = _dot_hi(avg, k_ref[...])
    gs = _dot_nt(q_ref[...].astype(BF), kmean.astype(BF))
    lane = lax.broadcasted_iota(jnp.int32, gs.shape, 1)
    own = lax.broadcasted_iota(jnp.int32, gs.shape, 0) // MOBA_BLOCK
    vals = jnp.where(lane < own, gs, NEG)
    sel = (lane == own).astype(F32)
    for mx, idx in _topk_lanes(vals, lane, min(MOBA_TOPK, nblk), -3e38):
        sel = jnp.where((lane.astype(F32) == idx) & (mx > NEG_TEST), 1.0, sel)
    sel_ref[...] = sel


def _moba_gate(proj):
    b, t, _ = proj.shape
    return pl.pallas_call(
        functools.partial(_moba_gate_kernel, t=t), grid=(b, N_HEADS),
        in_specs=[pl.BlockSpec((None, t, LANE), lambda bi, h: (bi, 0, QM + h)),
                  pl.BlockSpec((None, t, LANE), lambda bi, h: (bi, 0, KVM + h // GROUP))],
        out_specs=pl.BlockSpec((None, None, t, LANE), lambda bi, h: (bi, h, 0, 0)),
        out_shape=jax.ShapeDtypeStruct((b, N_HEADS, t, LANE), F32),
        compiler_params=_cparams(2), name="moba_gate",
    )(proj, proj)


def _compress_kernel(c_ref, w_ref, o_ref):
    o_ref[...] = _dot(c_ref[...].astype(BF), w_ref[...].astype(BF))


def _nsa_compress(chunks, w_cmp, l):
    b, _, _, n, sd = chunks.shape
    rt = min(n, 512)
    r_len = NSA_CMP_LEN // NSA_CMP_STRIDE
    w = w_cmp.reshape(DEPTH, 2, r_len, sd, HEAD_DIM)
    y = pl.pallas_call(
        _compress_kernel, grid=(b, 2, N_KV, r_len, n // rt),
        in_specs=[pl.BlockSpec((None, None, None, rt, sd), lambda bi, j, k, r, i: (bi, j, k, i, 0)),
                  pl.BlockSpec((None, None, None, sd, HEAD_DIM), lambda bi, j, k, r, i: (l, j, r, 0, 0))],
        out_specs=pl.BlockSpec((None, None, None, None, rt, HEAD_DIM), lambda bi, j, k, r, i: (bi, j, k, r, i, 0)),
        out_shape=jax.ShapeDtypeStruct((b, 2, N_KV, r_len, n, HEAD_DIM), F32),
        compiler_params=_cparams(5), name="nsa_compress",
    )(chunks, w)
    return _combine_halves(y)


def _combine_halves(y):
    return y[:, :, :, 0] + jnp.pad(y[:, :, :, 1, 1:], ((0, 0), (0, 0), (0, 0), (0, 1), (0, 0)))


def _compress_paged_kernel(*refs, pp):
    _pt_ref = refs[0]
    page_refs = refs[1:1 + pp]
    w_ref, o_ref = refs[1 + pp], refs[2 + pp]
    cpp = PAGE_SIZE // NSA_CMP_STRIDE
    chunk_rows = NSA_CMP_STRIDE * 2 * N_KV
    r_len = NSA_CMP_LEN // NSA_CMP_STRIDE
    for j in range(2):
        for kv in range(N_KV):
            acc = [jnp.zeros((pp * cpp, HEAD_DIM), F32) for _ in range(r_len)]
            for s in range(NSA_CMP_STRIDE):
                start = s * 2 * N_KV + j * N_KV + kv
                xs = jnp.concatenate([ref[pl.ds(start, cpp, stride=chunk_rows), :] for ref in page_refs], axis=0)
                xs = xs.astype(BF)
                for r in range(r_len):
                    acc[r] = acc[r] + _dot(xs, w_ref[j, r * NSA_CMP_STRIDE + s].astype(BF))
            for r in range(r_len):
                o_ref[j, kv, r] = acc[r]


def _nsa_compress_paged(pool, l, page_table, w_cmp, pp):
    b, n_pages = page_table.shape
    cpp = PAGE_SIZE // NSA_CMP_STRIDE
    r_len = NSA_CMP_LEN // NSA_CMP_STRIDE
    page_rows = PAGE_SIZE * 2 * N_KV
    pool2 = pool.reshape(pool.shape[0], pool.shape[1], page_rows, HEAD_DIM)
    page_specs = [pl.BlockSpec((None, None, page_rows, HEAD_DIM), (lambda bi, s, pt, i=i: (l, pt[bi, s * pp + i], 0, 0)))
                  for i in range(pp)]
    y = pl.pallas_call(
        functools.partial(_compress_paged_kernel, pp=pp),
        grid_spec=pltpu.PrefetchScalarGridSpec(
            num_scalar_prefetch=1, grid=(b, n_pages // pp),
            in_specs=page_specs + [pl.BlockSpec((None, 2, NSA_CMP_LEN, HEAD_DIM, HEAD_DIM), lambda bi, s, pt: (l, 0, 0, 0, 0))],
            out_specs=pl.BlockSpec((None, 2, N_KV, r_len, pp * cpp, HEAD_DIM), lambda bi, s, pt: (bi, 0, 0, 0, s, 0))),
        out_shape=jax.ShapeDtypeStruct((b, 2, N_KV, r_len, n_pages * cpp, HEAD_DIM), F32),
        compiler_params=_cparams(2), name="nsa_compress_paged",
    )(page_table, *([pool2] * pp), w_cmp)
    return _combine_halves(y)


def _nsa_cmp_kernel(q_ref, kc_ref, vc_ref, tab_ref, cover_ref, g_ref, o_ref, sel_ref, *, n_cmp, n_slc):
    i = pl.program_id(2)
    shape = (TB, LANE)
    qpos = i * TB + lax.broadcasted_iota(jnp.int32, shape, 0)
    lane = lax.broadcasted_iota(jnp.int32, shape, 1)
    maskc = (lane * NSA_CMP_STRIDE + (NSA_CMP_LEN - 1) <= qpos) & (lane < n_cmp)
    kc = kc_ref[...].astype(BF)
    vc = vc_ref[...].astype(BF)
    imp = jnp.zeros(shape, F32)
    for g in range(GROUP):
        q = q_ref[:, g * HEAD_DIM:(g + 1) * HEAD_DIM].astype(BF)
        s = jnp.where(maskc, _dot_nt(q, kc) * SCALE + tab_ref[g], NEG)
        mx = jnp.max(s, axis=-1, keepdims=True)
        p = jnp.where(maskc, jnp.exp(s - mx), 0.0)
        z = jnp.sum(p, axis=-1, keepdims=True)
        pc = (p / jnp.where(z > 0, z, 1.0)).astype(BF)
        o_ref[:, g * HEAD_DIM:(g + 1) * HEAD_DIM] = _sigmoid(g_ref[g]) * _dot(pc, vc)
        imp = imp + _dot(pc, cover_ref[...].astype(BF))
    cur = qpos // NSA_SLC_BLOCK
    forced = (lane == 0) | ((lane <= cur) & (lane > cur - NSA_N_LOCAL))
    vals = jnp.where(forced, NSA_FORCE, jnp.where(lane <= cur, imp, -1.0))
    vals = jnp.where(lane < n_slc, vals, NEG)
    sel = jnp.zeros(shape, F32)
    for mx, idx in _topk_lanes(vals, lane, min(NSA_TOPN, n_slc), -3e38):
        sel = jnp.where((lane.astype(F32) == idx) & (mx >= 0.0), 1.0, sel)
    sel_ref[...] = sel


def _nsa_cmp_prefill(proj, kvc, tabc, cover, gates, n_cmp, n_slc):
    b, t, _ = proj.shape
    nq = t // TB
    return pl.pallas_call(
        functools.partial(_nsa_cmp_kernel, n_cmp=n_cmp, n_slc=n_slc), grid=(b, N_KV, nq),
        in_specs=[pl.BlockSpec((None, TB, GROUP * LANE), lambda bi, k, i: (bi, i, QN // GROUP + k)),
                  pl.BlockSpec((None, None, None, LANE, HEAD_DIM), lambda bi, k, i: (bi, 0, k, 0, 0)),
                  pl.BlockSpec((None, None, None, LANE, HEAD_DIM), lambda bi, k, i: (bi, 1, k, 0, 0)),
                  pl.BlockSpec((GROUP, TB, LANE), lambda bi, k, i: (k, i, 0)),
                  pl.BlockSpec((LANE, LANE), lambda bi, k, i: (0, 0)),
                  pl.BlockSpec((None, None, GROUP, TB, 1), lambda bi, k, i: (bi, 0, k, i, 0))],
        out_specs=[pl.BlockSpec((None, TB, GROUP * LANE), lambda bi, k, i: (bi, i, k)),
                   pl.BlockSpec((None, None, TB, LANE), lambda bi, k, i: (bi, k, i, 0))],
        out_shape=[jax.ShapeDtypeStruct((b, t, MIX_W), F32), jax.ShapeDtypeStruct((b, N_KV, t, LANE), F32)],
        compiler_params=_cparams(3), name="nsa_cmp_prefill",
    )(proj, kvc, kvc, tabc, cover, gates)


def _cover_matrix(n_cmp, n_slc):
    c_start = np.arange(n_cmp) * NSA_CMP_STRIDE
    s_ids = np.arange(n_slc)
    return ((c_start[:, None] < (s_ids[None, :] + 1) * NSA_SLC_BLOCK)
            & (c_start[:, None] + NSA_CMP_LEN > s_ids[None, :] * NSA_SLC_BLOCK)).astype(np.float32)


def _sel_expand(n_tiles, block):
    j = np.arange(n_tiles)[:, None, None]
    m = np.arange(LANE)[None, :, None]
    c = np.arange(TB)[None, None, :]
    return jnp.asarray(m == (j * TB + c) // block, BF)


def _mixers_prompt(l, h, P, tabs):
    b, t, d = h.shape
    n = b * t
    proj = _mm(h.reshape(n, d), P["w_in_r"], P["b_in_r"], l).reshape(b, t, PROJ_W)
    misc = proj[:, :, MISC * LANE:MISC * LANE + N_HEADS + 3 * N_HEADS]
    logf = _log_sigmoid(jnp.swapaxes(misc[..., :N_HEADS], 1, 2))
    gates = jnp.transpose(misc[..., N_HEADS:].reshape(b, t, 3, N_HEADS), (0, 2, 3, 1))[..., None]
    kvs = lambda c: proj[:, :, c * LANE:(c + 4) * LANE].reshape(b, t, 2, N_KV, HEAD_DIM)
    rows = dict(fox_kv=kvs(KVF), fox_logf=jnp.swapaxes(logf, 1, 2), diff_kv=kvs(KVD), moba_kv=kvs(KVM),
                nsa_cmp_kv=kvs(KVC), nsa_slc_kv=kvs(KVS))
    win = kvs(KVW)
    rows["nsa_win_kv"] = win[:, -min(NSA_WINDOW, t):]

    f_cum = _cumsum(logf)
    o_fox = _flash_prefill(proj, QF, KVF, fq=f_cum[..., None], fk=f_cum[:, :, None, :])

    lam_init = 0.8 - 0.6 * math.exp(-0.3 * l)
    o_diff = _flash_prefill(proj, QD, KVD, tab=tabs, tab_i=0,
                            diff=(P["diff_lambda"][l], P["diff_norm_g"][l].reshape(1, HEAD_DIM), lam_init))

    nk = t // TB
    o_moba = _flash_prefill(proj, QM, KVM, tab=tabs, tab_i=1, sel=_moba_gate(proj), sel_e=_sel_expand(nk, MOBA_BLOCK))

    n_chunk = t // NSA_CMP_STRIDE
    n_cmp = n_chunk - NSA_CMP_LEN // NSA_CMP_STRIDE + 1
    n_slc = -(-t // NSA_SLC_BLOCK)
    chunks = jnp.transpose(rows["nsa_cmp_kv"], (0, 2, 3, 1, 4)).reshape(b, 2, N_KV, n_chunk, NSA_CMP_STRIDE * HEAD_DIM)
    kvc = _nsa_compress(chunks, P["w_cmp"], l)
    cmp_end = jnp.arange(LANE) * NSA_CMP_STRIDE + (NSA_CMP_LEN - 1)
    tabc = _bias_lookup(P["rel_bias"][:, 2], _t5_bucket(jnp.arange(t)[:, None] - cmp_end[None, :]))
    cover = jnp.asarray(np.pad(_cover_matrix(n_cmp, n_slc), ((0, LANE - n_cmp), (0, LANE - n_slc))))
    o1, sel_n = _nsa_cmp_prefill(proj, kvc, tabc, cover, gates, n_cmp, n_slc)
    o2 = _flash_prefill(proj, QN, KVS, tab=tabs, tab_i=2, sel=sel_n, sel_e=_sel_expand(nk, NSA_SLC_BLOCK), sel_per_kv=True,
                        gates=gates, gate_i=1, addend=o1, out_dtype=F32)
    o_nsa = _flash_prefill(proj, QN, KVW, tab=tabs, tab_i=2, band=True, gates=gates, gate_i=2, addend=o2)
    o_all = jnp.stack([o_fox, o_diff, o_moba, o_nsa]).reshape(4, n, MIX_W)
    return o_all, rows


def _decode_kernel(*refs, pp, ns, diff, lam_init):
    it = iter(refs)
    _pt_ref = next(it)
    q_ref = next(it)
    page_refs = [next(it) for _ in range(pp)]
    new_ref, bias_ref = next(it), next(it)
    if diff:
        dl_ref, gn_ref = next(it), next(it)
    o_ref = next(it)
    m_ref, l_ref, acc_ref = next(it), next(it), next(it)
    n_maps = 2 if diff else 1
    s_id = pl.program_id(1)

    @pl.when(s_id == 0)
    def _():
        m_ref[...] = jnp.full_like(m_ref, NEG)
        l_ref[...] = jnp.zeros_like(l_ref)
        acc_ref[...] = jnp.zeros_like(acc_ref)

    def process(page_list):
        n_keys = len(page_list) * PAGE_SIZE
        for kv in range(N_KV):
            rows_of = lambda ref, j: ref[pl.ds(j * N_KV + kv, PAGE_SIZE, stride=2 * N_KV), :]
            k = jnp.concatenate([rows_of(ref, 0) for ref in page_list], axis=0).astype(BF)
            v = jnp.concatenate([rows_of(ref, 1) for ref in page_list], axis=0).astype(BF)
            bias = bias_ref[kv, :, :n_keys]
            mask = bias > NEG_TEST
            q = q_ref[kv]
            for mi in range(n_maps):
                if diff:
                    lane = lax.broadcasted_iota(jnp.int32, q.shape, 1)
                    half = (lane < DIFF_DH) if mi == 0 else (lane >= DIFF_DH)
                    s = _dot_nt(jnp.where(half, q * DIFF_SCALE, 0.0).astype(BF), k) + bias
                else:
                    s = _dot_nt(q.astype(BF), k) * SCALE + bias
                _online_update(s, mask, v, m_ref.at[mi, kv], l_ref.at[mi, kv], acc_ref.at[mi, kv])

    @pl.when(s_id < ns)
    def _():
        process(page_refs)

    @pl.when(s_id == ns)
    def _():
        process([new_ref])
        for kv in range(N_KV):
            o = _normalized(l_ref.at[0, kv], acc_ref.at[0, kv])
            if diff:
                o = o - _lam_of(dl_ref, lam_init) * _normalized(l_ref.at[1, kv], acc_ref.at[1, kv])
                o = o * lax.rsqrt(jnp.mean(o * o, axis=-1, keepdims=True) + LN_EPS) * gn_ref[...] * (1.0 - lam_init)
            o_ref[kv] = o


def _decode_attn(q, pool, l, page_table, new_kv, bias, pp, diff=None):
    b, _, r, _ = q.shape
    n_pages = page_table.shape[1]
    ns = n_pages // pp
    bb = bias.shape[0]
    page_rows = PAGE_SIZE * 2 * N_KV
    pool2 = pool.reshape(pool.shape[0], pool.shape[1], page_rows, HEAD_DIM)
    blk = (None, None, page_rows, HEAD_DIM)
    page_specs = [pl.BlockSpec(blk, (lambda bi, s, pt, i=i: (l, pt[bi, jnp.minimum(s, ns - 1) * pp + i], 0, 0)))
                  for i in range(pp)]
    in_specs = ([pl.BlockSpec((None, N_KV, r, HEAD_DIM), lambda bi, s, pt: (bi, 0, 0, 0))] + page_specs +
                [pl.BlockSpec((None, page_rows, HEAD_DIM), lambda bi, s, pt: (bi, 0, 0)),
                 pl.BlockSpec((None, N_KV, r, pp * PAGE_SIZE), lambda bi, s, pt: (bi if bb > 1 else 0, 0, 0, s))])
    args = [q] + [pool2] * pp + [new_kv.reshape(b, page_rows, HEAD_DIM), bias]
    lam_init = 0.0
    if diff is not None:
        dl, gn, lam_init = diff
        in_specs += [pl.BlockSpec((4, DIFF_DH), lambda bi, s, pt: (0, 0)),
                     pl.BlockSpec((1, HEAD_DIM), lambda bi, s, pt: (0, 0))]
        args += [dl, gn]
    n_maps = 2 if diff is not None else 1
    return pl.pallas_call(
        functools.partial(_decode_kernel, pp=pp, ns=ns, diff=diff is not None, lam_init=lam_init),
        grid_spec=pltpu.PrefetchScalarGridSpec(
            num_scalar_prefetch=1, grid=(b, ns + 1), in_specs=in_specs,
            out_specs=pl.BlockSpec((None, N_KV, r, HEAD_DIM), lambda bi, s, pt: (bi, 0, 0, 0)),
            scratch_shapes=[pltpu.VMEM((n_maps, N_KV, r, 1), F32), pltpu.VMEM((n_maps, N_KV, r, 1), F32),
                            pltpu.VMEM((n_maps, N_KV, r, HEAD_DIM), F32)]),
        out_shape=jax.ShapeDtypeStruct((b, N_KV, r, HEAD_DIM), F32),
        compiler_params=_cparams(2), name="decode_attn",
    )(page_table, *args)


def _page_sum_kernel(*refs, pp):
    _pt_ref = refs[0]
    page_refs = refs[1:1 + pp]
    o_ref = refs[1 + pp]
    for i, ref in enumerate(page_refs):
        for kv in range(N_KV):
            o_ref[i, kv:kv + 1, :] = jnp.sum(ref[:, kv, :], axis=0, keepdims=True)


def _page_key_sums(pool, l, page_table, pp):
    b, n_pages = page_table.shape
    blk = (None, None, PAGE_SIZE, None, N_KV, HEAD_DIM)
    page_specs = [pl.BlockSpec(blk, (lambda bi, s, pt, i=i: (l, pt[bi, s * pp + i], 0, 0, 0, 0))) for i in range(pp)]
    return pl.pallas_call(
        functools.partial(_page_sum_kernel, pp=pp),
        grid_spec=pltpu.PrefetchScalarGridSpec(
            num_scalar_prefetch=1, grid=(b, n_pages // pp), in_specs=page_specs,
            out_specs=pl.BlockSpec((None, pp, N_KV, HEAD_DIM), lambda bi, s, pt: (bi, s, 0, 0))),
        out_shape=jax.ShapeDtypeStruct((b, n_pages, N_KV, HEAD_DIM), F32),
        compiler_params=_cparams(2), name="page_key_sums",
    )(page_table, *([pool] * pp))


def _cmp_decode_kernel(q_ref, kvc_ref, bias_ref, o_ref, p_ref):
    for kv in range(N_KV):
        q = q_ref[kv].astype(BF)
        bias = bias_ref[kv]
        mask = bias > NEG_TEST
        s = jnp.where(mask, _dot_nt(q, kvc_ref[0, kv].astype(BF)) * SCALE + bias, NEG)
        mx = jnp.max(s, axis=-1, keepdims=True)
        p = jnp.where(mask, jnp.exp(s - mx), 0.0)
        z = jnp.sum(p, axis=-1, keepdims=True)
        pc = p / jnp.where(z > 0, z, 1.0)
        p_ref[kv] = pc
        o_ref[kv] = _dot(pc.astype(BF), kvc_ref[1, kv].astype(BF))


def _cmp_decode(q, kvc, bias):
    b, _, r, _ = q.shape
    n = kvc.shape[3]
    return pl.pallas_call(
        _cmp_decode_kernel, grid=(b,),
        in_specs=[pl.BlockSpec((None, N_KV, r, HEAD_DIM), lambda bi: (bi, 0, 0, 0)),
                  pl.BlockSpec((None, 2, N_KV, n, HEAD_DIM), lambda bi: (bi, 0, 0, 0, 0)),
                  pl.BlockSpec((N_KV, r, n), lambda bi: (0, 0, 0))],
        out_specs=[pl.BlockSpec((None, N_KV, r, HEAD_DIM), lambda bi: (bi, 0, 0, 0)),
                   pl.BlockSpec((None, N_KV, r, n), lambda bi: (bi, 0, 0, 0))],
        out_shape=[jax.ShapeDtypeStruct((b, N_KV, r, HEAD_DIM), F32), jax.ShapeDtypeStruct((b, N_KV, r, n), F32)],
        compiler_params=_cparams(1), name="cmp_decode",
    )(q, kvc, bias)


def _rows_of(x, t):
    b = x.shape[0]
    return jnp.transpose(x.reshape(b, t, N_KV, GROUP, HEAD_DIM), (0, 2, 3, 1, 4)).reshape(b, N_KV, GROUP * t, HEAD_DIM)


def _unrows(o, t):
    b = o.shape[0]
    return jnp.transpose(o.reshape(b, N_KV, GROUP, t, HEAD_DIM), (0, 3, 1, 2, 4)).reshape(b, t, MIX_W)


def _head_rows(a, t):
    b = a.shape[0]
    return a.reshape(b, N_KV, GROUP * t, a.shape[-1])


DEC_PP = 16


def _sample_tables(rel_bias, t, n_pages, w_len, past_len):
    lkp = (n_pages // DEC_PP + 1) * DEC_PP * PAGE_SIZE
    qpos = past_len + jnp.arange(t)
    kpos = jnp.arange(lkp)
    kvalid = (kpos[None, :] <= qpos[:, None]) & (kpos[None, :] < past_len + t)
    t5 = _bias_lookup(rel_bias.reshape(N_BUCKETS, 3 * N_HEADS), _t5_bucket(qpos[:, None] - kpos[None, :]))
    t5 = jnp.where(kvalid[None, None], t5.reshape(3, N_HEADS, t, lkp), NEG)
    n_chunk = (past_len + t) // NSA_CMP_STRIDE
    n_cmp = n_chunk - NSA_CMP_LEN // NSA_CMP_STRIDE + 1
    cmp_end = jnp.arange(n_chunk) * NSA_CMP_STRIDE + (NSA_CMP_LEN - 1)
    cvalid = (cmp_end[None, :] <= qpos[:, None]) & (jnp.arange(n_chunk)[None, :] < n_cmp)
    bias_c = _bias_lookup(rel_bias[:, 2], _t5_bucket(qpos[:, None] - cmp_end[None, :]))
    bias_c = jnp.where(cvalid[None], bias_c, NEG).reshape(N_KV, GROUP * t, n_chunk)
    wp = w_len // PAGE_SIZE
    lkw = 2 * wp * PAGE_SIZE
    kidx = jnp.arange(lkw)
    dw = (w_len + jnp.arange(t))[:, None] - kidx[None, :]
    wvalid = (dw >= 0) & (dw <= NSA_WINDOW) & (kidx[None, :] < w_len + t)
    bias_w = _bias_lookup(rel_bias[:, 2], _t5_bucket(dw))
    bias_w = jnp.where(wvalid[None], bias_w, NEG).reshape(1, N_KV, GROUP * t, lkw)
    return dict(kvalid=kvalid, t5=t5, bias_c=bias_c, bias_w=bias_w)


def _mixers_sample(l, h, P, caches, page_table, past_len, tables):
    b, t, d = h.shape
    n = b * t
    n_pages = page_table.shape[1]
    pp = DEC_PP
    ns = n_pages // pp
    lkp = (ns + 1) * pp * PAGE_SIZE
    proj = _mm(h.reshape(n, d), P["w_in_r"], P["b_in_r"], l).reshape(b, t, PROJ_W)
    seg = lambda c, w: proj[:, :, c * LANE:(c + w) * LANE]
    misc = proj[:, :, MISC * LANE:MISC * LANE + 4 * N_HEADS]
    kvs = lambda c: seg(c, 4).reshape(b, t, 2, N_KV, HEAD_DIM)
    pad_new = lambda kv: jnp.pad(kv, ((0, 0), (0, PAGE_SIZE - t), (0, 0), (0, 0), (0, 0)))
    logf_new = _log_sigmoid(jnp.pad(jnp.swapaxes(misc[..., :N_HEADS], 1, 2), ((0, 0), (0, 0), (0, LANE - t))))[..., :t]
    rows = dict(fox_kv=kvs(KVF), fox_logf=jnp.swapaxes(logf_new, 1, 2), diff_kv=kvs(KVD), moba_kv=kvs(KVM),
                nsa_cmp_kv=kvs(KVC), nsa_slc_kv=kvs(KVS))
    win_new = kvs(KVW)
    win_past = caches["nsa_win_kv"][l]
    w_len = win_past.shape[1]
    rows["nsa_win_kv"] = jnp.concatenate([win_past, win_new], axis=1)[:, -min(NSA_WINDOW, w_len + t):]

    qpos = past_len + jnp.arange(t)
    kpos = jnp.arange(lkp)
    kvalid = tables["kvalid"]
    t5_bias = lambda ti: tables["t5"][ti]

    logf_past = caches["fox_logf"][l][page_table].reshape(b, past_len, N_HEADS)
    lf = jnp.concatenate([jnp.swapaxes(logf_past, 1, 2), logf_new], axis=2)
    lf = jnp.pad(lf, ((0, 0), (0, 0), (0, lkp - past_len - t)))
    f_cum = _cumsum(lf)
    f_q = f_cum[:, :, past_len:past_len + t]
    bias_fox = jnp.where(kvalid[None, None], f_q[..., None] - f_cum[:, :, None, :], NEG)
    o_fox = _decode_attn(_rows_of(seg(QF, 8), t), caches["fox_kv"], l, page_table, pad_new(rows["fox_kv"]),
                         _head_rows(bias_fox, t), pp)

    lam_init = 0.8 - 0.6 * math.exp(-0.3 * l)
    o_diff = _decode_attn(_rows_of(seg(QD, 8), t), caches["diff_kv"], l, page_table, pad_new(rows["diff_kv"]),
                          _head_rows(t5_bias(0)[None], t), pp,
                          diff=(P["diff_lambda"][l], P["diff_norm_g"][l].reshape(1, HEAD_DIM), lam_init))

    lk = past_len + t
    nblk = -(-lk // MOBA_BLOCK)
    ppb = MOBA_BLOCK // PAGE_SIZE
    psum = _page_key_sums(caches["moba_kv"], l, page_table, pp)
    kmean = psum.reshape(b, n_pages // ppb, ppb, N_KV, HEAD_DIM).sum(2) / MOBA_BLOCK
    q_m = seg(QM, 8).reshape(b, t, N_KV, GROUP, HEAD_DIM)
    gs = jnp.einsum("btkgd,bmkd->bkgtm", q_m, kmean)
    own = qpos // MOBA_BLOCK
    blk_ids = jnp.arange(n_pages // ppb)
    gs = jnp.where(blk_ids[None, :] < own[:, None], gs, -jnp.inf)
    top_v, top_i = lax.top_k(gs, min(MOBA_TOPK, nblk))
    kblk = kpos // MOBA_BLOCK
    in_chosen = jnp.any((top_i[..., None] == kblk) & jnp.isfinite(top_v)[..., None], axis=-2)
    keep = in_chosen | (kblk[None, :] == own[:, None])
    bias_moba = jnp.where(keep, t5_bias(1).reshape(N_KV, GROUP, t, lkp)[None], NEG)
    o_moba = _decode_attn(_rows_of(seg(QM, 8), t), caches["moba_kv"], l, page_table, pad_new(rows["moba_kv"]),
                          bias_moba.reshape(b, N_KV, GROUP * t, lkp), pp)

    n_chunk = lk // NSA_CMP_STRIDE
    r_len = NSA_CMP_LEN // NSA_CMP_STRIDE
    n_cmp = n_chunk - r_len + 1
    n_slc = -(-lk // NSA_SLC_BLOCK)
    assert t < NSA_CMP_STRIDE and past_len % NSA_CMP_STRIDE == 0
    kvc = _nsa_compress_paged(caches["nsa_cmp_kv"], l, page_table, P["w_cmp"], pp)
    q_n = _rows_of(seg(QN, 8), t)
    o_cmp, pc = _cmp_decode(q_n, kvc, tables["bias_c"])
    cover = jnp.asarray(_cover_matrix(n_cmp, n_slc))
    imp = jnp.einsum("bkgtm,mj->bktj", pc.reshape(b, N_KV, GROUP, t, n_chunk)[..., :n_cmp], cover)
    cur = (qpos // NSA_SLC_BLOCK)[:, None]
    sl_ids = jnp.arange(n_slc)
    forced = (sl_ids[None, :] == 0) | ((sl_ids[None, :] <= cur) & (sl_ids[None, :] > cur - NSA_N_LOCAL))
    imp = jnp.where(forced, NSA_FORCE, jnp.where(sl_ids[None, :] <= cur, imp, -1.0))
    top_v, top_i = lax.top_k(imp, min(NSA_TOPN, n_slc))
    kslc = jnp.minimum(kpos // NSA_SLC_BLOCK, n_slc - 1)
    keep_s = jnp.any((top_i[..., None] == kslc) & (top_v >= 0)[..., None], axis=-2)
    bias_t5n = t5_bias(2).reshape(N_KV, GROUP, t, lkp)
    bias_slc = jnp.where(keep_s[:, :, None], bias_t5n[None], NEG).reshape(b, N_KV, GROUP * t, lkp)
    o_slc = _decode_attn(q_n, caches["nsa_slc_kv"], l, page_table, pad_new(rows["nsa_slc_kv"]), bias_slc, pp)

    wp = w_len // PAGE_SIZE
    pool_w = caches["nsa_win_kv"].reshape(DEPTH, b * wp, PAGE_SIZE, 2, N_KV, HEAD_DIM)
    pt_w = (jnp.arange(b, dtype=jnp.int32)[:, None] * wp + jnp.arange(wp, dtype=jnp.int32)[None, :])
    o_win = _decode_attn(q_n, pool_w, l, pt_w, pad_new(win_new), tables["bias_w"], wp)

    g = _sigmoid(misc[..., N_HEADS:].reshape(b, t, 3, N_HEADS))[..., None]
    hd = lambda o: _unrows(o, t).reshape(b, t, N_HEADS, HEAD_DIM)
    o_nsa = (g[:, :, 0] * hd(o_cmp) + g[:, :, 1] * hd(o_slc) + g[:, :, 2] * hd(o_win)).reshape(b, t, MIX_W)
    o_all = jnp.stack([_unrows(o_fox, t), _unrows(o_diff, t), _unrows(o_moba, t), o_nsa]).astype(BF).reshape(4, n, MIX_W)
    return o_all, rows


STATE_NAMES = ("fox_kv", "fox_logf", "diff_kv", "moba_kv", "nsa_cmp_kv", "nsa_slc_kv", "nsa_win_kv")


def _run_group(x, mod, P, mixers):
    b, t, d = x.shape
    n = b * t
    rows = {name: [] for name in STATE_NAMES}
    h = _modulate(x, mod, 0)
    for l in range(DEPTH):
        o_all, new_rows = mixers(l, h)
        merged = _gate_merge(h.reshape(n, d), o_all, P["w_gate"], P["b_gate"], P["w_branch"], l)
        y = _mm(merged, P["w_o"], P["zero_bias"], l).reshape(b, t, d)
        x, h2, h2_packed = _ln_mod(x, y, None, mod, P["ln_g"], P["ln_b"], l, 0, l, 3)
        y_slots, gates = _moe(h2, h2_packed, P["w_router"], P["b_router"], P["w_gu"], P["b_gu"], P["w_dn"], P["b_dn"], l)
        x, h, _ = _ln_mod(x, y_slots, gates, mod, P["ln_g"], P["ln_b"], l, 1, l + 1 if l + 1 < DEPTH else None, 0)
        for name in STATE_NAMES:
            rows[name].append(new_rows[name])
    return x, {name: jnp.stack(rows[name]) for name in STATE_NAMES}


def kernel(x_prompt, x_sample, cache_fox_kv, cache_fox_logf, cache_diff_kv, cache_moba_kv, cache_nsa_cmp_kv,
           cache_nsa_slc_kv, cache_nsa_win_kv, page_table, c_prompt, c_sample, w_ada, b_ada, ln_g, ln_b, w_in, b_in,
           diff_lambda, diff_norm_g, rel_bias, w_cmp, w_gate, b_gate, w_branch, w_o, w_router, b_router,
           w_gu, b_gu, w_dn, b_dn):
    bp = x_prompt.shape[0]
    bs = x_sample.shape[0]
    past_len = page_table.shape[1] * PAGE_SIZE

    o_f = MIX_W + KV_W
    o_g = D_IN - 3 * N_HEADS

    def reorder(w):
        pad = jnp.zeros(w.shape[:-1] + (PROJ_W - D_IN,), w.dtype)
        return jnp.concatenate([w[..., :o_f], w[..., o_f + N_HEADS:o_g], w[..., o_f:o_f + N_HEADS], w[..., o_g:], pad], -1)

    P = dict(w_in_r=reorder(w_in), b_in_r=reorder(b_in).reshape(DEPTH, 1, PROJ_W), diff_lambda=diff_lambda,
             diff_norm_g=diff_norm_g, rel_bias=rel_bias, w_cmp=w_cmp, w_gate=w_gate, b_gate=b_gate, w_branch=w_branch,
             w_o=w_o, zero_bias=jnp.zeros((DEPTH, 1, D_MODEL), F32), w_router=w_router, b_router=b_router,
             w_gu=w_gu, b_gu=b_gu, w_dn=w_dn, b_dn=b_dn, ln_g=ln_g, ln_b=ln_b)

    r_pad = -(-(bp + bs) // 8) * 8
    c_all = jnp.pad(jnp.concatenate([c_prompt, c_sample], 0), ((0, r_pad - bp - bs), (0, 0)))
    mod_all = _ada_mod(c_all, w_ada, b_ada).reshape(DEPTH, r_pad, 6, D_MODEL)
    mod_p = mod_all[:, :bp]
    mod_s = mod_all[:, bp:bp + bs]

    n_off = x_prompt.shape[1] // TB
    tabs = _t5_tiles(rel_bias, n_off)
    y_prompt, sp = _run_group(x_prompt, mod_p, P, lambda l, h: _mixers_prompt(l, h, P, tabs))

    caches = dict(fox_kv=cache_fox_kv, fox_logf=cache_fox_logf, diff_kv=cache_diff_kv, moba_kv=cache_moba_kv,
                  nsa_cmp_kv=cache_nsa_cmp_kv, nsa_slc_kv=cache_nsa_slc_kv, nsa_win_kv=cache_nsa_win_kv)
    tables = _sample_tables(rel_bias, x_sample.shape[1], page_table.shape[1], cache_nsa_win_kv.shape[2], past_len)
    y_sample, ss = _run_group(x_sample, mod_s, P,
                              lambda l, h: _mixers_sample(l, h, P, caches, page_table, past_len, tables))
    return (y_prompt, y_sample,
            sp["fox_kv"], sp["fox_logf"], sp["diff_kv"], sp["moba_kv"], sp["nsa_cmp_kv"], sp["nsa_slc_kv"], sp["nsa_win_kv"],
            ss["fox_kv"], ss["fox_logf"], ss["diff_kv"], ss["moba_kv"], ss["nsa_cmp_kv"], ss["nsa_slc_kv"], ss["nsa_win_kv"])
```
